```python
import jax
import jax.numpy as jnp
from jax import lax
import numpy as np

D_MODEL = 2048
BATCH = 2
SEQ = 4096
DEPTH = 4

GRID_W = 64
CTX_LEN = 256
N_EVEN = (DEPTH + 1) // 2
N_ODD = DEPTH // 2
EPS = 1e-6
ROPE_THETA = 10000.0
Q_BLOCK = 128
N_MOD = 6

MLA_HEADS = 8
MLA_NOPE = 128
MLA_ROPE = 64
MLA_QK = MLA_NOPE + MLA_ROPE
MLA_V = 128
Q_LORA = 512
KV_LORA = 256

CONV_WIDTH = 1024
CONV_K = 3

EVEN_IN = Q_LORA + KV_LORA + MLA_ROPE + 3 * CONV_WIDTH
MIX_WIDTH = MLA_HEADS * MLA_V + CONV_WIDTH

GQA_HEADS = 16
GQA_KV_HEADS = 4
GQA_GROUP = GQA_HEADS // GQA_KV_HEADS
GQA_HD = 128
GQA_Q_DIM = GQA_HEADS * GQA_HD
GQA_KV_DIM = GQA_KV_HEADS * GQA_HD
ODD_IN = GQA_Q_DIM + 2 * GQA_KV_DIM

PEER_HEADS = 8
N_KEYS = 128
N_EXPERTS = N_KEYS * N_KEYS
PEER_TOPK = 16
PEER_DKEY = 256
PEER_DHALF = PEER_DKEY // 2
PEER_CHUNK = 128

kernel_name = 'hybrid_mla_conv_gqa_peer_dit'


def rmsnorm(x, g):
    xf = x.astype(jnp.float32)
    y = xf * lax.rsqrt(jnp.mean(xf * xf, axis=-1, keepdims=True) + EPS)
    return y.astype(x.dtype) * g


def axial_rope_tables(n_tokens, rope_dim):
    rows = n_tokens // GRID_W
    row = jnp.repeat(jnp.arange(rows, dtype=jnp.float32), GRID_W)
    col = jnp.tile(jnp.arange(GRID_W, dtype=jnp.float32), rows)
    quarter = rope_dim // 4
    freqs = ROPE_THETA ** (-jnp.arange(quarter, dtype=jnp.float32) / quarter)
    ang = jnp.concatenate([row[:, None] * freqs, col[:, None] * freqs], axis=-1)
    return jnp.cos(ang), jnp.sin(ang)


def apply_rope(x, cos, sin):
    shp = (cos.shape[0],) + (1,) * (x.ndim - 3) + (cos.shape[1],)
    cos = cos.reshape(shp).astype(x.dtype)
    sin = sin.reshape(shp).astype(x.dtype)
    x1, x2 = jnp.split(x, 2, axis=-1)
    return jnp.concatenate([x1 * cos - x2 * sin, x2 * cos + x1 * sin], axis=-1)


def softmax_attend(q, k, v):
    s = jnp.einsum('bqhgd,bkhd->bhgqk', q, k).astype(jnp.float32) * (q.shape[-1] ** -0.5)
    p = jax.nn.softmax(s, axis=-1).astype(v.dtype)
    return jnp.einsum('bhgqk,bkhd->bqhgd', p, v)


def latent_attend(q, k_lat, v_lat, k_ctx, v_ctx):
    b, s = q.shape[:2]
    k = jnp.concatenate([k_ctx, k_lat], axis=1)
    v = jnp.concatenate([v_ctx, v_lat], axis=1)
    qb = q.reshape((b, s // Q_BLOCK, Q_BLOCK) + q.shape[2:]).swapaxes(0, 1)
    out = lax.map(lambda qi: softmax_attend(qi, k, v), qb).swapaxes(0, 1)
    return out.reshape((b, s) + out.shape[3:])


def short_conv(u, w, bias):
    n = u.shape[1]
    up = jnp.pad(u, ((0, 0), (1, 1), (0, 0)))
    return up[:, :n] * w[0] + up[:, 1:n + 1] * w[1] + up[:, 2:] * w[2] + bias


def gated_conv(z, w, bias):
    b_gate, c_gate, hz = jnp.split(z, 3, axis=-1)
    return b_gate * short_conv(c_gate * hz, w, bias)


def even_mixer(h_ctx, h_lat, w_in, q_lora_g, kv_lora_g, w_uq, w_ukv, q_g, k_g, conv_w, conv_b, w_o,
               cos, sin, need_ctx):
    cuts = [Q_LORA, Q_LORA + KV_LORA, Q_LORA + KV_LORA + MLA_ROPE]

    def mla_q(cq, rope):
        b, n = cq.shape[:2]
        q = rmsnorm((rmsnorm(cq, q_lora_g) @ w_uq).reshape(b, n, MLA_HEADS, MLA_QK), q_g)
        if rope:
            q = jnp.concatenate([q[..., :MLA_NOPE], apply_rope(q[..., MLA_NOPE:], cos, sin)], axis=-1)
        return q[:, :, :, None, :]

    def mla_kv(ckv, kr, rope):
        b, n = ckv.shape[:2]
        kv = (rmsnorm(ckv, kv_lora_g) @ w_ukv).reshape(b, n, MLA_HEADS, MLA_NOPE + MLA_V)
        k_nope, v = jnp.split(kv, [MLA_NOPE], axis=-1)
        k_rope = jnp.broadcast_to(kr[:, :, None, :], (b, n, MLA_HEADS, MLA_ROPE))
        k = rmsnorm(jnp.concatenate([k_nope, k_rope], axis=-1), k_g)
        if rope:
            k = jnp.concatenate([k[..., :MLA_NOPE], apply_rope(k[..., MLA_NOPE:], cos, sin)], axis=-1)
        return k, v

    b, s = h_lat.shape[:2]
    cq, ckv, kr, z = jnp.split(h_lat @ w_in, cuts, axis=-1)
    k_lat, v_lat = mla_kv(ckv, kr, True)
    if need_ctx:
        cq_c, ckv_c, kr_c, z_c = jnp.split(h_ctx @ w_in, cuts, axis=-1)
    else:
        ckv_c, kr_c = jnp.split(h_ctx @ w_in[:, cuts[0]:cuts[2]], [KV_LORA], axis=-1)
    k_ctx, v_ctx = mla_kv(ckv_c, kr_c, False)
    a_lat = latent_attend(mla_q(cq, True), k_lat, v_lat, k_ctx, v_ctx).reshape(b, s, MLA_HEADS * MLA_V)
    y_lat = jnp.concatenate([a_lat, gated_conv(z, conv_w, conv_b)], axis=-1) @ w_o
    y_ctx = None
    if need_ctx:
        a_ctx = softmax_attend(mla_q(cq_c, False), k_ctx, v_ctx).reshape(b, h_ctx.shape[1], MLA_HEADS * MLA_V)
        y_ctx = jnp.concatenate([a_ctx, gated_conv(z_c, conv_w, conv_b)], axis=-1) @ w_o
    return y_ctx, y_lat


def odd_mixer(h_ctx, h_lat, w_qkv, q_g, k_g, w_o, cos, sin, need_ctx):
    cuts = [GQA_Q_DIM, GQA_Q_DIM + GQA_KV_DIM]

    def heads_q(pq, rope):
        b, n = pq.shape[:2]
        q = rmsnorm(pq.reshape(b, n, GQA_KV_HEADS, GQA_GROUP, GQA_HD), q_g)
        return apply_rope(q, cos, sin) if rope else q

    def heads_kv(pk, pv, rope):
        b, n = pk.shape[:2]
        k = rmsnorm(pk.reshape(b, n, GQA_KV_HEADS, GQA_HD), k_g)
        v = pv.reshape(b, n, GQA_KV_HEADS, GQA_HD)
        return (apply_rope(k, cos, sin) if rope else k), v

    b, s = h_lat.shape[:2]
    pq, pk, pv = jnp.split(h_lat @ w_qkv, cuts, axis=-1)
    k_lat, v_lat = heads_kv(pk, pv, True)
    if need_ctx:
        pq_c, pk_c, pv_c = jnp.split(h_ctx @ w_qkv, cuts, axis=-1)
    else:
        pk_c, pv_c = jnp.split(h_ctx @ w_qkv[:, GQA_Q_DIM:], [GQA_KV_DIM], axis=-1)
    k_ctx, v_ctx = heads_kv(pk_c, pv_c, False)
    y_lat = latent_attend(heads_q(pq, True), k_lat, v_lat, k_ctx, v_ctx).reshape(b, s, GQA_Q_DIM) @ w_o
    y_ctx = None
    if need_ctx:
        y_ctx = softmax_attend(heads_q(pq_c, False), k_ctx, v_ctx).reshape(b, h_ctx.shape[1], GQA_Q_DIM) @ w_o
    return y_ctx, y_lat


def peer_ffn(h, w_q, sub_keys, u_tab, v_tab):
    shape = h.shape
    chunks = h.reshape(-1, PEER_CHUNK, shape[-1])
    kk = PEER_TOPK * PEER_TOPK

    def one(xi):
        q = (xi @ w_q).reshape(PEER_CHUNK, PEER_HEADS, 2, PEER_DHALF)
        s = jnp.einsum('thpd,hpnd->thpn', q, sub_keys).astype(jnp.float32)
        s_top, i_top = lax.top_k(s, PEER_TOPK)
        cand_s = (s_top[:, :, 0, :, None] + s_top[:, :, 1, None, :]).reshape(PEER_CHUNK, PEER_HEADS, kk)
        cand_i = (i_top[:, :, 0, :, None] * N_KEYS + i_top[:, :, 1, None, :]).reshape(PEER_CHUNK, PEER_HEADS, kk)
        best_s, best_j = lax.top_k(cand_s, PEER_TOPK)
        idx = jnp.take_along_axis(cand_i, best_j, axis=-1)
        g = jax.nn.softmax(best_s, axis=-1).astype(xi.dtype)
        a = jax.nn.gelu(jnp.einsum('thkd,td->thk', u_tab[idx], xi), approximate=False)
        return jnp.einsum('thk,thkd->td', g * a, v_tab[idx])

    return lax.map(one, chunks).reshape(shape)


def setup_inputs(seed: int = 0) -> dict:
    key = jax.random.key(seed)
    ks = iter(jax.random.split(key, 32))

    def nrm(shape, std):
        return jax.random.normal(next(ks), shape, jnp.float32) * std

    def gain(shape):
        return 1.0 + nrm(shape, 0.01)

    d = D_MODEL
    return {
        'x': nrm((BATCH, SEQ, d), 1.0),
        'c': nrm((BATCH, d), 1.0),
        'ctx': nrm((BATCH, CTX_LEN, d), 1.0),
        'c_ctx': nrm((d,), 1.0),
        'ada_w': nrm((DEPTH, d, N_MOD * d), d ** -0.5),
        'ada_b': nrm((DEPTH, N_MOD * d), 0.01),
        'norm1_g': gain((DEPTH, d)),
        'norm2_g': gain((DEPTH, d)),
        'a_w_in': nrm((N_EVEN, d, EVEN_IN), d ** -0.5),
        'a_q_lora_g': gain((N_EVEN, Q_LORA)),
        'a_kv_lora_g': gain((N_EVEN, KV_LORA)),
        'a_w_uq': nrm((N_EVEN, Q_LORA, MLA_HEADS * MLA_QK), Q_LORA ** -0.5),
        'a_w_ukv': nrm((N_EVEN, KV_LORA, MLA_HEADS * (MLA_NOPE + MLA_V)), KV_LORA ** -0.5),
        'a_q_g': gain((N_EVEN, MLA_QK)),
        'a_k_g': gain((N_EVEN, MLA_QK)),
        'b_conv_w': nrm((N_EVEN, CONV_K, CONV_WIDTH), CONV_K ** -0.5),
        'b_conv_b': nrm((N_EVEN, CONV_WIDTH), 0.01),
        'e_w_o': nrm((N_EVEN, MIX_WIDTH, d), MIX_WIDTH ** -0.5),
        'c_w_qkv': nrm((N_ODD, d, ODD_IN), d ** -0.5),
        'c_q_g': gain((N_ODD, GQA_HD)),
        'c_k_g': gain((N_ODD, GQA_HD)),
        'c_w_o': nrm((N_ODD, GQA_Q_DIM, d), GQA_Q_DIM ** -0.5),
        'p_w_q': nrm((DEPTH, d, PEER_HEADS * PEER_DKEY), d ** -0.5),
        'p_sub_keys': nrm((DEPTH, PEER_HEADS, 2, N_KEYS, PEER_DHALF), PEER_DHALF ** -0.5),
        'p_u': nrm((DEPTH, N_EXPERTS, d), d ** -0.5),
        'p_v': nrm((DEPTH, N_EXPERTS, d), PEER_HEADS ** -0.5),
    }


def reference(x, c, ctx, c_ctx, ada_w, ada_b, norm1_g, norm2_g, a_w_in, a_q_lora_g, a_kv_lora_g, a_w_uq,
              a_w_ukv, a_q_g, a_k_g, b_conv_w, b_conv_b, e_w_o, c_w_qkv, c_q_g, c_k_g, c_w_o, p_w_q,
              p_sub_keys, p_u, p_v):
    s = x.shape[1]
    cos_a, sin_a = axial_rope_tables(s, MLA_ROPE)
    cos_c, sin_c = axial_rope_tables(s, GQA_HD)
    silu_lat = jax.nn.silu(c)
    silu_ctx = jax.nn.silu(c_ctx)
    xc = ctx
    for l in range(DEPTH):
        need_ctx = l < DEPTH - 1
        sh1, sc1, g1, sh2, sc2, g2 = jnp.split((silu_lat @ ada_w[l] + ada_b[l])[:, None, :], N_MOD, axis=-1)
        csh1, csc1, cg1, csh2, csc2, cg2 = jnp.split((silu_ctx @ ada_w[l] + ada_b[l])[None, None, :], N_MOD, axis=-1)
        h_lat = rmsnorm(x, norm1_g[l]) * (1 + sc1) + sh1
        h_ctx = rmsnorm(xc, norm1_g[l]) * (1 + csc1) + csh1
        i = l // 2
        if l % 2 == 0:
            y_ctx, y_lat = even_mixer(h_ctx, h_lat, a_w_in[i], a_q_lora_g[i], a_kv_lora_g[i], a_w_uq[i],
                                      a_w_ukv[i], a_q_g[i], a_k_g[i], b_conv_w[i], b_conv_b[i], e_w_o[i],
                                      cos_a, sin_a, need_ctx)
        else:
            y_ctx, y_lat = odd_mixer(h_ctx, h_lat, c_w_qkv[i], c_q_g[i], c_k_g[i], c_w_o[i],
                                     cos_c, sin_c, need_ctx)
        x = x + g1 * y_lat
        h_lat = rmsnorm(x, norm2_g[l]) * (1 + sc2) + sh2
        x = x + g2 * peer_ffn(h_lat, p_w_q[l], p_sub_keys[l], p_u[l], p_v[l])
        if need_ctx:
            xc = xc + cg1 * y_ctx
            h_ctx = rmsnorm(xc, norm2_g[l]) * (1 + csc2) + csh2
            xc = xc + cg2 * peer_ffn(h_ctx, p_w_q[l], p_sub_keys[l], p_u[l], p_v[l])
    return x
```

```python
import functools
import math

import jax
import jax.numpy as jnp
from jax import lax
from jax.experimental import pallas as pl
from jax.experimental.pallas import tpu as pltpu

F32 = jnp.float32
BF16 = jnp.bfloat16

EPS = 1e-6
ROPE_THETA = 10000.0
GRID_W = 64
MLA_HEADS = 8
MLA_NOPE = 128
MLA_ROPE = 64
MLA_V = 128
GQA_HEADS = 16
GQA_KV_HEADS = 4
GQA_HD = 128
PEER_HEADS = 8
N_KEYS = 128
PEER_TOPK = 16
N_MOD = 6

V7X_LANES = 128
V7X_SUBLANES = 8
V7X_VMEM_LIMIT_BYTES = 56 * 1024 * 1024

ROW_BLOCK = 512
ATT_Q_BLOCK = 256
PEER_TB = 512
PEER_EB = 512
SQRT_HALF = 0.7071067811865476
NEG_INF = float("-inf")


def _cparams(sem):
    return pltpu.CompilerParams(dimension_semantics=sem, vmem_limit_bytes=V7X_VMEM_LIMIT_BYTES)


def _nt_dot(a, b):
    return lax.dot_general(a, b, (((1,), (1,)), ((), ())), preferred_element_type=F32)


def _ada_kernel(c_ref, w_ref, b_ref, o_ref):
    c = c_ref[...]
    s = c / (1.0 + jnp.exp(-c))
    o_ref[0] = jnp.dot(s.astype(BF16), w_ref[0].astype(BF16), preferred_element_type=F32) + b_ref[0]


def _ada(cvec, ada_w, ada_b):
    depth, d, n = ada_w.shape
    nb = 1024
    rows = cvec.shape[0]
    return pl.pallas_call(
        _ada_kernel,
        grid=(depth, n // nb),
        in_specs=[
            pl.BlockSpec((rows, d), lambda l, j: (0, 0)),
            pl.BlockSpec((1, d, nb), lambda l, j: (l, 0, j)),
            pl.BlockSpec((1, 1, nb), lambda l, j: (l, 0, j)),
        ],
        out_specs=pl.BlockSpec((1, rows, nb), lambda l, j: (l, 0, j)),
        out_shape=jax.ShapeDtypeStruct((depth, rows, n), F32),
        compiler_params=_cparams(("parallel", "parallel")),
        name="ada_mod",
    )(cvec, ada_w, ada_b.reshape(depth, 1, n))


def _normproj_kernel(x_ref, g_ref, sc_ref, sh_ref, w_ref, o_ref, *rest, emit_h):
    hs_ref = rest[-1]

    @pl.when(pl.program_id(1) == 0)
    def _():
        x = x_ref[...]
        y = x * lax.rsqrt(jnp.mean(x * x, axis=-1, keepdims=True) + EPS) * g_ref[...]
        h = (y * (1.0 + sc_ref[0]) + sh_ref[0]).astype(BF16)
        hs_ref[...] = h
        if emit_h:
            rest[0][...] = h

    o_ref[...] = jnp.dot(hs_ref[...], w_ref[...], preferred_element_type=F32)


def _normproj(x, g, sc, sh, w, group_of, emit_h):
    t, d = x.shape
    n = w.shape[1]
    bm, bn = ROW_BLOCK, 512
    out_shape = [jax.ShapeDtypeStruct((t, n), F32)]
    out_specs = [pl.BlockSpec((bm, bn), lambda i, j: (i, j))]
    if emit_h:
        out_shape.append(jax.ShapeDtypeStruct((t, d), BF16))
        out_specs.append(pl.BlockSpec((bm, d), lambda i, j: (i, 0)))
    res = pl.pallas_call(
        functools.partial(_normproj_kernel, emit_h=emit_h),
        grid=(t // bm, n // bn),
        in_specs=[
            pl.BlockSpec((bm, d), lambda i, j: (i, 0)),
            pl.BlockSpec((1, d), lambda i, j: (0, 0)),
            pl.BlockSpec((1, 1, d), lambda i, j: (group_of(i), 0, 0)),
            pl.BlockSpec((1, 1, d), lambda i, j: (group_of(i), 0, 0)),
            pl.BlockSpec((d, bn), lambda i, j: (0, j)),
        ],
        out_specs=out_specs,
        out_shape=out_shape,
        scratch_shapes=[pltpu.VMEM((bm, d), BF16)],
        compiler_params=_cparams(("parallel", "arbitrary")),
        name="normproj_h" if emit_h else "normproj",
    )(x, g.reshape(1, d), sc, sh, w)
    return res if emit_h else res[0]


def _mla_qkv_kernel(p_ref, cos_ref, sin_ref, wuq_ref, wukv_ref, qlg_ref, kvlg_ref,
                    qgn_ref, qgr_ref, qgs_ref, kgn_ref, kgr_ref, kgs_ref,
                    q_ref, k_ref, v_ref, *, q_lora, kv_lora, scale):
    qk_dim = MLA_NOPE + MLA_ROPE
    lanes = V7X_LANES
    cq = p_ref[:, 0:q_lora]
    ckv = p_ref[:, q_lora:q_lora + kv_lora]
    kr = p_ref[:, q_lora + kv_lora:q_lora + kv_lora + lanes]
    krs = p_ref[:, q_lora + kv_lora + lanes:q_lora + kv_lora + 2 * lanes]
    cos = cos_ref[...]
    sin = sin_ref[...]

    cqn = cq * lax.rsqrt(jnp.mean(cq * cq, axis=-1, keepdims=True) + EPS) * qlg_ref[...]
    qraw = jnp.dot(cqn.astype(BF16), wuq_ref[...], preferred_element_type=F32)
    ckvn = ckv * lax.rsqrt(jnp.mean(ckv * ckv, axis=-1, keepdims=True) + EPS) * kvlg_ref[...]
    kvraw = jnp.dot(ckvn.astype(BF16), wukv_ref[...], preferred_element_type=F32)

    kr_rot = kr * kgr_ref[...] * cos + krs * kgs_ref[...] * sin
    kr_ssq = jnp.sum(kr * kr, axis=-1, keepdims=True)
    for h in range(MLA_HEADS):
        nope = qraw[:, h * 3 * lanes:h * 3 * lanes + lanes]
        rope = qraw[:, h * 3 * lanes + lanes:h * 3 * lanes + 2 * lanes]
        rope_sw = qraw[:, h * 3 * lanes + 2 * lanes:h * 3 * lanes + 3 * lanes]
        ssq = jnp.sum(nope * nope, axis=-1, keepdims=True) + jnp.sum(rope * rope, axis=-1, keepdims=True)
        r = lax.rsqrt(ssq * (1.0 / qk_dim) + EPS) * scale
        q_ref[:, h * 2 * lanes:h * 2 * lanes + lanes] = (nope * r * qgn_ref[...]).astype(BF16)
        q_ref[:, h * 2 * lanes + lanes:(h + 1) * 2 * lanes] = (
            (rope * qgr_ref[...] * cos + rope_sw * qgs_ref[...] * sin) * r).astype(BF16)

        k_nope = kvraw[:, h * 2 * lanes:h * 2 * lanes + lanes]
        v = kvraw[:, h * 2 * lanes + lanes:(h + 1) * 2 * lanes]
        kssq = jnp.sum(k_nope * k_nope, axis=-1, keepdims=True) + kr_ssq
        rk = lax.rsqrt(kssq * (1.0 / qk_dim) + EPS)
        k_ref[:, h * 2 * lanes:h * 2 * lanes + lanes] = (k_nope * rk * kgn_ref[...]).astype(BF16)
        k_ref[:, h * 2 * lanes + lanes:(h + 1) * 2 * lanes] = (kr_rot * rk).astype(BF16)
        v_ref[:, h * lanes:(h + 1) * lanes] = v.astype(BF16)


def _mla_qkv(p, cos, sin, wuq, wukv, gains, q_lora, kv_lora):
    t = p.shape[0]
    bm = ROW_BLOCK
    lanes = V7X_LANES
    head_cols = q_lora + kv_lora + 2 * lanes
    full = lambda a: pl.BlockSpec(a.shape, lambda i: (0,) * a.ndim)
    return pl.pallas_call(
        functools.partial(_mla_qkv_kernel, q_lora=q_lora, kv_lora=kv_lora,
                          scale=float(MLA_NOPE + MLA_ROPE) ** -0.5),
        grid=(t // bm,),
        in_specs=[
            pl.BlockSpec((bm, head_cols), lambda i: (i, 0)),
            pl.BlockSpec((bm, lanes), lambda i: (i, 0)),
            pl.BlockSpec((bm, lanes), lambda i: (i, 0)),
            full(wuq), full(wukv)] + [full(g) for g in gains],
        out_specs=[
            pl.BlockSpec((bm, MLA_HEADS * 2 * lanes), lambda i: (i, 0)),
            pl.BlockSpec((bm, MLA_HEADS * 2 * lanes), lambda i: (i, 0)),
            pl.BlockSpec((bm, MLA_HEADS * lanes), lambda i: (i, 0)),
        ],
        out_shape=[
            jax.ShapeDtypeStruct((t, MLA_HEADS * 2 * lanes), BF16),
            jax.ShapeDtypeStruct((t, MLA_HEADS * 2 * lanes), BF16),
            jax.ShapeDtypeStruct((t, MLA_HEADS * lanes), BF16),
        ],
        compiler_params=_cparams(("parallel",)),
        name="mla_qkv",
    )(p, cos, sin, wuq, wukv, *gains)


def _gqa_qkv_kernel(p_ref, cos_ref, sin_ref, qg_ref, kg_ref, q_ref, k_ref, v_ref, *, scale):
    hd = GQA_HD
    cos = cos_ref[...]
    sin = sin_ref[...]

    def head(x, g):
        y = x * lax.rsqrt(jnp.mean(x * x, axis=-1, keepdims=True) + EPS) * g
        return y * cos + pltpu.roll(y, hd // 2, 1) * sin

    for h in range(GQA_HEADS):
        q_ref[:, h * hd:(h + 1) * hd] = (head(p_ref[:, h * hd:(h + 1) * hd], qg_ref[...]) * scale).astype(BF16)
    k0 = GQA_HEADS * hd
    v0 = k0 + GQA_KV_HEADS * hd
    for h in range(GQA_KV_HEADS):
        k_ref[:, h * hd:(h + 1) * hd] = head(p_ref[:, k0 + h * hd:k0 + (h + 1) * hd], kg_ref[...]).astype(BF16)
    v_ref[...] = p_ref[:, v0:v0 + GQA_KV_HEADS * hd].astype(BF16)


def _gqa_qkv(p, cos, sin, qg, kg):
    t, n = p.shape
    bm = ROW_BLOCK
    hd = GQA_HD
    return pl.pallas_call(
        functools.partial(_gqa_qkv_kernel, scale=float(hd) ** -0.5),
        grid=(t // bm,),
        in_specs=[
            pl.BlockSpec((bm, n), lambda i: (i, 0)),
            pl.BlockSpec((bm, hd), lambda i: (i, 0)),
            pl.BlockSpec((bm, hd), lambda i: (i, 0)),
            pl.BlockSpec((1, hd), lambda i: (0, 0)),
            pl.BlockSpec((1, hd), lambda i: (0, 0)),
        ],
        out_specs=[
            pl.BlockSpec((bm, GQA_HEADS * hd), lambda i: (i, 0)),
            pl.BlockSpec((bm, GQA_KV_HEADS * hd), lambda i: (i, 0)),
            pl.BlockSpec((bm, GQA_KV_HEADS * hd), lambda i: (i, 0)),
        ],
        out_shape=[
            jax.ShapeDtypeStruct((t, GQA_HEADS * hd), BF16),
            jax.ShapeDtypeStruct((t, GQA_KV_HEADS * hd), BF16),
            jax.ShapeDtypeStruct((t, GQA_KV_HEADS * hd), BF16),
        ],
        compiler_params=_cparams(("parallel",)),
        name="gqa_qkv",
    )(p, cos, sin, qg.reshape(1, hd), kg.reshape(1, hd))


def _attn_kernel(q_ref, kc_ref, vc_ref, *rest, group, dk, dv, with_latent):
    if with_latent:
        kl_ref, vl_ref, o_ref = rest
    else:
        (o_ref,) = rest
    for g in range(group):
        q = q_ref[:, g * dk:(g + 1) * dk]
        sc = _nt_dot(q, kc_ref[...])
        m = jnp.max(sc, axis=-1, keepdims=True)
        if with_latent:
            sl = _nt_dot(q, kl_ref[...])
            m = jnp.maximum(m, jnp.max(sl, axis=-1, keepdims=True))
        pc = jnp.exp(sc - m)
        den = jnp.sum(pc, axis=-1, keepdims=True)
        o = jnp.dot(pc.astype(BF16), vc_ref[...], preferred_element_type=F32)
        if with_latent:
            pl_ = jnp.exp(sl - m)
            den = den + jnp.sum(pl_, axis=-1, keepdims=True)
            o = o + jnp.dot(pl_.astype(BF16), vl_ref[...], preferred_element_type=F32)
        o_ref[:, g * dv:(g + 1) * dv] = (o / den).astype(BF16)


def _attention(q, k, v, *, batch, seq, ctx_len, kv_heads, group, dk, dv):
    t_lat = batch * seq
    tq = ATT_Q_BLOCK
    nq = seq // tq
    ctx_blk0 = t_lat // ctx_len
    common = dict(group=group, dk=dk, dv=dv)
    lat = pl.pallas_call(
        functools.partial(_attn_kernel, with_latent=True, **common),
        grid=(batch, kv_heads, nq),
        in_specs=[
            pl.BlockSpec((tq, group * dk), lambda b, h, j: (b * nq + j, h)),
            pl.BlockSpec((ctx_len, dk), lambda b, h, j: (ctx_blk0 + b, h)),
            pl.BlockSpec((ctx_len, dv), lambda b, h, j: (ctx_blk0 + b, h)),
            pl.BlockSpec((seq, dk), lambda b, h, j: (b, h)),
            pl.BlockSpec((seq, dv), lambda b, h, j: (b, h)),
        ],
        out_specs=pl.BlockSpec((tq, group * dv), lambda b, h, j: (b * nq + j, h)),
        out_shape=jax.ShapeDtypeStruct((t_lat, kv_heads * group * dv), BF16),
        compiler_params=_cparams(("parallel", "parallel", "arbitrary")),
        name="attn_latent",
    )(q, k, v, k, v)
    ctx = pl.pallas_call(
        functools.partial(_attn_kernel, with_latent=False, **common),
        grid=(batch, kv_heads),
        in_specs=[
            pl.BlockSpec((ctx_len, group * dk), lambda b, h: (ctx_blk0 + b, h)),
            pl.BlockSpec((ctx_len, dk), lambda b, h: (ctx_blk0 + b, h)),
            pl.BlockSpec((ctx_len, dv), lambda b, h: (ctx_blk0 + b, h)),
        ],
        out_specs=pl.BlockSpec((ctx_len, group * dv), lambda b, h: (b, h)),
        out_shape=jax.ShapeDtypeStruct((batch * ctx_len, kv_heads * group * dv), BF16),
        compiler_params=_cparams(("parallel", "parallel")),
        name="attn_ctx",
    )(q, k, v)
    return jnp.concatenate([lat, ctx], axis=0)


def _mixout_kernel(*refs, with_conv, t_lat, seq, ctx_len, a_width):
    if with_conv:
        (a_ref, bg_ref, cg_ref, hz_ref, cgp_ref, hzp_ref, cgn_ref, hzn_ref, cw_ref, cb_ref,
         w_ref, x_ref, gate_ref, o_ref, cs_ref) = refs
    else:
        a_ref, w_ref, x_ref, gate_ref, o_ref = refs
    i = pl.program_id(0)

    if with_conv:
        @pl.when(pl.program_id(1) == 0)
        def _():
            bm = cg_ref.shape[0]
            u = cg_ref[...] * hz_ref[...]
            u_before = cgp_ref[V7X_SUBLANES - 1:V7X_SUBLANES, :] * hzp_ref[V7X_SUBLANES - 1:V7X_SUBLANES, :]
            u_after = cgn_ref[0:1, :] * hzn_ref[0:1, :]
            local = lax.broadcasted_iota(jnp.int32, (bm, 1), 0)
            row = local + i * bm
            in_lat = row < t_lat
            seg_pos = jnp.where(in_lat, jnp.bitwise_and(row, seq - 1), jnp.bitwise_and(row - t_lat, ctx_len - 1))
            seg_len = jnp.where(in_lat, seq, ctx_len)
            up = jnp.where(local == 0, u_before, pltpu.roll(u, 1, 0))
            up = jnp.where(seg_pos == 0, 0.0, up)
            un = jnp.where(local == bm - 1, u_after, pltpu.roll(u, bm - 1, 0))
            un = jnp.where(seg_pos == seg_len - 1, 0.0, un)
            y = up * cw_ref[0:1, :] + u * cw_ref[1:2, :] + un * cw_ref[2:3, :] + cb_ref[...]
            cs_ref[...] = (bg_ref[...] * y).astype(BF16)

        acc = jnp.dot(a_ref[...], w_ref[0:a_width, :], preferred_element_type=F32)
        acc = acc + jnp.dot(cs_ref[...], w_ref[a_width:, :], preferred_element_type=F32)
    else:
        acc = jnp.dot(a_ref[...], w_ref[...], preferred_element_type=F32)
    o_ref[...] = x_ref[...] + gate_ref[0] * acc


def _mixout(a, w, x, gate, group_of, conv=None, *, t_lat, seq, ctx_len):
    t, d = x.shape
    bm, bn = ROW_BLOCK, 512
    a_width = a.shape[1]
    kw = dict(t_lat=t_lat, seq=seq, ctx_len=ctx_len, a_width=a_width)
    in_specs = [pl.BlockSpec((bm, a_width), lambda i, j: (i, 0))]
    args = [a]
    scratch = []
    if conv is not None:
        p, z_col0, cw, cb = conv
        cwid = cw.shape[1]
        assert z_col0 % cwid == 0
        zb = z_col0 // cwid
        sub = V7X_SUBLANES
        last_halo = t // sub - 1
        prev_idx = lambda i: jnp.maximum(i * (bm // sub) - 1, 0)
        next_idx = lambda i: jnp.minimum((i + 1) * (bm // sub), last_halo)
        in_specs += [
            pl.BlockSpec((bm, cwid), lambda i, j: (i, zb)),
            pl.BlockSpec((bm, cwid), lambda i, j: (i, zb + 1)),
            pl.BlockSpec((bm, cwid), lambda i, j: (i, zb + 2)),
            pl.BlockSpec((sub, cwid), lambda i, j: (prev_idx(i), zb + 1)),
            pl.BlockSpec((sub, cwid), lambda i, j: (prev_idx(i), zb + 2)),
            pl.BlockSpec((sub, cwid), lambda i, j: (next_idx(i), zb + 1)),
            pl.BlockSpec((sub, cwid), lambda i, j: (next_idx(i), zb + 2)),
            pl.BlockSpec(cw.shape, lambda i, j: (0, 0)),
            pl.BlockSpec((1, cwid), lambda i, j: (0, 0)),
        ]
        args += [p] * 7 + [cw, cb.reshape(1, cwid)]
        scratch = [pltpu.VMEM((bm, cwid), BF16)]
    in_specs += [
        pl.BlockSpec((w.shape[0], bn), lambda i, j: (0, j)),
        pl.BlockSpec((bm, bn), lambda i, j: (i, j)),
        pl.BlockSpec((1, 1, bn), lambda i, j: (group_of(i), 0, j)),
    ]
    args += [w, x, gate]
    return pl.pallas_call(
        functools.partial(_mixout_kernel, with_conv=conv is not None, **kw),
        grid=(t // bm, d // bn),
        in_specs=in_specs,
        out_specs=pl.BlockSpec((bm, bn), lambda i, j: (i, j)),
        out_shape=jax.ShapeDtypeStruct((t, d), F32),
        scratch_shapes=scratch,
        compiler_params=_cparams(("parallel", "arbitrary")),
        name="mixout_conv" if conv is not None else "mixout",
    )(*args)


def _route_kernel(q_ref, sk_ref, s0_ref, s1_ref, p0_ref, p1_ref, tau_ref, cur_ref, top_ref, cand_ref, sel_ref):
    nk = N_KEYS
    k_top = PEER_TOPK
    s_refs = (s0_ref, s1_ref)
    for half in range(2):
        qh = q_ref[:, half * nk:(half + 1) * nk].astype(BF16)
        s = _nt_dot(sk_ref[0, half], qh)
        s_refs[half][0] = s
        cur_ref[...] = s

        def take_max(k, carry, half=half):
            cur = cur_ref[...]
            m = jnp.max(cur, axis=0, keepdims=True)
            top_ref[half, pl.ds(k, 1), :] = m
            cur_ref[...] = jnp.where(cur == m, NEG_INF, cur)
            return carry

        lax.fori_loop(0, k_top, take_max, 0)

    v1 = top_ref[1]
    for a in range(k_top):
        cand_ref[a * k_top:(a + 1) * k_top, :] = top_ref[0, a:a + 1, :] + v1

    def take_cand(k, carry):
        cur = cand_ref[...]
        m = jnp.max(cur, axis=0, keepdims=True)
        sel_ref[pl.ds(k, 1), :] = m
        cand_ref[...] = jnp.where(cur == m, NEG_INF, cur)
        return carry

    lax.fori_loop(0, k_top, take_cand, 0)

    sel = sel_ref[...]
    tau_ref[0] = sel[k_top - 1:k_top, :]
    m0 = top_ref[0, 0:1, :]
    m1 = top_ref[1, 0:1, :]
    z = jnp.sum(jnp.exp(sel - (m0 + m1)), axis=0, keepdims=True)
    p0_ref[0] = jnp.exp(s0_ref[0] - m0)
    p1_ref[0] = jnp.exp(s1_ref[0] - m1) / z


def _route(qp, sub_keys):
    t = qp.shape[0]
    tb = PEER_TB
    nk = N_KEYS
    heads = PEER_HEADS
    big = jax.ShapeDtypeStruct((heads, nk, t), F32)
    big_spec = pl.BlockSpec((1, nk, tb), lambda i, h: (h, 0, i))
    return pl.pallas_call(
        _route_kernel,
        grid=(t // tb, heads),
        in_specs=[
            pl.BlockSpec((tb, 2 * nk), lambda i, h: (i, h)),
            pl.BlockSpec((1, 2, nk, sub_keys.shape[-1]), lambda i, h: (h, 0, 0, 0)),
        ],
        out_specs=[big_spec, big_spec, big_spec, big_spec, pl.BlockSpec((1, 1, tb), lambda i, h: (h, 0, i))],
        out_shape=[big, big, big, big, jax.ShapeDtypeStruct((heads, 1, t), F32)],
        scratch_shapes=[
            pltpu.VMEM((nk, tb), F32),
            pltpu.VMEM((2, PEER_TOPK, tb), F32),
            pltpu.VMEM((PEER_TOPK * PEER_TOPK, tb), F32),
            pltpu.VMEM((PEER_TOPK, tb), F32),
        ],
        compiler_params=_cparams(("parallel", "arbitrary")),
        name="peer_route",
    )(qp, sub_keys)


def _peer_kernel(h_ref, u_ref, vt_ref, s0_ref, s1_ref, p0_ref, p1_ref, tau_ref, x_ref, gate_ref,
                 o_ref, acc_ref, at_ref, hs_ref):
    j = pl.program_id(1)
    nk = N_KEYS
    lanes = V7X_LANES
    eb, tb = at_ref.shape

    @pl.when(j == 0)
    def _():
        acc_ref[...] = jnp.zeros_like(acc_ref)

    at_ref[...] = _nt_dot(u_ref[...], h_ref[...])
    rows = eb // nk
    sub = V7X_SUBLANES
    assert rows <= sub and sub % rows == 0
    grp0 = pl.multiple_of((j * rows // sub) * sub, sub)
    phase = (j * rows) % sub

    def pick_row(grp, il):
        row = grp[il:il + 1, :]
        for k in range(1, sub // rows):
            row = jnp.where(phase == k * rows, grp[k * rows + il:k * rows + il + 1, :], row)
        return row

    for il in range(rows):
        for c in range(tb // lanes):
            cs = slice(c * lanes, (c + 1) * lanes)
            w = jnp.zeros((nk, lanes), F32)
            for hd in range(PEER_HEADS):
                s0row = pick_row(s0_ref[hd, pl.ds(grp0, sub), cs], il)
                p0row = pick_row(p0_ref[hd, pl.ds(grp0, sub), cs], il)
                hit = (s0row + s1_ref[hd, :, cs]) >= tau_ref[hd, :, cs]
                w = w + jnp.where(hit, p0row * p1_ref[hd, :, cs], 0.0)
            a = at_ref[il * nk:(il + 1) * nk, cs]
            gelu = 0.5 * a * (1.0 + lax.erf(a * SQRT_HALF))
            hs_ref[il * nk:(il + 1) * nk, cs] = (w * gelu).astype(BF16)
    acc_ref[...] += jnp.dot(vt_ref[...], hs_ref[...], preferred_element_type=F32)

    @pl.when(j == pl.num_programs(1) - 1)
    def _():
        o_ref[...] = x_ref[...] + gate_ref[0] * acc_ref[...].T


def _peer(h, u, vt, s0, s1, p0, p1, tau, x, gate, group_of):
    t, d = x.shape
    ne = u.shape[0]
    tb, eb = PEER_TB, PEER_EB
    nk = N_KEYS
    heads = PEER_HEADS
    route_spec = pl.BlockSpec((heads, nk, tb), lambda i, j: (0, 0, i))
    return pl.pallas_call(
        _peer_kernel,
        grid=(t // tb, ne // eb),
        in_specs=[
            pl.BlockSpec((tb, d), lambda i, j: (i, 0)),
            pl.BlockSpec((eb, d), lambda i, j: (j, 0)),
            pl.BlockSpec((d, eb), lambda i, j: (0, j)),
            route_spec, route_spec, route_spec, route_spec,
            pl.BlockSpec((heads, 1, tb), lambda i, j: (0, 0, i)),
            pl.BlockSpec((tb, d), lambda i, j: (i, 0)),
            pl.BlockSpec((1, 1, d), lambda i, j: (group_of(i), 0, 0)),
        ],
        out_specs=pl.BlockSpec((tb, d), lambda i, j: (i, 0)),
        out_shape=jax.ShapeDtypeStruct((t, d), F32),
        scratch_shapes=[
            pltpu.VMEM((d, tb), F32),
            pltpu.VMEM((eb, tb), F32),
            pltpu.VMEM((eb, tb), BF16),
        ],
        compiler_params=_cparams(("parallel", "arbitrary")),
        name="peer_mix",
    )(h, u, vt, s0, s1, p0, p1, tau, x, gate)


def _rope_tables(seq, rope_dim, batch, ctx_rows):
    rows = seq // GRID_W
    row = jnp.repeat(jnp.arange(rows, dtype=F32), GRID_W)
    col = jnp.tile(jnp.arange(GRID_W, dtype=F32), rows)
    quarter = rope_dim // 4
    freqs = ROPE_THETA ** (-jnp.arange(quarter, dtype=F32) / quarter)
    ang = jnp.concatenate([row[:, None] * freqs, col[:, None] * freqs], axis=-1)
    cos, sin = jnp.cos(ang), jnp.sin(ang)
    pad = V7X_LANES - rope_dim
    cos_l = jnp.pad(jnp.concatenate([cos, cos], axis=-1), ((0, 0), (0, pad)))
    sin_l = jnp.pad(jnp.concatenate([-sin, sin], axis=-1), ((0, 0), (0, pad)))
    cos_c = jnp.pad(jnp.ones((ctx_rows, rope_dim), F32), ((0, 0), (0, pad)))
    sin_c = jnp.zeros((ctx_rows, V7X_LANES), F32)
    return (jnp.concatenate([jnp.tile(cos_l, (batch, 1)), cos_c], axis=0),
            jnp.concatenate([jnp.tile(sin_l, (batch, 1)), sin_c], axis=0))


def _swap_halves(a):
    half = a.shape[-1] // 2
    return jnp.concatenate([a[..., half:], a[..., :half]], axis=-1)


def _pad_lanes(a):
    return jnp.pad(a, [(0, 0)] * (a.ndim - 1) + [(0, V7X_LANES - a.shape[-1])])


def _mla_weights(w_in, w_uq, q_g, k_g, q_lora, kv_lora):
    d = w_in.shape[0]
    c_kr = q_lora + kv_lora
    w_kr = w_in[:, c_kr:c_kr + MLA_ROPE]
    w_in_ext = jnp.concatenate([
        w_in[:, :c_kr], _pad_lanes(w_kr), _pad_lanes(_swap_halves(w_kr)), w_in[:, c_kr + MLA_ROPE:]], axis=1)
    wq = w_uq.reshape(q_lora, MLA_HEADS, MLA_NOPE + MLA_ROPE)
    wq_rope = wq[:, :, MLA_NOPE:]
    wq_ext = jnp.concatenate([wq[:, :, :MLA_NOPE], _pad_lanes(wq_rope), _pad_lanes(_swap_halves(wq_rope))], axis=-1)
    wq_ext = wq_ext.reshape(q_lora, MLA_HEADS * 3 * V7X_LANES)

    def gains(g):
        g_r = g[MLA_NOPE:]
        return [g[:MLA_NOPE].reshape(1, -1), _pad_lanes(g_r).reshape(1, -1), _pad_lanes(_swap_halves(g_r)).reshape(1, -1)]

    return w_in_ext.astype(BF16), wq_ext.astype(BF16), gains(q_g) + gains(k_g)


def kernel(x, c, ctx, c_ctx, ada_w, ada_b, norm1_g, norm2_g, a_w_in, a_q_lora_g, a_kv_lora_g, a_w_uq, a_w_ukv, a_q_g, a_k_g, b_conv_w, b_conv_b, e_w_o, c_w_qkv, c_q_g, c_k_g, c_w_o, p_w_q, p_sub_keys, p_u, p_v):
    batch, seq, d = x.shape
    ctx_len = ctx.shape[1]
    depth = ada_w.shape[0]
    q_lora = a_q_lora_g.shape[1]
    kv_lora = a_kv_lora_g.shape[1]
    conv_width = b_conv_w.shape[2]
    t_lat = batch * seq
    t_ctx = batch * ctx_len
    t = t_lat + t_ctx
    bm = ROW_BLOCK
    assert seq % bm == 0 and t_ctx % bm == 0 and t % PEER_TB == 0 and seq % ATT_Q_BLOCK == 0
    assert seq & (seq - 1) == 0 and ctx_len & (ctx_len - 1) == 0 and seq % GRID_W == 0
    assert batch + 1 <= V7X_SUBLANES

    blocks_per_batch = seq // bm
    group_of = lambda i: jnp.minimum(i // blocks_per_batch, batch)

    cvec = jnp.zeros((V7X_SUBLANES, d), F32).at[:batch].set(c).at[batch].set(c_ctx)
    mods = _ada(cvec, ada_w, ada_b)
    mods = mods.reshape(depth, V7X_SUBLANES, N_MOD, 1, d).transpose(0, 2, 1, 3, 4)

    cos_a, sin_a = _rope_tables(seq, MLA_ROPE, batch, t_ctx)
    cos_c, sin_c = _rope_tables(seq, GQA_HD, batch, t_ctx)

    xs = jnp.concatenate([x.reshape(t_lat, d), ctx.reshape(t_ctx, d)], axis=0)
    for l in range(depth):
        sh1, sc1, g1, sh2, sc2, g2 = (mods[l, k] for k in range(N_MOD))
        i = l // 2
        if l % 2 == 0:
            w_in_ext, wq_ext, gains = _mla_weights(a_w_in[i], a_w_uq[i], a_q_g[i], a_k_g[i], q_lora, kv_lora)
            p = _normproj(xs, norm1_g[l], sc1, sh1, w_in_ext, group_of, emit_h=False)
            q, k, v = _mla_qkv(p, cos_a, sin_a, wq_ext, a_w_ukv[i].astype(BF16),
                               [a_q_lora_g[i].reshape(1, -1), a_kv_lora_g[i].reshape(1, -1)] + gains,
                               q_lora, kv_lora)
            att = _attention(q, k, v, batch=batch, seq=seq, ctx_len=ctx_len, kv_heads=MLA_HEADS, group=1,
                             dk=2 * V7X_LANES, dv=MLA_V)
            z_col0 = q_lora + kv_lora + 2 * V7X_LANES
            xs = _mixout(att, e_w_o[i].astype(BF16), xs, g1, group_of,
                         conv=(p, z_col0, b_conv_w[i], b_conv_b[i]), t_lat=t_lat, seq=seq, ctx_len=ctx_len)
        else:
            p = _normproj(xs, norm1_g[l], sc1, sh1, c_w_qkv[i].astype(BF16), group_of, emit_h=False)
            q, k, v = _gqa_qkv(p, cos_c, sin_c, c_q_g[i], c_k_g[i])
            att = _attention(q, k, v, batch=batch, seq=seq, ctx_len=ctx_len, kv_heads=GQA_KV_HEADS,
                             group=GQA_HEADS // GQA_KV_HEADS, dk=GQA_HD, dv=GQA_HD)
            xs = _mixout(att, c_w_o[i].astype(BF16), xs, g1, group_of, t_lat=t_lat, seq=seq, ctx_len=ctx_len)
        qp, h2 = _normproj(xs, norm2_g[l], sc2, sh2, p_w_q[l].astype(BF16), group_of, emit_h=True)
        s0, s1, p0, p1, tau = _route(qp, p_sub_keys[l].astype(BF16))
        xs = _peer(h2, p_u[l].astype(BF16), p_v[l].astype(BF16).T, s0, s1, p0, p1, tau, xs, g2, group_of)
    return xs[:t_lat].reshape(batch, seq, d)
```

```python
import functools
import math

import jax
import jax.numpy as jnp
from jax import lax
from jax.experimental import pallas as pl
from jax.experimental.pallas import tpu as pltpu

F32 = jnp.float32
BF16 = jnp.bfloat16

EPS = 1e-6
ROPE_THETA = 10000.0
GRID_W = 64
MLA_HEADS = 8
MLA_NOPE = 128
MLA_ROPE = 64
MLA_V = 128
GQA_HEADS = 16
GQA_KV_HEADS = 4
GQA_HD = 128
PEER_HEADS = 8
N_KEYS = 128
PEER_TOPK = 16
N_MOD = 6

V7X_LANES = 128
V7X_SUBLANES = 8
V7X_VMEM_LIMIT_BYTES = 56 * 1024 * 1024

ROW_BLOCK = 512
ATT_Q_BLOCK = 256
PEER_TB = 512
PEER_EB = 512
PEER_ROWS = PEER_EB // N_KEYS
SQRT_HALF = 0.7071067811865476
NEG_INF = float("-inf")


def _cparams(sem):
    return pltpu.CompilerParams(dimension_semantics=sem, vmem_limit_bytes=V7X_VMEM_LIMIT_BYTES)


def _nt_dot(a, b):
    return lax.dot_general(a, b, (((1,), (1,)), ((), ())), preferred_element_type=F32)


def _ada_kernel(c_ref, w_ref, b_ref, o_ref):
    c = c_ref[...]
    s = c / (1.0 + jnp.exp(-c))
    o_ref[0] = jnp.dot(s.astype(BF16), w_ref[0].astype(BF16), preferred_element_type=F32) + b_ref[0]


def _ada(cvec, ada_w, ada_b):
    depth, d, n = ada_w.shape
    nb = 1024
    rows = cvec.shape[0]
    return pl.pallas_call(
        _ada_kernel,
        grid=(depth, n // nb),
        in_specs=[
            pl.BlockSpec((rows, d), lambda l, j: (0, 0)),
            pl.BlockSpec((1, d, nb), lambda l, j: (l, 0, j)),
            pl.BlockSpec((1, 1, nb), lambda l, j: (l, 0, j)),
        ],
        out_specs=pl.BlockSpec((1, rows, nb), lambda l, j: (l, 0, j)),
        out_shape=jax.ShapeDtypeStruct((depth, rows, n), F32),
        compiler_params=_cparams(("parallel", "parallel")),
        name="ada_mod",
    )(cvec, ada_w, ada_b.reshape(depth, 1, n))


def _normproj_kernel(x_ref, g_ref, sc_ref, sh_ref, w_ref, o_ref, *rest, emit_h):
    hs_ref = rest[-1]

    @pl.when(pl.program_id(1) == 0)
    def _():
        x = x_ref[...]
        y = x * lax.rsqrt(jnp.mean(x * x, axis=-1, keepdims=True) + EPS) * g_ref[...]
        h = (y * (1.0 + sc_ref[0]) + sh_ref[0]).astype(BF16)
        hs_ref[...] = h
        if emit_h:
            rest[0][...] = h

    o_ref[...] = jnp.dot(hs_ref[...], w_ref[...], preferred_element_type=F32)


def _normproj(x, g, sc, sh, w, group_of, emit_h):
    t, d = x.shape
    n = w.shape[1]
    bm, bn = ROW_BLOCK, 512
    out_shape = [jax.ShapeDtypeStruct((t, n), F32)]
    out_specs = [pl.BlockSpec((bm, bn), lambda i, j: (i, j))]
    if emit_h:
        out_shape.append(jax.ShapeDtypeStruct((t, d), BF16))
        out_specs.append(pl.BlockSpec((bm, d), lambda i, j: (i, 0)))
    res = pl.pallas_call(
        functools.partial(_normproj_kernel, emit_h=emit_h),
        grid=(t // bm, n // bn),
        in_specs=[
            pl.BlockSpec((bm, d), lambda i, j: (i, 0)),
            pl.BlockSpec((1, d), lambda i, j: (0, 0)),
            pl.BlockSpec((1, 1, d), lambda i, j: (group_of(i), 0, 0)),
            pl.BlockSpec((1, 1, d), lambda i, j: (group_of(i), 0, 0)),
            pl.BlockSpec((d, bn), lambda i, j: (0, j)),
        ],
        out_specs=out_specs,
        out_shape=out_shape,
        scratch_shapes=[pltpu.VMEM((bm, d), BF16)],
        compiler_params=_cparams(("parallel", "arbitrary")),
        name="normproj_h" if emit_h else "normproj",
    )(x, g.reshape(1, d), sc, sh, w)
    return res if emit_h else res[0]


def _mla_qkv_kernel(p_ref, cos_ref, sin_ref, wuq_ref, wukv_ref, qlg_ref, kvlg_ref,
                    qgn_ref, qgr_ref, qgs_ref, kgn_ref, kgr_ref, kgs_ref,
                    q_ref, k_ref, v_ref, *, q_lora, kv_lora, scale):
    qk_dim = MLA_NOPE + MLA_ROPE
    lanes = V7X_LANES
    cq = p_ref[:, 0:q_lora]
    ckv = p_ref[:, q_lora:q_lora + kv_lora]
    kr = p_ref[:, q_lora + kv_lora:q_lora + kv_lora + lanes]
    krs = p_ref[:, q_lora + kv_lora + lanes:q_lora + kv_lora + 2 * lanes]
    cos = cos_ref[...]
    sin = sin_ref[...]

    cqn = cq * lax.rsqrt(jnp.mean(cq * cq, axis=-1, keepdims=True) + EPS) * qlg_ref[...]
    qraw = jnp.dot(cqn.astype(BF16), wuq_ref[...], preferred_element_type=F32)
    ckvn = ckv * lax.rsqrt(jnp.mean(ckv * ckv, axis=-1, keepdims=True) + EPS) * kvlg_ref[...]
    kvraw = jnp.dot(ckvn.astype(BF16), wukv_ref[...], preferred_element_type=F32)

    kr_rot = kr * kgr_ref[...] * cos + krs * kgs_ref[...] * sin
    kr_ssq = jnp.sum(kr * kr, axis=-1, keepdims=True)
    for h in range(MLA_HEADS):
        nope = qraw[:, h * 3 * lanes:h * 3 * lanes + lanes]
        rope = qraw[:, h * 3 * lanes + lanes:h * 3 * lanes + 2 * lanes]
        rope_sw = qraw[:, h * 3 * lanes + 2 * lanes:h * 3 * lanes + 3 * lanes]
        ssq = jnp.sum(nope * nope, axis=-1, keepdims=True) + jnp.sum(rope * rope, axis=-1, keepdims=True)
        r = lax.rsqrt(ssq * (1.0 / qk_dim) + EPS) * scale
        q_ref[:, h * 2 * lanes:h * 2 * lanes + lanes] = (nope * r * qgn_ref[...]).astype(BF16)
        q_ref[:, h * 2 * lanes + lanes:(h + 1) * 2 * lanes] = (
            (rope * qgr_ref[...] * cos + rope_sw * qgs_ref[...] * sin) * r).astype(BF16)

        k_nope = kvraw[:, h * 2 * lanes:h * 2 * lanes + lanes]
        v = kvraw[:, h * 2 * lanes + lanes:(h + 1) * 2 * lanes]
        kssq = jnp.sum(k_nope * k_nope, axis=-1, keepdims=True) + kr_ssq
        rk = lax.rsqrt(kssq * (1.0 / qk_dim) + EPS)
        k_ref[:, h * 2 * lanes:h * 2 * lanes + lanes] = (k_nope * rk * kgn_ref[...]).astype(BF16)
        k_ref[:, h * 2 * lanes + lanes:(h + 1) * 2 * lanes] = (kr_rot * rk).astype(BF16)
        v_ref[:, h * lanes:(h + 1) * lanes] = v.astype(BF16)


def _mla_qkv(p, cos, sin, wuq, wukv, gains, q_lora, kv_lora):
    t = p.shape[0]
    bm = ROW_BLOCK
    lanes = V7X_LANES
    head_cols = q_lora + kv_lora + 2 * lanes
    full = lambda a: pl.BlockSpec(a.shape, lambda i: (0,) * a.ndim)
    return pl.pallas_call(
        functools.partial(_mla_qkv_kernel, q_lora=q_lora, kv_lora=kv_lora,
                          scale=float(MLA_NOPE + MLA_ROPE) ** -0.5),
        grid=(t // bm,),
        in_specs=[
            pl.BlockSpec((bm, head_cols), lambda i: (i, 0)),
            pl.BlockSpec((bm, lanes), lambda i: (i, 0)),
            pl.BlockSpec((bm, lanes), lambda i: (i, 0)),
            full(wuq), full(wukv)] + [full(g) for g in gains],
        out_specs=[
            pl.BlockSpec((bm, MLA_HEADS * 2 * lanes), lambda i: (i, 0)),
            pl.BlockSpec((bm, MLA_HEADS * 2 * lanes), lambda i: (i, 0)),
            pl.BlockSpec((bm, MLA_HEADS * lanes), lambda i: (i, 0)),
        ],
        out_shape=[
            jax.ShapeDtypeStruct((t, MLA_HEADS * 2 * lanes), BF16),
            jax.ShapeDtypeStruct((t, MLA_HEADS * 2 * lanes), BF16),
            jax.ShapeDtypeStruct((t, MLA_HEADS * lanes), BF16),
        ],
        compiler_params=_cparams(("parallel",)),
        name="mla_qkv",
    )(p, cos, sin, wuq, wukv, *gains)


def _gqa_qkv_kernel(p_ref, cos_ref, sin_ref, qg_ref, kg_ref, q_ref, k_ref, v_ref, *, scale):
    hd = GQA_HD
    cos = cos_ref[...]
    sin = sin_ref[...]

    def head(x, g):
        y = x * lax.rsqrt(jnp.mean(x * x, axis=-1, keepdims=True) + EPS) * g
        return y * cos + pltpu.roll(y, hd // 2, 1) * sin

    for h in range(GQA_HEADS):
        q_ref[:, h * hd:(h + 1) * hd] = (head(p_ref[:, h * hd:(h + 1) * hd], qg_ref[...]) * scale).astype(BF16)
    k0 = GQA_HEADS * hd
    v0 = k0 + GQA_KV_HEADS * hd
    for h in range(GQA_KV_HEADS):
        k_ref[:, h * hd:(h + 1) * hd] = head(p_ref[:, k0 + h * hd:k0 + (h + 1) * hd], kg_ref[...]).astype(BF16)
    v_ref[...] = p_ref[:, v0:v0 + GQA_KV_HEADS * hd].astype(BF16)


def _gqa_qkv(p, cos, sin, qg, kg):
    t, n = p.shape
    bm = ROW_BLOCK
    hd = GQA_HD
    return pl.pallas_call(
        functools.partial(_gqa_qkv_kernel, scale=float(hd) ** -0.5),
        grid=(t // bm,),
        in_specs=[
            pl.BlockSpec((bm, n), lambda i: (i, 0)),
            pl.BlockSpec((bm, hd), lambda i: (i, 0)),
            pl.BlockSpec((bm, hd), lambda i: (i, 0)),
            pl.BlockSpec((1, hd), lambda i: (0, 0)),
            pl.BlockSpec((1, hd), lambda i: (0, 0)),
        ],
        out_specs=[
            pl.BlockSpec((bm, GQA_HEADS * hd), lambda i: (i, 0)),
            pl.BlockSpec((bm, GQA_KV_HEADS * hd), lambda i: (i, 0)),
            pl.BlockSpec((bm, GQA_KV_HEADS * hd), lambda i: (i, 0)),
        ],
        out_shape=[
            jax.ShapeDtypeStruct((t, GQA_HEADS * hd), BF16),
            jax.ShapeDtypeStruct((t, GQA_KV_HEADS * hd), BF16),
            jax.ShapeDtypeStruct((t, GQA_KV_HEADS * hd), BF16),
        ],
        compiler_params=_cparams(("parallel",)),
        name="gqa_qkv",
    )(p, cos, sin, qg.reshape(1, hd), kg.reshape(1, hd))


def _attn_kernel(q_ref, kc_ref, vc_ref, *rest, group, dk, dv, with_latent):
    if with_latent:
        kl_ref, vl_ref, o_ref = rest
    else:
        (o_ref,) = rest
    for g in range(group):
        q = q_ref[:, g * dk:(g + 1) * dk]
        sc = _nt_dot(q, kc_ref[...])
        m = jnp.max(sc, axis=-1, keepdims=True)
        if with_latent:
            sl = _nt_dot(q, kl_ref[...])
            m = jnp.maximum(m, jnp.max(sl, axis=-1, keepdims=True))
        pc = jnp.exp(sc - m)
        den = jnp.sum(pc, axis=-1, keepdims=True)
        o = jnp.dot(pc.astype(BF16), vc_ref[...], preferred_element_type=F32)
        if with_latent:
            pl_ = jnp.exp(sl - m)
            den = den + jnp.sum(pl_, axis=-1, keepdims=True)
            o = o + jnp.dot(pl_.astype(BF16), vl_ref[...], preferred_element_type=F32)
        o_ref[:, g * dv:(g + 1) * dv] = (o / den).astype(BF16)


def _attention(q, k, v, *, batch, seq, ctx_len, kv_heads, group, dk, dv):
    t_lat = batch * seq
    tq = ATT_Q_BLOCK
    nq = seq // tq
    ctx_blk0 = t_lat // ctx_len
    common = dict(group=group, dk=dk, dv=dv)
    lat = pl.pallas_call(
        functools.partial(_attn_kernel, with_latent=True, **common),
        grid=(batch, kv_heads, nq),
        in_specs=[
            pl.BlockSpec((tq, group * dk), lambda b, h, j: (b * nq + j, h)),
            pl.BlockSpec((ctx_len, dk), lambda b, h, j: (ctx_blk0 + b, h)),
            pl.BlockSpec((ctx_len, dv), lambda b, h, j: (ctx_blk0 + b, h)),
            pl.BlockSpec((seq, dk), lambda b, h, j: (b, h)),
            pl.BlockSpec((seq, dv), lambda b, h, j: (b, h)),
        ],
        out_specs=pl.BlockSpec((tq, group * dv), lambda b, h, j: (b * nq + j, h)),
        out_shape=jax.ShapeDtypeStruct((t_lat, kv_heads * group * dv), BF16),
        compiler_params=_cparams(("parallel", "parallel", "arbitrary")),
        name="attn_latent",
    )(q, k, v, k, v)
    ctx = pl.pallas_call(
        functools.partial(_attn_kernel, with_latent=False, **common),
        grid=(batch, kv_heads),
        in_specs=[
            pl.BlockSpec((ctx_len, group * dk), lambda b, h: (ctx_blk0 + b, h)),
            pl.BlockSpec((ctx_len, dk), lambda b, h: (ctx_blk0 + b, h)),
            pl.BlockSpec((ctx_len, dv), lambda b, h: (ctx_blk0 + b, h)),
        ],
        out_specs=pl.BlockSpec((ctx_len, group * dv), lambda b, h: (b, h)),
        out_shape=jax.ShapeDtypeStruct((batch * ctx_len, kv_heads * group * dv), BF16),
        compiler_params=_cparams(("parallel", "parallel")),
        name="attn_ctx",
    )(q, k, v)
    return jnp.concatenate([lat, ctx], axis=0)


def _mixout_kernel(*refs, with_conv, t_lat, seq, ctx_len, a_width):
    if with_conv:
        (a_ref, bg_ref, cg_ref, hz_ref, cgp_ref, hzp_ref, cgn_ref, hzn_ref, cw_ref, cb_ref,
         w_ref, x_ref, gate_ref, o_ref, cs_ref) = refs
    else:
        a_ref, w_ref, x_ref, gate_ref, o_ref = refs
    i = pl.program_id(0)

    if with_conv:
        @pl.when(pl.program_id(1) == 0)
        def _():
            bm = cg_ref.shape[0]
            u = cg_ref[...] * hz_ref[...]
            u_before = cgp_ref[V7X_SUBLANES - 1:V7X_SUBLANES, :] * hzp_ref[V7X_SUBLANES - 1:V7X_SUBLANES, :]
            u_after = cgn_ref[0:1, :] * hzn_ref[0:1, :]
            local = lax.broadcasted_iota(jnp.int32, (bm, 1), 0)
            row = local + i * bm
            in_lat = row < t_lat
            seg_pos = jnp.where(in_lat, jnp.bitwise_and(row, seq - 1), jnp.bitwise_and(row - t_lat, ctx_len - 1))
            seg_len = jnp.where(in_lat, seq, ctx_len)
            up = jnp.where(local == 0, u_before, pltpu.roll(u, 1, 0))
            up = jnp.where(seg_pos == 0, 0.0, up)
            un = jnp.where(local == bm - 1, u_after, pltpu.roll(u, bm - 1, 0))
            un = jnp.where(seg_pos == seg_len - 1, 0.0, un)
            y = up * cw_ref[0:1, :] + u * cw_ref[1:2, :] + un * cw_ref[2:3, :] + cb_ref[...]
            cs_ref[...] = (bg_ref[...] * y).astype(BF16)

        acc = jnp.dot(a_ref[...], w_ref[0:a_width, :], preferred_element_type=F32)
        acc = acc + jnp.dot(cs_ref[...], w_ref[a_width:, :], preferred_element_type=F32)
    else:
        acc = jnp.dot(a_ref[...], w_ref[...], preferred_element_type=F32)
    o_ref[...] = x_ref[...] + gate_ref[0] * acc


def _mixout(a, w, x, gate, group_of, conv=None, *, t_lat, seq, ctx_len):
    t, d = x.shape
    bm, bn = ROW_BLOCK, 512
    a_width = a.shape[1]
    kw = dict(t_lat=t_lat, seq=seq, ctx_len=ctx_len, a_width=a_width)
    in_specs = [pl.BlockSpec((bm, a_width), lambda i, j: (i, 0))]
    args = [a]
    scratch = []
    if conv is not None:
        p, z_col0, cw, cb = conv
        cwid = cw.shape[1]
        assert z_col0 % cwid == 0
        zb = z_col0 // cwid
        sub = V7X_SUBLANES
        last_halo = t // sub - 1
        prev_idx = lambda i: jnp.maximum(i * (bm // sub) - 1, 0)
        next_idx = lambda i: jnp.minimum((i + 1) * (bm // sub), last_halo)
        in_specs += [
            pl.BlockSpec((bm, cwid), lambda i, j: (i, zb)),
            pl.BlockSpec((bm, cwid), lambda i, j: (i, zb + 1)),
            pl.BlockSpec((bm, cwid), lambda i, j: (i, zb + 2)),
            pl.BlockSpec((sub, cwid), lambda i, j: (prev_idx(i), zb + 1)),
            pl.BlockSpec((sub, cwid), lambda i, j: (prev_idx(i), zb + 2)),
            pl.BlockSpec((sub, cwid), lambda i, j: (next_idx(i), zb + 1)),
            pl.BlockSpec((sub, cwid), lambda i, j: (next_idx(i), zb + 2)),
            pl.BlockSpec(cw.shape, lambda i, j: (0, 0)),
            pl.BlockSpec((1, cwid), lambda i, j: (0, 0)),
        ]
        args += [p] * 7 + [cw, cb.reshape(1, cwid)]
        scratch = [pltpu.VMEM((bm, cwid), BF16)]
    in_specs += [
        pl.BlockSpec((w.shape[0], bn), lambda i, j: (0, j)),
        pl.BlockSpec((bm, bn), lambda i, j: (i, j)),
        pl.BlockSpec((1, 1, bn), lambda i, j: (group_of(i), 0, j)),
    ]
    args += [w, x, gate]
    return pl.pallas_call(
        functools.partial(_mixout_kernel, with_conv=conv is not None, **kw),
        grid=(t // bm, d // bn),
        in_specs=in_specs,
        out_specs=pl.BlockSpec((bm, bn), lambda i, j: (i, j)),
        out_shape=jax.ShapeDtypeStruct((t, d), F32),
        scratch_shapes=scratch,
        compiler_params=_cparams(("parallel", "arbitrary")),
        name="mixout_conv" if conv is not None else "mixout",
    )(*args)


_CAND_PAIRS = [(a, b) for a in range(PEER_TOPK) for b in range(PEER_TOPK // (a + 1))]
_CAND_ROWS = -(-len(_CAND_PAIRS) // V7X_SUBLANES) * V7X_SUBLANES
BF16_ROWS = 2 * V7X_SUBLANES


def _route_kernel(q_ref, sk_ref, r1_ref, p1_ref, np_ref, cand_ref):
    nk = N_KEYS
    k_top = PEER_TOPK
    lanes = V7X_LANES
    tb = q_ref.shape[0]
    cand_ref[len(_CAND_PAIRS):, :] = jnp.full((_CAND_ROWS - len(_CAND_PAIRS), lanes), NEG_INF, F32)
    for c in range(tb // lanes):
        cs = slice(c * lanes, (c + 1) * lanes)
        s0, s1 = (_nt_dot(sk_ref[0, half], q_ref[cs, half * nk:(half + 1) * nk].astype(BF16))
                  for half in range(2))

        cur = s0
        tops0 = []
        for k in range(k_top):
            m = jnp.max(cur, axis=0, keepdims=True)
            tops0.append(m)
            cur = jnp.where(cur == m, NEG_INF, cur)
        cur = s1
        rank = jnp.full((nk, lanes), float(k_top), F32)
        tops1 = []
        for k in range(k_top):
            m = jnp.max(cur, axis=0, keepdims=True)
            tops1.append(m)
            hit = cur == m
            rank = jnp.where(hit, float(k), rank)
            cur = jnp.where(hit, NEG_INF, cur)

        for r, (a, b) in enumerate(_CAND_PAIRS):
            cand_ref[r:r + 1, :] = tops0[a] + tops1[b]
        cand = cand_ref[...]
        top = tops0[0] + tops1[0]
        z = jnp.zeros((1, lanes), F32)
        for k in range(k_top):
            tau = jnp.max(cand, axis=0, keepdims=True)
            z = z + jnp.exp(tau - top)
            cand = jnp.where(cand == tau, NEG_INF, cand)

        n0 = jnp.zeros((nk, lanes), F32)
        for b in range(k_top):
            n0 = n0 + jnp.where(s0 + tops1[b] >= tau, 1.0, 0.0)
        p0 = jnp.exp(s0 - tops0[0])
        for g in range(nk // PEER_ROWS):
            np_ref[0, g, 0:PEER_ROWS, cs] = n0[g * PEER_ROWS:(g + 1) * PEER_ROWS, :]
            np_ref[0, g, PEER_ROWS:, cs] = p0[g * PEER_ROWS:(g + 1) * PEER_ROWS, :]
        rank_b = rank.astype(BF16)
        p1_b = (jnp.exp(s1 - tops1[0]) / z).astype(BF16)
        for g in range(nk // BF16_ROWS):
            r1_ref[0, g, :, cs] = pltpu.bitcast(rank_b[g * BF16_ROWS:(g + 1) * BF16_ROWS, :], jnp.int32)
            p1_ref[0, g, :, cs] = pltpu.bitcast(p1_b[g * BF16_ROWS:(g + 1) * BF16_ROWS, :], jnp.int32)


def _route(qp, sub_keys):
    t = qp.shape[0]
    tb = PEER_TB
    nk = N_KEYS
    heads = PEER_HEADS
    packed = jax.ShapeDtypeStruct((heads, nk // BF16_ROWS, V7X_SUBLANES, t), jnp.int32)
    packed_spec = pl.BlockSpec((1, nk // BF16_ROWS, V7X_SUBLANES, tb), lambda i, h: (h, 0, 0, i))
    plain = jax.ShapeDtypeStruct((heads, nk // PEER_ROWS, 2 * PEER_ROWS, t), F32)
    plain_spec = pl.BlockSpec((1, nk // PEER_ROWS, 2 * PEER_ROWS, tb), lambda i, h: (h, 0, 0, i))
    return pl.pallas_call(
        _route_kernel,
        grid=(t // tb, heads),
        in_specs=[
            pl.BlockSpec((tb, 2 * nk), lambda i, h: (i, h)),
            pl.BlockSpec((1, 2, nk, sub_keys.shape[-1]), lambda i, h: (h, 0, 0, 0)),
        ],
        out_specs=[packed_spec, packed_spec, plain_spec],
        out_shape=[packed, packed, plain],
        scratch_shapes=[pltpu.VMEM((_CAND_ROWS, V7X_LANES), F32)],
        compiler_params=_cparams(("parallel", "arbitrary")),
        name="peer_route",
    )(qp, sub_keys)


def _peer_kernel(h_ref, u_ref, vt_ref, r1_ref, p1_ref, np_ref, x_ref, gate_ref,
                 o_ref, acc_ref, at_ref, hs_ref):
    j = pl.program_id(1)
    n_blocks = pl.num_programs(1) - 1
    nk = N_KEYS
    lanes = V7X_LANES
    _, eb, tb = at_ref.shape

    @pl.when(j == 0)
    def _():
        acc_ref[...] = jnp.zeros_like(acc_ref)
        at_ref[1] = jnp.zeros((eb, tb), F32)

    jm = jnp.maximum(j - 1, 0)

    def key_row(hd, row, cs):
        row = np_ref[hd, jm, row:row + 1, cs]
        return jnp.broadcast_to(row, (BF16_ROWS, lanes)).astype(BF16)[None]

    def step(cur, prev):
        at_ref[cur] = _nt_dot(u_ref[...], h_ref[...])
        for il in range(PEER_ROWS):
            for c in range(tb // lanes):
                cs = slice(c * lanes, (c + 1) * lanes)
                w = None
                for hd in range(PEER_HEADS):
                    r1 = pltpu.bitcast(r1_ref[hd, :, :, cs], BF16)
                    p1 = pltpu.bitcast(p1_ref[hd, :, :, cs], BF16)
                    term = jnp.where(r1 < key_row(hd, il, cs), p1 * key_row(hd, PEER_ROWS + il, cs),
                                     jnp.zeros((), BF16))
                    w = term if w is None else w + term
                a = at_ref[prev, il * nk:(il + 1) * nk, cs]
                gelu = (0.5 * a * (1.0 + lax.erf(a * SQRT_HALF))).astype(BF16)
                for g in range(nk // BF16_ROWS):
                    r0 = il * nk + g * BF16_ROWS
                    hs_ref[r0:r0 + BF16_ROWS, cs] = w[g] * gelu[g * BF16_ROWS:(g + 1) * BF16_ROWS, :]
        acc_ref[...] += jnp.dot(vt_ref[...], hs_ref[...], preferred_element_type=F32)

    @pl.when(j % 2 == 0)
    def _():
        step(0, 1)

    @pl.when(j % 2 == 1)
    def _():
        step(1, 0)

    @pl.when(j == n_blocks)
    def _():
        o_ref[...] = x_ref[...] + gate_ref[0] * acc_ref[...].T


def _peer(h, u, vt, r1, p1, n0p0, x, gate, group_of):
    t, d = x.shape
    tb, eb = PEER_TB, PEER_EB
    n_blocks = u.shape[0] // eb
    nk = N_KEYS
    heads = PEER_HEADS
    packed_spec = pl.BlockSpec((heads, nk // BF16_ROWS, V7X_SUBLANES, tb), lambda i, j: (0, 0, 0, i))
    plain_spec = pl.BlockSpec((heads, nk // PEER_ROWS, 2 * PEER_ROWS, tb), lambda i, j: (0, 0, 0, i))
    return pl.pallas_call(
        _peer_kernel,
        grid=(t // tb, n_blocks + 1),
        in_specs=[
            pl.BlockSpec((tb, d), lambda i, j: (i, 0)),
            pl.BlockSpec((eb, d), lambda i, j: (jnp.minimum(j, n_blocks - 1), 0)),
            pl.BlockSpec((d, eb), lambda i, j: (0, jnp.maximum(j - 1, 0))),
            packed_spec, packed_spec, plain_spec,
            pl.BlockSpec((tb, d), lambda i, j: (i, 0)),
            pl.BlockSpec((1, 1, d), lambda i, j: (group_of(i), 0, 0)),
        ],
        out_specs=pl.BlockSpec((tb, d), lambda i, j: (i, 0)),
        out_shape=jax.ShapeDtypeStruct((t, d), F32),
        scratch_shapes=[
            pltpu.VMEM((d, tb), F32),
            pltpu.VMEM((2, eb, tb), F32),
            pltpu.VMEM((eb, tb), BF16),
        ],
        compiler_params=_cparams(("parallel", "arbitrary")),
        name="peer_mix",
    )(h, u, vt, r1, p1, n0p0, x, gate)


def _rope_tables(seq, rope_dim, batch, ctx_rows):
    rows = seq // GRID_W
    row = jnp.repeat(jnp.arange(rows, dtype=F32), GRID_W)
    col = jnp.tile(jnp.arange(GRID_W, dtype=F32), rows)
    quarter = rope_dim // 4
    freqs = ROPE_THETA ** (-jnp.arange(quarter, dtype=F32) / quarter)
    ang = jnp.concatenate([row[:, None] * freqs, col[:, None] * freqs], axis=-1)
    cos, sin = jnp.cos(ang), jnp.sin(ang)
    pad = V7X_LANES - rope_dim
    cos_l = jnp.pad(jnp.concatenate([cos, cos], axis=-1), ((0, 0), (0, pad)))
    sin_l = jnp.pad(jnp.concatenate([-sin, sin], axis=-1), ((0, 0), (0, pad)))
    cos_c = jnp.pad(jnp.ones((ctx_rows, rope_dim), F32), ((0, 0), (0, pad)))
    sin_c = jnp.zeros((ctx_rows, V7X_LANES), F32)
    return (jnp.concatenate([jnp.tile(cos_l, (batch, 1)), cos_c], axis=0),
            jnp.concatenate([jnp.tile(sin_l, (batch, 1)), sin_c], axis=0))


def _swap_halves(a):
    half = a.shape[-1] // 2
    return jnp.concatenate([a[..., half:], a[..., :half]], axis=-1)


def _pad_lanes(a):
    return jnp.pad(a, [(0, 0)] * (a.ndim - 1) + [(0, V7X_LANES - a.shape[-1])])


def _mla_weights(w_in, w_uq, q_g, k_g, q_lora, kv_lora):
    d = w_in.shape[0]
    c_kr = q_lora + kv_lora
    w_kr = w_in[:, c_kr:c_kr + MLA_ROPE]
    w_in_ext = jnp.concatenate([
        w_in[:, :c_kr], _pad_lanes(w_kr), _pad_lanes(_swap_halves(w_kr)), w_in[:, c_kr + MLA_ROPE:]], axis=1)
    wq = w_uq.reshape(q_lora, MLA_HEADS, MLA_NOPE + MLA_ROPE)
    wq_rope = wq[:, :, MLA_NOPE:]
    wq_ext = jnp.concatenate([wq[:, :, :MLA_NOPE], _pad_lanes(wq_rope), _pad_lanes(_swap_halves(wq_rope))], axis=-1)
    wq_ext = wq_ext.reshape(q_lora, MLA_HEADS * 3 * V7X_LANES)

    def gains(g):
        g_r = g[MLA_NOPE:]
        return [g[:MLA_NOPE].reshape(1, -1), _pad_lanes(g_r).reshape(1, -1), _pad_lanes(_swap_halves(g_r)).reshape(1, -1)]

    return w_in_ext.astype(BF16), wq_ext.astype(BF16), gains(q_g) + gains(k_g)


def kernel(x, c, ctx, c_ctx, ada_w, ada_b, norm1_g, norm2_g, a_w_in, a_q_lora_g, a_kv_lora_g, a_w_uq, a_w_ukv, a_q_g, a_k_g, b_conv_w, b_conv_b, e_w_o, c_w_qkv, c_q_g, c_k_g, c_w_o, p_w_q, p_sub_keys, p_u, p_v):
    batch, seq, d = x.shape
    ctx_len = ctx.shape[1]
    depth = ada_w.shape[0]
    q_lora = a_q_lora_g.shape[1]
    kv_lora = a_kv_lora_g.shape[1]
    conv_width = b_conv_w.shape[2]
    t_lat = batch * seq
    t_ctx = batch * ctx_len
    t = t_lat + t_ctx
    bm = ROW_BLOCK
    assert seq % bm == 0 and t_ctx % bm == 0 and t % PEER_TB == 0 and seq % ATT_Q_BLOCK == 0
    assert seq & (seq - 1) == 0 and ctx_len & (ctx_len - 1) == 0 and seq % GRID_W == 0
    assert batch + 1 <= V7X_SUBLANES

    blocks_per_batch = seq // bm
    group_of = lambda i: jnp.minimum(i // blocks_per_batch, batch)

    cvec = jnp.zeros((V7X_SUBLANES, d), F32).at[:batch].set(c).at[batch].set(c_ctx)
    mods = _ada(cvec, ada_w, ada_b)
    mods = mods.reshape(depth, V7X_SUBLANES, N_MOD, 1, d).transpose(0, 2, 1, 3, 4)

    cos_a, sin_a = _rope_tables(seq, MLA_ROPE, batch, t_ctx)
    cos_c, sin_c = _rope_tables(seq, GQA_HD, batch, t_ctx)

    xs = jnp.concatenate([x.reshape(t_lat, d), ctx.reshape(t_ctx, d)], axis=0)
    for l in range(depth):
        sh1, sc1, g1, sh2, sc2, g2 = (mods[l, k] for k in range(N_MOD))
        i = l // 2
        if l % 2 == 0:
            w_in_ext, wq_ext, gains = _mla_weights(a_w_in[i], a_w_uq[i], a_q_g[i], a_k_g[i], q_lora, kv_lora)
            p = _normproj(xs, norm1_g[l], sc1, sh1, w_in_ext, group_of, emit_h=False)
            q, k, v = _mla_qkv(p, cos_a, sin_a, wq_ext, a_w_ukv[i].astype(BF16),
                               [a_q_lora_g[i].reshape(1, -1), a_kv_lora_g[i].reshape(1, -1)] + gains,
                               q_lora, kv_lora)
            att = _attention(q, k, v, batch=batch, seq=seq, ctx_len=ctx_len, kv_heads=MLA_HEADS, group=1,
                             dk=2 * V7X_LANES, dv=MLA_V)
            z_col0 = q_lora + kv_lora + 2 * V7X_LANES
            xs = _mixout(att, e_w_o[i].astype(BF16), xs, g1, group_of,
                         conv=(p, z_col0, b_conv_w[i], b_conv_b[i]), t_lat=t_lat, seq=seq, ctx_len=ctx_len)
        else:
            p = _normproj(xs, norm1_g[l], sc1, sh1, c_w_qkv[i].astype(BF16), group_of, emit_h=False)
            q, k, v = _gqa_qkv(p, cos_c, sin_c, c_q_g[i], c_k_g[i])
            att = _attention(q, k, v, batch=batch, seq=seq, ctx_len=ctx_len, kv_heads=GQA_KV_HEADS,
                             group=GQA_HEADS // GQA_KV_HEADS, dk=GQA_HD, dv=GQA_HD)
            xs = _mixout(att, c_w_o[i].astype(BF16), xs, g1, group_of, t_lat=t_lat, seq=seq, ctx_len=ctx_len)
        qp, h2 = _normproj(xs, norm2_g[l], sc2, sh2, p_w_q[l].astype(BF16), group_of, emit_h=True)
        r1, p1, n0p0 = _route(qp, p_sub_keys[l].astype(BF16))
        xs = _peer(h2, p_u[l].astype(BF16), p_v[l].astype(BF16).T, r1, p1, n0p0, xs, g2, group_of)
    return xs[:t_lat].reshape(batch, seq, d)
```

```python
import functools
import math

import jax
import jax.numpy as jnp
from jax import lax
from jax.experimental import pallas as pl
from jax.experimental.pallas import tpu as pltpu

F32 = jnp.float32
BF16 = jnp.bfloat16

EPS = 1e-6
ROPE_THETA = 10000.0
GRID_W = 64
MLA_HEADS = 8
MLA_NOPE = 128
MLA_ROPE = 64
MLA_V = 128
GQA_HEADS = 16
GQA_KV_HEADS = 4
GQA_HD = 128
PEER_HEADS = 8
N_KEYS = 128
PEER_TOPK = 16
N_MOD = 6

V7X_LANES = 128
V7X_SUBLANES = 8
V7X_VMEM_LIMIT_BYTES = 56 * 1024 * 1024

ROW_BLOCK = 512
ATT_Q_BLOCK = 256
PEER_TB = 512
PEER_EB = 512
PEER_ROWS = PEER_EB // N_KEYS
SQRT_HALF = 0.7071067811865476
NEG_INF = float("-inf")


def _cparams(sem):
    return pltpu.CompilerParams(dimension_semantics=sem, vmem_limit_bytes=V7X_VMEM_LIMIT_BYTES)


def _nt_dot(a, b):
    return lax.dot_general(a, b, (((1,), (1,)), ((), ())), preferred_element_type=F32)


def _ada_kernel(c_ref, w_ref, b_ref, o_ref):
    c = c_ref[...]
    s = c / (1.0 + jnp.exp(-c))
    o_ref[0] = jnp.dot(s.astype(BF16), w_ref[0].astype(BF16), preferred_element_type=F32) + b_ref[0]


def _ada(cvec, ada_w, ada_b):
    depth, d, n = ada_w.shape
    nb = 1024
    rows = cvec.shape[0]
    return pl.pallas_call(
        _ada_kernel,
        grid=(depth, n // nb),
        in_specs=[
            pl.BlockSpec((rows, d), lambda l, j: (0, 0)),
            pl.BlockSpec((1, d, nb), lambda l, j: (l, 0, j)),
            pl.BlockSpec((1, 1, nb), lambda l, j: (l, 0, j)),
        ],
        out_specs=pl.BlockSpec((1, rows, nb), lambda l, j: (l, 0, j)),
        out_shape=jax.ShapeDtypeStruct((depth, rows, n), F32),
        compiler_params=_cparams(("parallel", "parallel")),
        name="ada_mod",
    )(cvec, ada_w, ada_b.reshape(depth, 1, n))


def _normproj_kernel(x_ref, g_ref, sc_ref, sh_ref, w_ref, o_ref, *rest, emit_h):
    hs_ref = rest[-1]

    @pl.when(pl.program_id(1) == 0)
    def _():
        x = x_ref[...]
        y = x * lax.rsqrt(jnp.mean(x * x, axis=-1, keepdims=True) + EPS) * g_ref[...]
        h = (y * (1.0 + sc_ref[0]) + sh_ref[0]).astype(BF16)
        hs_ref[...] = h
        if emit_h:
            rest[0][...] = pltpu.bitcast(h, jnp.uint32)

    o_ref[...] = jnp.dot(hs_ref[...], w_ref[...], preferred_element_type=F32)


def _normproj(x, g, sc, sh, w, group_of, emit_h):
    t, d = x.shape
    n = w.shape[1]
    bm, bn = ROW_BLOCK, 512
    out_shape = [jax.ShapeDtypeStruct((t, n), F32)]
    out_specs = [pl.BlockSpec((bm, bn), lambda i, j: (i, j))]
    if emit_h:
        out_shape.append(jax.ShapeDtypeStruct((t // 2, d), jnp.uint32))
        out_specs.append(pl.BlockSpec((bm // 2, d), lambda i, j: (i, 0)))
    res = pl.pallas_call(
        functools.partial(_normproj_kernel, emit_h=emit_h),
        grid=(t // bm, n // bn),
        in_specs=[
            pl.BlockSpec((bm, d), lambda i, j: (i, 0)),
            pl.BlockSpec((1, d), lambda i, j: (0, 0)),
            pl.BlockSpec((1, 1, d), lambda i, j: (group_of(i), 0, 0)),
            pl.BlockSpec((1, 1, d), lambda i, j: (group_of(i), 0, 0)),
            pl.BlockSpec((d, bn), lambda i, j: (0, j)),
        ],
        out_specs=out_specs,
        out_shape=out_shape,
        scratch_shapes=[pltpu.VMEM((bm, d), BF16)],
        compiler_params=_cparams(("parallel", "arbitrary")),
        name="normproj_h" if emit_h else "normproj",
    )(x, g.reshape(1, d), sc, sh, w)
    return res if emit_h else res[0]


def _mla_qkv_kernel(p_ref, cos_ref, sin_ref, wuq_ref, wukv_ref, qlg_ref, kvlg_ref,
                    qgn_ref, qgr_ref, qgs_ref, kgn_ref, kgr_ref, kgs_ref,
                    q_ref, k_ref, v_ref, *, q_lora, kv_lora, scale):
    qk_dim = MLA_NOPE + MLA_ROPE
    lanes = V7X_LANES
    cq = p_ref[:, 0:q_lora]
    ckv = p_ref[:, q_lora:q_lora + kv_lora]
    kr = p_ref[:, q_lora + kv_lora:q_lora + kv_lora + lanes]
    krs = p_ref[:, q_lora + kv_lora + lanes:q_lora + kv_lora + 2 * lanes]
    cos = cos_ref[...]
    sin = sin_ref[...]

    cqn = cq * lax.rsqrt(jnp.mean(cq * cq, axis=-1, keepdims=True) + EPS) * qlg_ref[...]
    qraw = jnp.dot(cqn.astype(BF16), wuq_ref[...], preferred_element_type=F32)
    ckvn = ckv * lax.rsqrt(jnp.mean(ckv * ckv, axis=-1, keepdims=True) + EPS) * kvlg_ref[...]
    kvraw = jnp.dot(ckvn.astype(BF16), wukv_ref[...], preferred_element_type=F32)

    kr_rot = kr * kgr_ref[...] * cos + krs * kgs_ref[...] * sin
    kr_ssq = jnp.sum(kr * kr, axis=-1, keepdims=True)
    for h in range(MLA_HEADS):
        nope = qraw[:, h * 3 * lanes:h * 3 * lanes + lanes]
        rope = qraw[:, h * 3 * lanes + lanes:h * 3 * lanes + 2 * lanes]
        rope_sw = qraw[:, h * 3 * lanes + 2 * lanes:h * 3 * lanes + 3 * lanes]
        ssq = jnp.sum(nope * nope, axis=-1, keepdims=True) + jnp.sum(rope * rope, axis=-1, keepdims=True)
        r = lax.rsqrt(ssq * (1.0 / qk_dim) + EPS) * scale
        q_ref[:, h * 2 * lanes:h * 2 * lanes + lanes] = (nope * r * qgn_ref[...]).astype(BF16)
        q_ref[:, h * 2 * lanes + lanes:(h + 1) * 2 * lanes] = (
            (rope * qgr_ref[...] * cos + rope_sw * qgs_ref[...] * sin) * r).astype(BF16)

        k_nope = kvraw[:, h * 2 * lanes:h * 2 * lanes + lanes]
        v = kvraw[:, h * 2 * lanes + lanes:(h + 1) * 2 * lanes]
        kssq = jnp.sum(k_nope * k_nope, axis=-1, keepdims=True) + kr_ssq
        rk = lax.rsqrt(kssq * (1.0 / qk_dim) + EPS)
        k_ref[:, h * 2 * lanes:h * 2 * lanes + lanes] = (k_nope * rk * kgn_ref[...]).astype(BF16)
        k_ref[:, h * 2 * lanes + lanes:(h + 1) * 2 * lanes] = (kr_rot * rk).astype(BF16)
        v_ref[:, h * lanes:(h + 1) * lanes] = v.astype(BF16)


def _mla_qkv(p, cos, sin, wuq, wukv, gains, q_lora, kv_lora):
    t = p.shape[0]
    bm = ROW_BLOCK
    lanes = V7X_LANES
    head_cols = q_lora + kv_lora + 2 * lanes
    full = lambda a: pl.BlockSpec(a.shape, lambda i: (0,) * a.ndim)
    return pl.pallas_call(
        functools.partial(_mla_qkv_kernel, q_lora=q_lora, kv_lora=kv_lora,
                          scale=float(MLA_NOPE + MLA_ROPE) ** -0.5),
        grid=(t // bm,),
        in_specs=[
            pl.BlockSpec((bm, head_cols), lambda i: (i, 0)),
            pl.BlockSpec((bm, lanes), lambda i: (i, 0)),
            pl.BlockSpec((bm, lanes), lambda i: (i, 0)),
            full(wuq), full(wukv)] + [full(g) for g in gains],
        out_specs=[
            pl.BlockSpec((bm, MLA_HEADS * 2 * lanes), lambda i: (i, 0)),
            pl.BlockSpec((bm, MLA_HEADS * 2 * lanes), lambda i: (i, 0)),
            pl.BlockSpec((bm, MLA_HEADS * lanes), lambda i: (i, 0)),
        ],
        out_shape=[
            jax.ShapeDtypeStruct((t, MLA_HEADS * 2 * lanes), BF16),
            jax.ShapeDtypeStruct((t, MLA_HEADS * 2 * lanes), BF16),
            jax.ShapeDtypeStruct((t, MLA_HEADS * lanes), BF16),
        ],
        compiler_params=_cparams(("parallel",)),
        name="mla_qkv",
    )(p, cos, sin, wuq, wukv, *gains)


def _gqa_qkv_kernel(p_ref, cos_ref, sin_ref, qg_ref, kg_ref, q_ref, k_ref, v_ref, *, scale):
    hd = GQA_HD
    cos = cos_ref[...]
    sin = sin_ref[...]

    def head(x, g):
        y = x * lax.rsqrt(jnp.mean(x * x, axis=-1, keepdims=True) + EPS) * g
        return y * cos + pltpu.roll(y, hd // 2, 1) * sin

    for h in range(GQA_HEADS):
        q_ref[:, h * hd:(h + 1) * hd] = (head(p_ref[:, h * hd:(h + 1) * hd], qg_ref[...]) * scale).astype(BF16)
    k0 = GQA_HEADS * hd
    v0 = k0 + GQA_KV_HEADS * hd
    for h in range(GQA_KV_HEADS):
        k_ref[:, h * hd:(h + 1) * hd] = head(p_ref[:, k0 + h * hd:k0 + (h + 1) * hd], kg_ref[...]).astype(BF16)
    v_ref[...] = p_ref[:, v0:v0 + GQA_KV_HEADS * hd].astype(BF16)


def _gqa_qkv(p, cos, sin, qg, kg):
    t, n = p.shape
    bm = ROW_BLOCK
    hd = GQA_HD
    return pl.pallas_call(
        functools.partial(_gqa_qkv_kernel, scale=float(hd) ** -0.5),
        grid=(t // bm,),
        in_specs=[
            pl.BlockSpec((bm, n), lambda i: (i, 0)),
            pl.BlockSpec((bm, hd), lambda i: (i, 0)),
            pl.BlockSpec((bm, hd), lambda i: (i, 0)),
            pl.BlockSpec((1, hd), lambda i: (0, 0)),
            pl.BlockSpec((1, hd), lambda i: (0, 0)),
        ],
        out_specs=[
            pl.BlockSpec((bm, GQA_HEADS * hd), lambda i: (i, 0)),
            pl.BlockSpec((bm, GQA_KV_HEADS * hd), lambda i: (i, 0)),
            pl.BlockSpec((bm, GQA_KV_HEADS * hd), lambda i: (i, 0)),
        ],
        out_shape=[
            jax.ShapeDtypeStruct((t, GQA_HEADS * hd), BF16),
            jax.ShapeDtypeStruct((t, GQA_KV_HEADS * hd), BF16),
            jax.ShapeDtypeStruct((t, GQA_KV_HEADS * hd), BF16),
        ],
        compiler_params=_cparams(("parallel",)),
        name="gqa_qkv",
    )(p, cos, sin, qg.reshape(1, hd), kg.reshape(1, hd))


def _attn_kernel(q_ref, kc_ref, vc_ref, *rest, group, dk, dv, with_latent):
    if with_latent:
        kl_ref, vl_ref, o_ref = rest
    else:
        (o_ref,) = rest
    for g in range(group):
        q = q_ref[:, g * dk:(g + 1) * dk]
        sc = _nt_dot(q, kc_ref[...])
        m = jnp.max(sc, axis=-1, keepdims=True)
        if with_latent:
            sl = _nt_dot(q, kl_ref[...])
            m = jnp.maximum(m, jnp.max(sl, axis=-1, keepdims=True))
        pc = jnp.exp(sc - m)
        den = jnp.sum(pc, axis=-1, keepdims=True)
        o = jnp.dot(pc.astype(BF16), vc_ref[...], preferred_element_type=F32)
        if with_latent:
            pl_ = jnp.exp(sl - m)
            den = den + jnp.sum(pl_, axis=-1, keepdims=True)
            o = o + jnp.dot(pl_.astype(BF16), vl_ref[...], preferred_element_type=F32)
        o_ref[:, g * dv:(g + 1) * dv] = (o / den).astype(BF16)


def _attention(q, k, v, *, batch, seq, ctx_len, kv_heads, group, dk, dv):
    t_lat = batch * seq
    tq = ATT_Q_BLOCK
    nq = seq // tq
    ctx_blk0 = t_lat // ctx_len
    common = dict(group=group, dk=dk, dv=dv)
    lat = pl.pallas_call(
        functools.partial(_attn_kernel, with_latent=True, **common),
        grid=(batch, kv_heads, nq),
        in_specs=[
            pl.BlockSpec((tq, group * dk), lambda b, h, j: (b * nq + j, h)),
            pl.BlockSpec((ctx_len, dk), lambda b, h, j: (ctx_blk0 + b, h)),
            pl.BlockSpec((ctx_len, dv), lambda b, h, j: (ctx_blk0 + b, h)),
            pl.BlockSpec((seq, dk), lambda b, h, j: (b, h)),
            pl.BlockSpec((seq, dv), lambda b, h, j: (b, h)),
        ],
        out_specs=pl.BlockSpec((tq, group * dv), lambda b, h, j: (b * nq + j, h)),
        out_shape=jax.ShapeDtypeStruct((t_lat, kv_heads * group * dv), BF16),
        compiler_params=_cparams(("parallel", "parallel", "arbitrary")),
        name="attn_latent",
    )(q, k, v, k, v)
    ctx = pl.pallas_call(
        functools.partial(_attn_kernel, with_latent=False, **common),
        grid=(batch, kv_heads),
        in_specs=[
            pl.BlockSpec((ctx_len, group * dk), lambda b, h: (ctx_blk0 + b, h)),
            pl.BlockSpec((ctx_len, dk), lambda b, h: (ctx_blk0 + b, h)),
            pl.BlockSpec((ctx_len, dv), lambda b, h: (ctx_blk0 + b, h)),
        ],
        out_specs=pl.BlockSpec((ctx_len, group * dv), lambda b, h: (b, h)),
        out_shape=jax.ShapeDtypeStruct((batch * ctx_len, kv_heads * group * dv), BF16),
        compiler_params=_cparams(("parallel", "parallel")),
        name="attn_ctx",
    )(q, k, v)
    return jnp.concatenate([lat, ctx], axis=0)


def _mixout_kernel(*refs, with_conv, t_lat, seq, ctx_len, a_width):
    if with_conv:
        (a_ref, bg_ref, cg_ref, hz_ref, cgp_ref, hzp_ref, cgn_ref, hzn_ref, cw_ref, cb_ref,
         w_ref, x_ref, gate_ref, o_ref, cs_ref) = refs
    else:
        a_ref, w_ref, x_ref, gate_ref, o_ref = refs
    i = pl.program_id(0)

    if with_conv:
        @pl.when(pl.program_id(1) == 0)
        def _():
            bm = cg_ref.shape[0]
            u = cg_ref[...] * hz_ref[...]
            u_before = cgp_ref[V7X_SUBLANES - 1:V7X_SUBLANES, :] * hzp_ref[V7X_SUBLANES - 1:V7X_SUBLANES, :]
            u_after = cgn_ref[0:1, :] * hzn_ref[0:1, :]
            local = lax.broadcasted_iota(jnp.int32, (bm, 1), 0)
            row = local + i * bm
            in_lat = row < t_lat
            seg_pos = jnp.where(in_lat, jnp.bitwise_and(row, seq - 1), jnp.bitwise_and(row - t_lat, ctx_len - 1))
            seg_len = jnp.where(in_lat, seq, ctx_len)
            up = jnp.where(local == 0, u_before, pltpu.roll(u, 1, 0))
            up = jnp.where(seg_pos == 0, 0.0, up)
            un = jnp.where(local == bm - 1, u_after, pltpu.roll(u, bm - 1, 0))
            un = jnp.where(seg_pos == seg_len - 1, 0.0, un)
            y = up * cw_ref[0:1, :] + u * cw_ref[1:2, :] + un * cw_ref[2:3, :] + cb_ref[...]
            cs_ref[...] = (bg_ref[...] * y).astype(BF16)

        acc = jnp.dot(a_ref[...], w_ref[0:a_width, :], preferred_element_type=F32)
        acc = acc + jnp.dot(cs_ref[...], w_ref[a_width:, :], preferred_element_type=F32)
    else:
        acc = jnp.dot(a_ref[...], w_ref[...], preferred_element_type=F32)
    o_ref[...] = x_ref[...] + gate_ref[0] * acc


def _mixout(a, w, x, gate, group_of, conv=None, *, t_lat, seq, ctx_len):
    t, d = x.shape
    bm, bn = ROW_BLOCK, 512
    a_width = a.shape[1]
    kw = dict(t_lat=t_lat, seq=seq, ctx_len=ctx_len, a_width=a_width)
    in_specs = [pl.BlockSpec((bm, a_width), lambda i, j: (i, 0))]
    args = [a]
    scratch = []
    if conv is not None:
        p, z_col0, cw, cb = conv
        cwid = cw.shape[1]
        assert z_col0 % cwid == 0
        zb = z_col0 // cwid
        sub = V7X_SUBLANES
        last_halo = t // sub - 1
        prev_idx = lambda i: jnp.maximum(i * (bm // sub) - 1, 0)
        next_idx = lambda i: jnp.minimum((i + 1) * (bm // sub), last_halo)
        in_specs += [
            pl.BlockSpec((bm, cwid), lambda i, j: (i, zb)),
            pl.BlockSpec((bm, cwid), lambda i, j: (i, zb + 1)),
            pl.BlockSpec((bm, cwid), lambda i, j: (i, zb + 2)),
            pl.BlockSpec((sub, cwid), lambda i, j: (prev_idx(i), zb + 1)),
            pl.BlockSpec((sub, cwid), lambda i, j: (prev_idx(i), zb + 2)),
            pl.BlockSpec((sub, cwid), lambda i, j: (next_idx(i), zb + 1)),
            pl.BlockSpec((sub, cwid), lambda i, j: (next_idx(i), zb + 2)),
            pl.BlockSpec(cw.shape, lambda i, j: (0, 0)),
            pl.BlockSpec((1, cwid), lambda i, j: (0, 0)),
        ]
        args += [p] * 7 + [cw, cb.reshape(1, cwid)]
        scratch = [pltpu.VMEM((bm, cwid), BF16)]
    in_specs += [
        pl.BlockSpec((w.shape[0], bn), lambda i, j: (0, j)),
        pl.BlockSpec((bm, bn), lambda i, j: (i, j)),
        pl.BlockSpec((1, 1, bn), lambda i, j: (group_of(i), 0, j)),
    ]
    args += [w, x, gate]
    return pl.pallas_call(
        functools.partial(_mixout_kernel, with_conv=conv is not None, **kw),
        grid=(t // bm, d // bn),
        in_specs=in_specs,
        out_specs=pl.BlockSpec((bm, bn), lambda i, j: (i, j)),
        out_shape=jax.ShapeDtypeStruct((t, d), F32),
        scratch_shapes=scratch,
        compiler_params=_cparams(("parallel", "arbitrary")),
        name="mixout_conv" if conv is not None else "mixout",
    )(*args)


_CAND_PAIRS = [(a, b) for a in range(PEER_TOPK) for b in range(PEER_TOPK // (a + 1))]
_CAND_ROWS = -(-len(_CAND_PAIRS) // V7X_SUBLANES) * V7X_SUBLANES
BF16_ROWS = 2 * V7X_SUBLANES


def _bf16_pair_words(v):
    bits = pltpu.bitcast(v.astype(BF16).astype(F32), jnp.uint32)
    return bits | (bits >> 16)


def _route_kernel(q_ref, sk_ref, r1_ref, p1_ref, np_ref, cand_ref):
    nk = N_KEYS
    k_top = PEER_TOPK
    lanes = V7X_LANES
    tb = q_ref.shape[0]
    cand_ref[len(_CAND_PAIRS):, :] = jnp.full((_CAND_ROWS - len(_CAND_PAIRS), lanes), NEG_INF, F32)
    for c in range(tb // lanes):
        cs = slice(c * lanes, (c + 1) * lanes)
        s0, s1 = (_nt_dot(sk_ref[0, half], q_ref[cs, half * nk:(half + 1) * nk].astype(BF16))
                  for half in range(2))

        cur = s0
        tops0 = []
        for k in range(k_top):
            m = jnp.max(cur, axis=0, keepdims=True)
            tops0.append(m)
            cur = jnp.where(cur == m, NEG_INF, cur)
        cur = s1
        rank = jnp.full((nk, lanes), float(k_top), F32)
        tops1 = []
        for k in range(k_top):
            m = jnp.max(cur, axis=0, keepdims=True)
            tops1.append(m)
            hit = cur == m
            rank = jnp.where(hit, float(k), rank)
            cur = jnp.where(hit, NEG_INF, cur)

        for r, (a, b) in enumerate(_CAND_PAIRS):
            cand_ref[r:r + 1, :] = tops0[a] + tops1[b]
        cand = cand_ref[...]
        top = tops0[0] + tops1[0]
        z = jnp.zeros((1, lanes), F32)
        for k in range(k_top):
            tau = jnp.max(cand, axis=0, keepdims=True)
            z = z + jnp.exp(tau - top)
            cand = jnp.where(cand == tau, NEG_INF, cand)

        n0 = jnp.zeros((nk, lanes), F32)
        for b in range(k_top):
            n0 = n0 + jnp.where(s0 + tops1[b] >= tau, 1.0, 0.0)
        n0 = _bf16_pair_words(n0)
        p0 = _bf16_pair_words(jnp.exp(s0 - tops0[0]))
        for g in range(nk // PEER_ROWS):
            np_ref[0, g, 0:PEER_ROWS, cs] = n0[g * PEER_ROWS:(g + 1) * PEER_ROWS, :]
            np_ref[0, g, PEER_ROWS:, cs] = p0[g * PEER_ROWS:(g + 1) * PEER_ROWS, :]
        rank_b = rank.astype(BF16)
        p1_b = (jnp.exp(s1 - tops1[0]) / z).astype(BF16)
        for g in range(nk // BF16_ROWS):
            r1_ref[0, g, :, cs] = pltpu.bitcast(rank_b[g * BF16_ROWS:(g + 1) * BF16_ROWS, :], jnp.int32)
            p1_ref[0, g, :, cs] = pltpu.bitcast(p1_b[g * BF16_ROWS:(g + 1) * BF16_ROWS, :], jnp.int32)


def _route(qp, sub_keys):
    t = qp.shape[0]
    tb = PEER_TB
    nk = N_KEYS
    heads = PEER_HEADS
    packed = jax.ShapeDtypeStruct((heads, nk // BF16_ROWS, V7X_SUBLANES, t), jnp.int32)
    packed_spec = pl.BlockSpec((1, nk // BF16_ROWS, V7X_SUBLANES, tb), lambda i, h: (h, 0, 0, i))
    plain = jax.ShapeDtypeStruct((heads, nk // PEER_ROWS, 2 * PEER_ROWS, t), jnp.uint32)
    plain_spec = pl.BlockSpec((1, nk // PEER_ROWS, 2 * PEER_ROWS, tb), lambda i, h: (h, 0, 0, i))
    return pl.pallas_call(
        _route_kernel,
        grid=(t // tb, heads),
        in_specs=[
            pl.BlockSpec((tb, 2 * nk), lambda i, h: (i, h)),
            pl.BlockSpec((1, 2, nk, sub_keys.shape[-1]), lambda i, h: (h, 0, 0, 0)),
        ],
        out_specs=[packed_spec, packed_spec, plain_spec],
        out_shape=[packed, packed, plain],
        scratch_shapes=[pltpu.VMEM((_CAND_ROWS, V7X_LANES), F32)],
        compiler_params=_cparams(("parallel", "arbitrary")),
        name="peer_route",
    )(qp, sub_keys)


def _peer_kernel(h_ref, u_ref, vt_ref, r1_ref, p1_ref, np_ref, x_ref, gate_ref,
                 o_ref, acc_ref, at0_ref, at1_ref, hs_ref):
    j = pl.program_id(1)
    n_blocks = pl.num_programs(1) - 1
    nk = N_KEYS
    lanes = V7X_LANES
    eb, tb = at0_ref.shape
    at_refs = (at0_ref, at1_ref)

    @pl.when(j == 0)
    def _():
        acc_ref[...] = jnp.zeros_like(acc_ref)
        at1_ref[...] = jnp.zeros((eb, tb), F32)

    jm = jnp.maximum(j - 1, 0)

    def key_row(hd, row, cs):
        words = jnp.broadcast_to(np_ref[hd, jm, row:row + 1, cs], (V7X_SUBLANES, lanes))
        return pltpu.bitcast(words, BF16)[None]

    def gate_rows(il, prev):
        for c in range(tb // lanes):
            cs = slice(c * lanes, (c + 1) * lanes)
            w = None
            for hd in range(PEER_HEADS):
                r1 = pltpu.bitcast(r1_ref[hd, :, :, cs], BF16)
                p1 = pltpu.bitcast(p1_ref[hd, :, :, cs], BF16)
                term = jnp.where(r1 < key_row(hd, il, cs), p1 * key_row(hd, PEER_ROWS + il, cs),
                                 jnp.zeros((), BF16))
                w = term if w is None else w + term
            a = at_refs[prev][il * nk:(il + 1) * nk, cs]
            gelu = (0.5 * a * (1.0 + lax.erf(a * SQRT_HALF))).astype(BF16)
            for g in range(nk // BF16_ROWS):
                r0 = il * nk + g * BF16_ROWS
                hs_ref[r0:r0 + BF16_ROWS, cs] = w[g] * gelu[g * BF16_ROWS:(g + 1) * BF16_ROWS, :]

    def step(cur, prev):
        at_refs[cur][...] = _nt_dot(pltpu.bitcast(u_ref[0], BF16), pltpu.bitcast(h_ref[...], BF16))
        for il in range(PEER_ROWS):
            gate_rows(il, prev)
        acc_ref[...] += jnp.dot(pltpu.bitcast(vt_ref[0, 0], BF16), hs_ref[...], preferred_element_type=F32)

    @pl.when(j % 2 == 0)
    def _():
        step(0, 1)

    @pl.when(j % 2 == 1)
    def _():
        step(1, 0)

    @pl.when(j == n_blocks)
    def _():
        o_ref[...] = x_ref[...] + gate_ref[0] * acc_ref[...].T


def _peer(h, u, vt, layer, r1, p1, n0p0, x, gate, group_of):
    t, d = x.shape
    tb, eb = PEER_TB, PEER_EB
    n_blocks = vt.shape[1]
    nk = N_KEYS
    heads = PEER_HEADS
    packed_spec = pl.BlockSpec((heads, nk // BF16_ROWS, V7X_SUBLANES, tb), lambda i, j: (0, 0, 0, i))
    plain_spec = pl.BlockSpec((heads, nk // PEER_ROWS, 2 * PEER_ROWS, tb), lambda i, j: (0, 0, 0, i))
    return pl.pallas_call(
        _peer_kernel,
        grid=(t // tb, n_blocks + 1),
        in_specs=[
            pl.BlockSpec((tb // 2, d), lambda i, j: (i, 0)),
            pl.BlockSpec((1, eb // 2, d), lambda i, j: (layer, jnp.minimum(j, n_blocks - 1), 0)),
            pl.BlockSpec((1, 1, d // 2, eb), lambda i, j: (layer, jnp.maximum(j - 1, 0), 0, 0)),
            packed_spec, packed_spec, plain_spec,
            pl.BlockSpec((tb, d), lambda i, j: (i, 0)),
            pl.BlockSpec((1, 1, d), lambda i, j: (group_of(i), 0, 0)),
        ],
        out_specs=pl.BlockSpec((tb, d), lambda i, j: (i, 0)),
        out_shape=jax.ShapeDtypeStruct((t, d), F32),
        scratch_shapes=[
            pltpu.VMEM((d, tb), F32),
            pltpu.VMEM((eb, tb), F32),
            pltpu.VMEM((eb, tb), F32),
            pltpu.VMEM((eb, tb), BF16),
        ],
        compiler_params=_cparams(("parallel", "arbitrary")),
        name="peer_mix",
    )(h, u, vt, r1, p1, n0p0, x, gate)


def _pack_table_kernel(x_ref, o_ref, *, transpose):
    x = x_ref[0]
    if transpose:
        o_ref[0, 0] = pltpu.bitcast(x.T.astype(BF16), jnp.uint32)
    else:
        o_ref[0] = pltpu.bitcast(x.astype(BF16), jnp.uint32)


def _pack_table(tab, transpose):
    layers, ne, d = tab.shape
    eb = PEER_EB
    if transpose:
        out_shape = jax.ShapeDtypeStruct((layers, ne // eb, d // 2, eb), jnp.uint32)
        out_spec = pl.BlockSpec((1, 1, d // 2, eb), lambda l, j: (l, j, 0, 0))
    else:
        out_shape = jax.ShapeDtypeStruct((layers, ne // 2, d), jnp.uint32)
        out_spec = pl.BlockSpec((1, eb // 2, d), lambda l, j: (l, j, 0))
    return pl.pallas_call(
        functools.partial(_pack_table_kernel, transpose=transpose),
        grid=(layers, ne // eb),
        in_specs=[pl.BlockSpec((1, eb, d), lambda l, j: (l, j, 0))],
        out_specs=out_spec,
        out_shape=out_shape,
        compiler_params=_cparams(("parallel", "parallel")),
        name="pack_table_t" if transpose else "pack_table",
    )(tab)


def _rope_tables(seq, rope_dim, batch, ctx_rows):
    rows = seq // GRID_W
    row = jnp.repeat(jnp.arange(rows, dtype=F32), GRID_W)
    col = jnp.tile(jnp.arange(GRID_W, dtype=F32), rows)
    quarter = rope_dim // 4
    freqs = ROPE_THETA ** (-jnp.arange(quarter, dtype=F32) / quarter)
    ang = jnp.concatenate([row[:, None] * freqs, col[:, None] * freqs], axis=-1)
    cos, sin = jnp.cos(ang), jnp.sin(ang)
    pad = V7X_LANES - rope_dim
    cos_l = jnp.pad(jnp.concatenate([cos, cos], axis=-1), ((0, 0), (0, pad)))
    sin_l = jnp.pad(jnp.concatenate([-sin, sin], axis=-1), ((0, 0), (0, pad)))
    cos_c = jnp.pad(jnp.ones((ctx_rows, rope_dim), F32), ((0, 0), (0, pad)))
    sin_c = jnp.zeros((ctx_rows, V7X_LANES), F32)
    return (jnp.concatenate([jnp.tile(cos_l, (batch, 1)), cos_c], axis=0),
            jnp.concatenate([jnp.tile(sin_l, (batch, 1)), sin_c], axis=0))


def _swap_halves(a):
    half = a.shape[-1] // 2
    return jnp.concatenate([a[..., half:], a[..., :half]], axis=-1)


def _pad_lanes(a):
    return jnp.pad(a, [(0, 0)] * (a.ndim - 1) + [(0, V7X_LANES - a.shape[-1])])


def _mla_weights(w_in, w_uq, q_g, k_g, q_lora, kv_lora):
    d = w_in.shape[0]
    c_kr = q_lora + kv_lora
    w_kr = w_in[:, c_kr:c_kr + MLA_ROPE]
    w_in_ext = jnp.concatenate([
        w_in[:, :c_kr], _pad_lanes(w_kr), _pad_lanes(_swap_halves(w_kr)), w_in[:, c_kr + MLA_ROPE:]], axis=1)
    wq = w_uq.reshape(q_lora, MLA_HEADS, MLA_NOPE + MLA_ROPE)
    wq_rope = wq[:, :, MLA_NOPE:]
    wq_ext = jnp.concatenate([wq[:, :, :MLA_NOPE], _pad_lanes(wq_rope), _pad_lanes(_swap_halves(wq_rope))], axis=-1)
    wq_ext = wq_ext.reshape(q_lora, MLA_HEADS * 3 * V7X_LANES)

    def gains(g):
        g_r = g[MLA_NOPE:]
        return [g[:MLA_NOPE].reshape(1, -1), _pad_lanes(g_r).reshape(1, -1), _pad_lanes(_swap_halves(g_r)).reshape(1, -1)]

    return w_in_ext.astype(BF16), wq_ext.astype(BF16), gains(q_g) + gains(k_g)


def kernel(x, c, ctx, c_ctx, ada_w, ada_b, norm1_g, norm2_g, a_w_in, a_q_lora_g, a_kv_lora_g, a_w_uq, a_w_ukv, a_q_g, a_k_g, b_conv_w, b_conv_b, e_w_o, c_w_qkv, c_q_g, c_k_g, c_w_o, p_w_q, p_sub_keys, p_u, p_v):
    batch, seq, d = x.shape
    ctx_len = ctx.shape[1]
    depth = ada_w.shape[0]
    q_lora = a_q_lora_g.shape[1]
    kv_lora = a_kv_lora_g.shape[1]
    conv_width = b_conv_w.shape[2]
    t_lat = batch * seq
    t_ctx = batch * ctx_len
    t = t_lat + t_ctx
    bm = ROW_BLOCK
    assert seq % bm == 0 and t_ctx % bm == 0 and t % PEER_TB == 0 and seq % ATT_Q_BLOCK == 0
    assert seq & (seq - 1) == 0 and ctx_len & (ctx_len - 1) == 0 and seq % GRID_W == 0
    assert batch + 1 <= V7X_SUBLANES

    blocks_per_batch = seq // bm
    group_of = lambda i: jnp.minimum(i // blocks_per_batch, batch)

    cvec = jnp.zeros((V7X_SUBLANES, d), F32).at[:batch].set(c).at[batch].set(c_ctx)
    mods = _ada(cvec, ada_w, ada_b)
    mods = mods.reshape(depth, V7X_SUBLANES, N_MOD, 1, d).transpose(0, 2, 1, 3, 4)

    cos_a, sin_a = _rope_tables(seq, MLA_ROPE, batch, t_ctx)
    cos_c, sin_c = _rope_tables(seq, GQA_HD, batch, t_ctx)

    u_all = _pack_table(p_u, transpose=False)
    vt_all = _pack_table(p_v, transpose=True)
    xs = jnp.concatenate([x.reshape(t_lat, d), ctx.reshape(t_ctx, d)], axis=0)
    for l in range(depth):
        sh1, sc1, g1, sh2, sc2, g2 = (mods[l, k] for k in range(N_MOD))
        i = l // 2
        if l % 2 == 0:
            w_in_ext, wq_ext, gains = _mla_weights(a_w_in[i], a_w_uq[i], a_q_g[i], a_k_g[i], q_lora, kv_lora)
            p = _normproj(xs, norm1_g[l], sc1, sh1, w_in_ext, group_of, emit_h=False)
            q, k, v = _mla_qkv(p, cos_a, sin_a, wq_ext, a_w_ukv[i].astype(BF16),
                               [a_q_lora_g[i].reshape(1, -1), a_kv_lora_g[i].reshape(1, -1)] + gains,
                               q_lora, kv_lora)
            att = _attention(q, k, v, batch=batch, seq=seq, ctx_len=ctx_len, kv_heads=MLA_HEADS, group=1,
                             dk=2 * V7X_LANES, dv=MLA_V)
            z_col0 = q_lora + kv_lora + 2 * V7X_LANES
            xs = _mixout(att, e_w_o[i].astype(BF16), xs, g1, group_of,
                         conv=(p, z_col0, b_conv_w[i], b_conv_b[i]), t_lat=t_lat, seq=seq, ctx_len=ctx_len)
        else:
            p = _normproj(xs, norm1_g[l], sc1, sh1, c_w_qkv[i].astype(BF16), group_of, emit_h=False)
            q, k, v = _gqa_qkv(p, cos_c, sin_c, c_q_g[i], c_k_g[i])
            att = _attention(q, k, v, batch=batch, seq=seq, ctx_len=ctx_len, kv_heads=GQA_KV_HEADS,
                             group=GQA_HEADS // GQA_KV_HEADS, dk=GQA_HD, dv=GQA_HD)
            xs = _mixout(att, c_w_o[i].astype(BF16), xs, g1, group_of, t_lat=t_lat, seq=seq, ctx_len=ctx_len)
        qp, h2 = _normproj(xs, norm2_g[l], sc2, sh2, p_w_q[l].astype(BF16), group_of, emit_h=True)
        r1, p1, n0p0 = _route(qp, p_sub_keys[l].astype(BF16))
        xs = _peer(h2, u_all, vt_all, l, r1, p1, n0p0, xs, g2, group_of)
    return xs[:t_lat].reshape(batch, seq, d)
```

```python
import functools
import math

import jax
import jax.numpy as jnp
from jax import lax
from jax.experimental import pallas as pl
from jax.experimental.pallas import tpu as pltpu

F32 = jnp.float32
BF16 = jnp.bfloat16

EPS = 1e-6
ROPE_THETA = 10000.0
GRID_W = 64
MLA_HEADS = 8
MLA_NOPE = 128
MLA_ROPE = 64
MLA_V = 128
GQA_HEADS = 16
GQA_KV_HEADS = 4
GQA_HD = 128
PEER_HEADS = 8
N_KEYS = 128
PEER_TOPK = 16
N_MOD = 6

V7X_LANES = 128
V7X_SUBLANES = 8
V7X_VMEM_LIMIT_BYTES = 56 * 1024 * 1024

ROW_BLOCK = 512
COL_BLOCK = 1024
ATT_Q_BLOCK = 256
PEER_TB = 512
PEER_EB = 1024
PEER_ROWS = PEER_EB // N_KEYS
SQRT_HALF = 0.7071067811865476
NEG_INF = float("-inf")


def _cparams(sem):
    return pltpu.CompilerParams(dimension_semantics=sem, vmem_limit_bytes=V7X_VMEM_LIMIT_BYTES)


def _nt_dot(a, b):
    return lax.dot_general(a, b, (((1,), (1,)), ((), ())), preferred_element_type=F32)


def _ada_kernel(c_ref, w_ref, b_ref, o_ref):
    c = c_ref[...]
    s = c / (1.0 + jnp.exp(-c))
    o_ref[0] = jnp.dot(s.astype(BF16), w_ref[0].astype(BF16), preferred_element_type=F32) + b_ref[0]


def _ada(cvec, ada_w, ada_b):
    depth, d, n = ada_w.shape
    nb = 1024
    rows = cvec.shape[0]
    return pl.pallas_call(
        _ada_kernel,
        grid=(depth, n // nb),
        in_specs=[
            pl.BlockSpec((rows, d), lambda l, j: (0, 0)),
            pl.BlockSpec((1, d, nb), lambda l, j: (l, 0, j)),
            pl.BlockSpec((1, 1, nb), lambda l, j: (l, 0, j)),
        ],
        out_specs=pl.BlockSpec((1, rows, nb), lambda l, j: (l, 0, j)),
        out_shape=jax.ShapeDtypeStruct((depth, rows, n), F32),
        compiler_params=_cparams(("parallel", "parallel")),
        name="ada_mod",
    )(cvec, ada_w, ada_b.reshape(depth, 1, n))


def _normproj_kernel(x_ref, g_ref, sc_ref, sh_ref, w_ref, o_ref, *rest, emit_h):
    hs_ref = rest[-1]

    @pl.when(pl.program_id(1) == 0)
    def _():
        x = x_ref[...]
        y = x * lax.rsqrt(jnp.mean(x * x, axis=-1, keepdims=True) + EPS) * g_ref[...]
        h = (y * (1.0 + sc_ref[0]) + sh_ref[0]).astype(BF16)
        hs_ref[...] = h
        if emit_h:
            rest[0][...] = pltpu.bitcast(h, jnp.uint32)

    o_ref[...] = jnp.dot(hs_ref[...], w_ref[...], preferred_element_type=F32)


def _normproj(x, g, sc, sh, w, group_of, emit_h):
    t, d = x.shape
    n = w.shape[1]
    bm, bn = ROW_BLOCK, COL_BLOCK
    out_shape = [jax.ShapeDtypeStruct((t, n), F32)]
    out_specs = [pl.BlockSpec((bm, bn), lambda i, j: (i, j))]
    if emit_h:
        out_shape.append(jax.ShapeDtypeStruct((t // 2, d), jnp.uint32))
        out_specs.append(pl.BlockSpec((bm // 2, d), lambda i, j: (i, 0)))
    res = pl.pallas_call(
        functools.partial(_normproj_kernel, emit_h=emit_h),
        grid=(t // bm, n // bn),
        in_specs=[
            pl.BlockSpec((bm, d), lambda i, j: (i, 0)),
            pl.BlockSpec((1, d), lambda i, j: (0, 0)),
            pl.BlockSpec((1, 1, d), lambda i, j: (group_of(i), 0, 0)),
            pl.BlockSpec((1, 1, d), lambda i, j: (group_of(i), 0, 0)),
            pl.BlockSpec((d, bn), lambda i, j: (0, j)),
        ],
        out_specs=out_specs,
        out_shape=out_shape,
        scratch_shapes=[pltpu.VMEM((bm, d), BF16)],
        compiler_params=_cparams(("parallel", "arbitrary")),
        name="normproj_h" if emit_h else "normproj",
    )(x, g.reshape(1, d), sc, sh, w)
    return res if emit_h else res[0]


def _mla_qkv_kernel(p_ref, cos_ref, sin_ref, wuq_ref, wukv_ref, qlg_ref, kvlg_ref,
                    qgn_ref, qgr_ref, qgs_ref, kgn_ref, kgr_ref, kgs_ref,
                    q_ref, k_ref, v_ref, *, q_lora, kv_lora, scale):
    qk_dim = MLA_NOPE + MLA_ROPE
    lanes = V7X_LANES
    cq = p_ref[:, 0:q_lora]
    ckv = p_ref[:, q_lora:q_lora + kv_lora]
    kr = p_ref[:, q_lora + kv_lora:q_lora + kv_lora + lanes]
    krs = p_ref[:, q_lora + kv_lora + lanes:q_lora + kv_lora + 2 * lanes]
    cos = cos_ref[...]
    sin = sin_ref[...]

    cqn = cq * lax.rsqrt(jnp.mean(cq * cq, axis=-1, keepdims=True) + EPS) * qlg_ref[...]
    qraw = jnp.dot(cqn.astype(BF16), wuq_ref[...], preferred_element_type=F32)
    ckvn = ckv * lax.rsqrt(jnp.mean(ckv * ckv, axis=-1, keepdims=True) + EPS) * kvlg_ref[...]
    kvraw = jnp.dot(ckvn.astype(BF16), wukv_ref[...], preferred_element_type=F32)

    kr_rot = kr * kgr_ref[...] * cos + krs * kgs_ref[...] * sin
    kr_ssq = jnp.sum(kr * kr, axis=-1, keepdims=True)
    for h in range(MLA_HEADS):
        nope = qraw[:, h * 3 * lanes:h * 3 * lanes + lanes]
        rope = qraw[:, h * 3 * lanes + lanes:h * 3 * lanes + 2 * lanes]
        rope_sw = qraw[:, h * 3 * lanes + 2 * lanes:h * 3 * lanes + 3 * lanes]
        ssq = jnp.sum(nope * nope, axis=-1, keepdims=True) + jnp.sum(rope * rope, axis=-1, keepdims=True)
        r = lax.rsqrt(ssq * (1.0 / qk_dim) + EPS) * scale
        q_ref[:, h * 2 * lanes:h * 2 * lanes + lanes] = (nope * r * qgn_ref[...]).astype(BF16)
        q_ref[:, h * 2 * lanes + lanes:(h + 1) * 2 * lanes] = (
            (rope * qgr_ref[...] * cos + rope_sw * qgs_ref[...] * sin) * r).astype(BF16)

        k_nope = kvraw[:, h * 2 * lanes:h * 2 * lanes + lanes]
        v = kvraw[:, h * 2 * lanes + lanes:(h + 1) * 2 * lanes]
        kssq = jnp.sum(k_nope * k_nope, axis=-1, keepdims=True) + kr_ssq
        rk = lax.rsqrt(kssq * (1.0 / qk_dim) + EPS)
        k_ref[:, h * 2 * lanes:h * 2 * lanes + lanes] = (k_nope * rk * kgn_ref[...]).astype(BF16)
        k_ref[:, h * 2 * lanes + lanes:(h + 1) * 2 * lanes] = (kr_rot * rk).astype(BF16)
        v_ref[:, h * lanes:(h + 1) * lanes] = v.astype(BF16)


def _mla_qkv(p, cos, sin, wuq, wukv, gains, q_lora, kv_lora):
    t = p.shape[0]
    bm = ROW_BLOCK
    lanes = V7X_LANES
    head_cols = q_lora + kv_lora + 2 * lanes
    full = lambda a: pl.BlockSpec(a.shape, lambda i: (0,) * a.ndim)
    return pl.pallas_call(
        functools.partial(_mla_qkv_kernel, q_lora=q_lora, kv_lora=kv_lora,
                          scale=float(MLA_NOPE + MLA_ROPE) ** -0.5),
        grid=(t // bm,),
        in_specs=[
            pl.BlockSpec((bm, head_cols), lambda i: (i, 0)),
            pl.BlockSpec((bm, lanes), lambda i: (i, 0)),
            pl.BlockSpec((bm, lanes), lambda i: (i, 0)),
            full(wuq), full(wukv)] + [full(g) for g in gains],
        out_specs=[
            pl.BlockSpec((bm, MLA_HEADS * 2 * lanes), lambda i: (i, 0)),
            pl.BlockSpec((bm, MLA_HEADS * 2 * lanes), lambda i: (i, 0)),
            pl.BlockSpec((bm, MLA_HEADS * lanes), lambda i: (i, 0)),
        ],
        out_shape=[
            jax.ShapeDtypeStruct((t, MLA_HEADS * 2 * lanes), BF16),
            jax.ShapeDtypeStruct((t, MLA_HEADS * 2 * lanes), BF16),
            jax.ShapeDtypeStruct((t, MLA_HEADS * lanes), BF16),
        ],
        compiler_params=_cparams(("parallel",)),
        name="mla_qkv",
    )(p, cos, sin, wuq, wukv, *gains)


def _gqa_qkv_kernel(p_ref, cos_ref, sin_ref, qg_ref, kg_ref, q_ref, k_ref, v_ref, *, scale):
    hd = GQA_HD
    cos = cos_ref[...]
    sin = sin_ref[...]

    def head(x, g):
        y = x * lax.rsqrt(jnp.mean(x * x, axis=-1, keepdims=True) + EPS) * g
        return y * cos + pltpu.roll(y, hd // 2, 1) * sin

    for h in range(GQA_HEADS):
        q_ref[:, h * hd:(h + 1) * hd] = (head(p_ref[:, h * hd:(h + 1) * hd], qg_ref[...]) * scale).astype(BF16)
    k0 = GQA_HEADS * hd
    v0 = k0 + GQA_KV_HEADS * hd
    for h in range(GQA_KV_HEADS):
        k_ref[:, h * hd:(h + 1) * hd] = head(p_ref[:, k0 + h * hd:k0 + (h + 1) * hd], kg_ref[...]).astype(BF16)
    v_ref[...] = p_ref[:, v0:v0 + GQA_KV_HEADS * hd].astype(BF16)


def _gqa_qkv(p, cos, sin, qg, kg):
    t, n = p.shape
    bm = ROW_BLOCK
    hd = GQA_HD
    return pl.pallas_call(
        functools.partial(_gqa_qkv_kernel, scale=float(hd) ** -0.5),
        grid=(t // bm,),
        in_specs=[
            pl.BlockSpec((bm, n), lambda i: (i, 0)),
            pl.BlockSpec((bm, hd), lambda i: (i, 0)),
            pl.BlockSpec((bm, hd), lambda i: (i, 0)),
            pl.BlockSpec((1, hd), lambda i: (0, 0)),
            pl.BlockSpec((1, hd), lambda i: (0, 0)),
        ],
        out_specs=[
            pl.BlockSpec((bm, GQA_HEADS * hd), lambda i: (i, 0)),
            pl.BlockSpec((bm, GQA_KV_HEADS * hd), lambda i: (i, 0)),
            pl.BlockSpec((bm, GQA_KV_HEADS * hd), lambda i: (i, 0)),
        ],
        out_shape=[
            jax.ShapeDtypeStruct((t, GQA_HEADS * hd), BF16),
            jax.ShapeDtypeStruct((t, GQA_KV_HEADS * hd), BF16),
            jax.ShapeDtypeStruct((t, GQA_KV_HEADS * hd), BF16),
        ],
        compiler_params=_cparams(("parallel",)),
        name="gqa_qkv",
    )(p, cos, sin, qg.reshape(1, hd), kg.reshape(1, hd))


def _attn_kernel(q_ref, kc_ref, vc_ref, *rest, group, dk, dv, with_latent):
    if with_latent:
        kl_ref, vl_ref, o_ref = rest
    else:
        (o_ref,) = rest
    for g in range(group):
        q = q_ref[:, g * dk:(g + 1) * dk]
        sc = _nt_dot(q, kc_ref[...])
        m = jnp.max(sc, axis=-1, keepdims=True)
        if with_latent:
            sl = _nt_dot(q, kl_ref[...])
            m = jnp.maximum(m, jnp.max(sl, axis=-1, keepdims=True))
        pc = jnp.exp(sc - m)
        den = jnp.sum(pc, axis=-1, keepdims=True)
        o = jnp.dot(pc.astype(BF16), vc_ref[...], preferred_element_type=F32)
        if with_latent:
            pl_ = jnp.exp(sl - m)
            den = den + jnp.sum(pl_, axis=-1, keepdims=True)
            o = o + jnp.dot(pl_.astype(BF16), vl_ref[...], preferred_element_type=F32)
        o_ref[:, g * dv:(g + 1) * dv] = (o / den).astype(BF16)


def _attention(q, k, v, *, batch, seq, ctx_len, kv_heads, group, dk, dv):
    t_lat = batch * seq
    tq = ATT_Q_BLOCK
    nq = seq // tq
    ctx_blk0 = t_lat // ctx_len
    common = dict(group=group, dk=dk, dv=dv)
    lat = pl.pallas_call(
        functools.partial(_attn_kernel, with_latent=True, **common),
        grid=(batch, kv_heads, nq),
        in_specs=[
            pl.BlockSpec((tq, group * dk), lambda b, h, j: (b * nq + j, h)),
            pl.BlockSpec((ctx_len, dk), lambda b, h, j: (ctx_blk0 + b, h)),
            pl.BlockSpec((ctx_len, dv), lambda b, h, j: (ctx_blk0 + b, h)),
            pl.BlockSpec((seq, dk), lambda b, h, j: (b, h)),
            pl.BlockSpec((seq, dv), lambda b, h, j: (b, h)),
        ],
        out_specs=pl.BlockSpec((tq, group * dv), lambda b, h, j: (b * nq + j, h)),
        out_shape=jax.ShapeDtypeStruct((t_lat, kv_heads * group * dv), BF16),
        compiler_params=_cparams(("parallel", "parallel", "arbitrary")),
        name="attn_latent",
    )(q, k, v, k, v)
    ctx = pl.pallas_call(
        functools.partial(_attn_kernel, with_latent=False, **common),
        grid=(batch, kv_heads),
        in_specs=[
            pl.BlockSpec((ctx_len, group * dk), lambda b, h: (ctx_blk0 + b, h)),
            pl.BlockSpec((ctx_len, dk), lambda b, h: (ctx_blk0 + b, h)),
            pl.BlockSpec((ctx_len, dv), lambda b, h: (ctx_blk0 + b, h)),
        ],
        out_specs=pl.BlockSpec((ctx_len, group * dv), lambda b, h: (b, h)),
        out_shape=jax.ShapeDtypeStruct((batch * ctx_len, kv_heads * group * dv), BF16),
        compiler_params=_cparams(("parallel", "parallel")),
        name="attn_ctx",
    )(q, k, v)
    return jnp.concatenate([lat, ctx], axis=0)


def _mixout_kernel(*refs, with_conv, t_lat, seq, ctx_len, a_width):
    if with_conv:
        (a_ref, bg_ref, cg_ref, hz_ref, cgp_ref, hzp_ref, cgn_ref, hzn_ref, cw_ref, cb_ref,
         w_ref, x_ref, gate_ref, o_ref, cs_ref) = refs
    else:
        a_ref, w_ref, x_ref, gate_ref, o_ref = refs
    i = pl.program_id(0)

    if with_conv:
        @pl.when(pl.program_id(1) == 0)
        def _():
            bm = cg_ref.shape[0]
            u = cg_ref[...] * hz_ref[...]
            u_before = cgp_ref[V7X_SUBLANES - 1:V7X_SUBLANES, :] * hzp_ref[V7X_SUBLANES - 1:V7X_SUBLANES, :]
            u_after = cgn_ref[0:1, :] * hzn_ref[0:1, :]
            local = lax.broadcasted_iota(jnp.int32, (bm, 1), 0)
            row = local + i * bm
            in_lat = row < t_lat
            seg_pos = jnp.where(in_lat, jnp.bitwise_and(row, seq - 1), jnp.bitwise_and(row - t_lat, ctx_len - 1))
            seg_len = jnp.where(in_lat, seq, ctx_len)
            up = jnp.where(local == 0, u_before, pltpu.roll(u, 1, 0))
            up = jnp.where(seg_pos == 0, 0.0, up)
            un = jnp.where(local == bm - 1, u_after, pltpu.roll(u, bm - 1, 0))
            un = jnp.where(seg_pos == seg_len - 1, 0.0, un)
            y = up * cw_ref[0:1, :] + u * cw_ref[1:2, :] + un * cw_ref[2:3, :] + cb_ref[...]
            cs_ref[...] = (bg_ref[...] * y).astype(BF16)

        acc = jnp.dot(a_ref[...], w_ref[0:a_width, :], preferred_element_type=F32)
        acc = acc + jnp.dot(cs_ref[...], w_ref[a_width:, :], preferred_element_type=F32)
    else:
        acc = jnp.dot(a_ref[...], w_ref[...], preferred_element_type=F32)
    o_ref[...] = x_ref[...] + gate_ref[0] * acc


def _mixout(a, w, x, gate, group_of, conv=None, *, t_lat, seq, ctx_len):
    t, d = x.shape
    bm, bn = ROW_BLOCK, COL_BLOCK
    a_width = a.shape[1]
    kw = dict(t_lat=t_lat, seq=seq, ctx_len=ctx_len, a_width=a_width)
    in_specs = [pl.BlockSpec((bm, a_width), lambda i, j: (i, 0))]
    args = [a]
    scratch = []
    if conv is not None:
        p, z_col0, cw, cb = conv
        cwid = cw.shape[1]
        assert z_col0 % cwid == 0
        zb = z_col0 // cwid
        sub = V7X_SUBLANES
        last_halo = t // sub - 1
        prev_idx = lambda i: jnp.maximum(i * (bm // sub) - 1, 0)
        next_idx = lambda i: jnp.minimum((i + 1) * (bm // sub), last_halo)
        in_specs += [
            pl.BlockSpec((bm, cwid), lambda i, j: (i, zb)),
            pl.BlockSpec((bm, cwid), lambda i, j: (i, zb + 1)),
            pl.BlockSpec((bm, cwid), lambda i, j: (i, zb + 2)),
            pl.BlockSpec((sub, cwid), lambda i, j: (prev_idx(i), zb + 1)),
            pl.BlockSpec((sub, cwid), lambda i, j: (prev_idx(i), zb + 2)),
            pl.BlockSpec((sub, cwid), lambda i, j: (next_idx(i), zb + 1)),
            pl.BlockSpec((sub, cwid), lambda i, j: (next_idx(i), zb + 2)),
            pl.BlockSpec(cw.shape, lambda i, j: (0, 0)),
            pl.BlockSpec((1, cwid), lambda i, j: (0, 0)),
        ]
        args += [p] * 7 + [cw, cb.reshape(1, cwid)]
        scratch = [pltpu.VMEM((bm, cwid), BF16)]
    in_specs += [
        pl.BlockSpec((w.shape[0], bn), lambda i, j: (0, j)),
        pl.BlockSpec((bm, bn), lambda i, j: (i, j)),
        pl.BlockSpec((1, 1, bn), lambda i, j: (group_of(i), 0, j)),
    ]
    args += [w, x, gate]
    return pl.pallas_call(
        functools.partial(_mixout_kernel, with_conv=conv is not None, **kw),
        grid=(t // bm, d // bn),
        in_specs=in_specs,
        out_specs=pl.BlockSpec((bm, bn), lambda i, j: (i, j)),
        out_shape=jax.ShapeDtypeStruct((t, d), F32),
        scratch_shapes=scratch,
        compiler_params=_cparams(("parallel", "arbitrary")),
        name="mixout_conv" if conv is not None else "mixout",
    )(*args)


_CAND_PAIRS = [(a, b) for a in range(PEER_TOPK) for b in range(PEER_TOPK // (a + 1))]
_CAND_ROWS = -(-len(_CAND_PAIRS) // V7X_SUBLANES) * V7X_SUBLANES
BF16_ROWS = 2 * V7X_SUBLANES


def _bf16_pair_words(v):
    bits = pltpu.bitcast(v.astype(BF16).astype(F32), jnp.uint32)
    return bits | (bits >> 16)


def _route_kernel(q_ref, sk_ref, r1_ref, p1_ref, np_ref, cand_ref):
    nk = N_KEYS
    k_top = PEER_TOPK
    lanes = V7X_LANES
    tb = q_ref.shape[0]
    cand_ref[len(_CAND_PAIRS):, :] = jnp.full((_CAND_ROWS - len(_CAND_PAIRS), lanes), NEG_INF, F32)
    for c in range(tb // lanes):
        cs = slice(c * lanes, (c + 1) * lanes)
        s0, s1 = (_nt_dot(sk_ref[0, half], q_ref[cs, half * nk:(half + 1) * nk].astype(BF16))
                  for half in range(2))

        cur = s0
        tops0 = []
        for k in range(k_top):
            m = jnp.max(cur, axis=0, keepdims=True)
            tops0.append(m)
            cur = jnp.where(cur == m, NEG_INF, cur)
        cur = s1
        rank = jnp.full((nk, lanes), float(k_top), F32)
        tops1 = []
        for k in range(k_top):
            m = jnp.max(cur, axis=0, keepdims=True)
            tops1.append(m)
            hit = cur == m
            rank = jnp.where(hit, float(k), rank)
            cur = jnp.where(hit, NEG_INF, cur)

        for r, (a, b) in enumerate(_CAND_PAIRS):
            cand_ref[r:r + 1, :] = tops0[a] + tops1[b]
        cand = cand_ref[...]
        top = tops0[0] + tops1[0]
        z = jnp.zeros((1, lanes), F32)
        for k in range(k_top):
            tau = jnp.max(cand, axis=0, keepdims=True)
            z = z + jnp.exp(tau - top)
            cand = jnp.where(cand == tau, NEG_INF, cand)

        n0 = jnp.zeros((nk, lanes), F32)
        for b in range(k_top):
            n0 = n0 + jnp.where(s0 + tops1[b] >= tau, 1.0, 0.0)
        n0 = _bf16_pair_words(n0)
        p0 = _bf16_pair_words(jnp.exp(s0 - tops0[0]))
        for g in range(nk // PEER_ROWS):
            np_ref[0, g, 0:PEER_ROWS, cs] = n0[g * PEER_ROWS:(g + 1) * PEER_ROWS, :]
            np_ref[0, g, PEER_ROWS:, cs] = p0[g * PEER_ROWS:(g + 1) * PEER_ROWS, :]
        rank_b = rank.astype(BF16)
        p1_b = (jnp.exp(s1 - tops1[0]) / z).astype(BF16)
        for g in range(nk // BF16_ROWS):
            r1_ref[0, g, :, cs] = pltpu.bitcast(rank_b[g * BF16_ROWS:(g + 1) * BF16_ROWS, :], jnp.int32)
            p1_ref[0, g, :, cs] = pltpu.bitcast(p1_b[g * BF16_ROWS:(g + 1) * BF16_ROWS, :], jnp.int32)


def _route(qp, sub_keys):
    t = qp.shape[0]
    tb = PEER_TB
    nk = N_KEYS
    heads = PEER_HEADS
    packed = jax.ShapeDtypeStruct((heads, nk // BF16_ROWS, V7X_SUBLANES, t), jnp.int32)
    packed_spec = pl.BlockSpec((1, nk // BF16_ROWS, V7X_SUBLANES, tb), lambda i, h: (h, 0, 0, i))
    plain = jax.ShapeDtypeStruct((heads, nk // PEER_ROWS, 2 * PEER_ROWS, t), jnp.uint32)
    plain_spec = pl.BlockSpec((1, nk // PEER_ROWS, 2 * PEER_ROWS, tb), lambda i, h: (h, 0, 0, i))
    return pl.pallas_call(
        _route_kernel,
        grid=(t // tb, heads),
        in_specs=[
            pl.BlockSpec((tb, 2 * nk), lambda i, h: (i, h)),
            pl.BlockSpec((1, 2, nk, sub_keys.shape[-1]), lambda i, h: (h, 0, 0, 0)),
        ],
        out_specs=[packed_spec, packed_spec, plain_spec],
        out_shape=[packed, packed, plain],
        scratch_shapes=[pltpu.VMEM((_CAND_ROWS, V7X_LANES), F32)],
        compiler_params=_cparams(("parallel", "arbitrary")),
        name="peer_route",
    )(qp, sub_keys)


def _peer_kernel(h_ref, u_ref, vt_ref, r1_ref, p1_ref, np_ref, x_ref, gate_ref,
                 o_ref, acc_ref, at0_ref, at1_ref, hs_ref, *, n_blocks):
    j = pl.program_id(1)
    nk = N_KEYS
    lanes = V7X_LANES
    eb, tb = at0_ref.shape
    at_refs = (at0_ref, at1_ref)
    jm = jnp.maximum(j - 1, 0)

    def key_row(hd, row, cs):
        words = jnp.broadcast_to(np_ref[hd, jm, row:row + 1, cs], (V7X_SUBLANES, lanes))
        return pltpu.bitcast(words, BF16)[None]

    def gate_rows(il, prev):
        for c in range(tb // lanes):
            cs = slice(c * lanes, (c + 1) * lanes)
            w = None
            for hd in range(PEER_HEADS):
                r1 = pltpu.bitcast(r1_ref[hd, :, :, cs], BF16)
                p1 = pltpu.bitcast(p1_ref[hd, :, :, cs], BF16)
                term = jnp.where(r1 < key_row(hd, il, cs), p1 * key_row(hd, PEER_ROWS + il, cs),
                                 jnp.zeros((), BF16))
                w = term if w is None else w + term
            a = at_refs[prev][il * nk:(il + 1) * nk, cs]
            gelu = (0.5 * a * (1.0 + lax.erf(a * SQRT_HALF))).astype(BF16)
            for g in range(nk // BF16_ROWS):
                r0 = il * nk + g * BF16_ROWS
                hs_ref[r0:r0 + BF16_ROWS, cs] = w[g] * gelu[g * BF16_ROWS:(g + 1) * BF16_ROWS, :]

    def first_matmul(cur):
        at_refs[cur][...] = _nt_dot(pltpu.bitcast(u_ref[0], BF16), pltpu.bitcast(h_ref[...], BF16))

    def mix(prev):
        for il in range(PEER_ROWS):
            gate_rows(il, prev)
        acc_ref[...] += jnp.dot(pltpu.bitcast(vt_ref[0, 0], BF16), hs_ref[...], preferred_element_type=F32)

    @pl.when(j == 0)
    def _():
        acc_ref[...] = jnp.zeros_like(acc_ref)
        first_matmul(0)

    for parity in range(2):
        @pl.when(jnp.logical_and(jnp.logical_and(j > 0, j < n_blocks), j % 2 == parity))
        def _(parity=parity):
            first_matmul(parity)
            mix(1 - parity)

    @pl.when(j == n_blocks)
    def _():
        mix((n_blocks - 1) % 2)
        o_ref[...] = x_ref[...] + gate_ref[0] * acc_ref[...].T


def _peer(h, u, vt, layer, r1, p1, n0p0, x, gate, group_of):
    t, d = x.shape
    tb, eb = PEER_TB, PEER_EB
    n_blocks = vt.shape[1]
    nk = N_KEYS
    heads = PEER_HEADS
    packed_spec = pl.BlockSpec((heads, nk // BF16_ROWS, V7X_SUBLANES, tb), lambda i, j: (0, 0, 0, i))
    plain_spec = pl.BlockSpec((heads, nk // PEER_ROWS, 2 * PEER_ROWS, tb), lambda i, j: (0, 0, 0, i))
    once = dict(pipeline_mode=pl.Buffered(1))
    return pl.pallas_call(
        functools.partial(_peer_kernel, n_blocks=n_blocks),
        grid=(t // tb, n_blocks + 1),
        in_specs=[
            pl.BlockSpec((tb // 2, d), lambda i, j: (i, 0), **once),
            pl.BlockSpec((1, eb // 2, d), lambda i, j: (layer, jnp.minimum(j, n_blocks - 1), 0)),
            pl.BlockSpec((1, 1, d // 2, eb), lambda i, j: (layer, jnp.maximum(j - 1, 0), 0, 0)),
            pl.BlockSpec(packed_spec.block_shape, packed_spec.index_map, **once),
            pl.BlockSpec(packed_spec.block_shape, packed_spec.index_map, **once),
            pl.BlockSpec(plain_spec.block_shape, plain_spec.index_map, **once),
            pl.BlockSpec((tb, d), lambda i, j: (i, 0), **once),
            pl.BlockSpec((1, 1, d), lambda i, j: (group_of(i), 0, 0)),
        ],
        out_specs=pl.BlockSpec((tb, d), lambda i, j: (i, 0)),
        out_shape=jax.ShapeDtypeStruct((t, d), F32),
        scratch_shapes=[
            pltpu.VMEM((d, tb), F32),
            pltpu.VMEM((eb, tb), F32),
            pltpu.VMEM((eb, tb), F32),
            pltpu.VMEM((eb, tb), BF16),
        ],
        compiler_params=_cparams(("parallel", "arbitrary")),
        name="peer_mix",
    )(h, u, vt, r1, p1, n0p0, x, gate)


def _pack_table_kernel(x_ref, o_ref, *, transpose):
    x = x_ref[0]
    if transpose:
        o_ref[0, 0] = pltpu.bitcast(x.T.astype(BF16), jnp.uint32)
    else:
        o_ref[0] = pltpu.bitcast(x.astype(BF16), jnp.uint32)


def _pack_table(tab, transpose):
    layers, ne, d = tab.shape
    eb = PEER_EB
    if transpose:
        out_shape = jax.ShapeDtypeStruct((layers, ne // eb, d // 2, eb), jnp.uint32)
        out_spec = pl.BlockSpec((1, 1, d // 2, eb), lambda l, j: (l, j, 0, 0))
    else:
        out_shape = jax.ShapeDtypeStruct((layers, ne // 2, d), jnp.uint32)
        out_spec = pl.BlockSpec((1, eb // 2, d), lambda l, j: (l, j, 0))
    return pl.pallas_call(
        functools.partial(_pack_table_kernel, transpose=transpose),
        grid=(layers, ne // eb),
        in_specs=[pl.BlockSpec((1, eb, d), lambda l, j: (l, j, 0))],
        out_specs=out_spec,
        out_shape=out_shape,
        compiler_params=_cparams(("parallel", "parallel")),
        name="pack_table_t" if transpose else "pack_table",
    )(tab)


def _rope_tables(seq, rope_dim, batch, ctx_rows):
    rows = seq // GRID_W
    row = jnp.repeat(jnp.arange(rows, dtype=F32), GRID_W)
    col = jnp.tile(jnp.arange(GRID_W, dtype=F32), rows)
    quarter = rope_dim // 4
    freqs = ROPE_THETA ** (-jnp.arange(quarter, dtype=F32) / quarter)
    ang = jnp.concatenate([row[:, None] * freqs, col[:, None] * freqs], axis=-1)
    cos, sin = jnp.cos(ang), jnp.sin(ang)
    pad = V7X_LANES - rope_dim
    cos_l = jnp.pad(jnp.concatenate([cos, cos], axis=-1), ((0, 0), (0, pad)))
    sin_l = jnp.pad(jnp.concatenate([-sin, sin], axis=-1), ((0, 0), (0, pad)))
    cos_c = jnp.pad(jnp.ones((ctx_rows, rope_dim), F32), ((0, 0), (0, pad)))
    sin_c = jnp.zeros((ctx_rows, V7X_LANES), F32)
    return (jnp.concatenate([jnp.tile(cos_l, (batch, 1)), cos_c], axis=0),
            jnp.concatenate([jnp.tile(sin_l, (batch, 1)), sin_c], axis=0))


def _swap_halves(a):
    half = a.shape[-1] // 2
    return jnp.concatenate([a[..., half:], a[..., :half]], axis=-1)


def _pad_lanes(a):
    return jnp.pad(a, [(0, 0)] * (a.ndim - 1) + [(0, V7X_LANES - a.shape[-1])])


def _mla_weights(w_in, w_uq, q_g, k_g, q_lora, kv_lora):
    d = w_in.shape[0]
    c_kr = q_lora + kv_lora
    w_kr = w_in[:, c_kr:c_kr + MLA_ROPE]
    w_in_ext = jnp.concatenate([
        w_in[:, :c_kr], _pad_lanes(w_kr), _pad_lanes(_swap_halves(w_kr)), w_in[:, c_kr + MLA_ROPE:]], axis=1)
    wq = w_uq.reshape(q_lora, MLA_HEADS, MLA_NOPE + MLA_ROPE)
    wq_rope = wq[:, :, MLA_NOPE:]
    wq_ext = jnp.concatenate([wq[:, :, :MLA_NOPE], _pad_lanes(wq_rope), _pad_lanes(_swap_halves(wq_rope))], axis=-1)
    wq_ext = wq_ext.reshape(q_lora, MLA_HEADS * 3 * V7X_LANES)

    def gains(g):
        g_r = g[MLA_NOPE:]
        return [g[:MLA_NOPE].reshape(1, -1), _pad_lanes(g_r).reshape(1, -1), _pad_lanes(_swap_halves(g_r)).reshape(1, -1)]

    return w_in_ext.astype(BF16), wq_ext.astype(BF16), gains(q_g) + gains(k_g)


def kernel(x, c, ctx, c_ctx, ada_w, ada_b, norm1_g, norm2_g, a_w_in, a_q_lora_g, a_kv_lora_g, a_w_uq, a_w_ukv, a_q_g, a_k_g, b_conv_w, b_conv_b, e_w_o, c_w_qkv, c_q_g, c_k_g, c_w_o, p_w_q, p_sub_keys, p_u, p_v):
    batch, seq, d = x.shape
    ctx_len = ctx.shape[1]
    depth = ada_w.shape[0]
    q_lora = a_q_lora_g.shape[1]
    kv_lora = a_kv_lora_g.shape[1]
    conv_width = b_conv_w.shape[2]
    t_lat = batch * seq
    t_ctx = batch * ctx_len
    t = t_lat + t_ctx
    bm = ROW_BLOCK
    assert seq % bm == 0 and t_ctx % bm == 0 and t % PEER_TB == 0 and seq % ATT_Q_BLOCK == 0
    assert seq & (seq - 1) == 0 and ctx_len & (ctx_len - 1) == 0 and seq % GRID_W == 0
    assert batch + 1 <= V7X_SUBLANES

    blocks_per_batch = seq // bm
    group_of = lambda i: jnp.minimum(i // blocks_per_batch, batch)

    cvec = jnp.zeros((V7X_SUBLANES, d), F32).at[:batch].set(c).at[batch].set(c_ctx)
    mods = _ada(cvec, ada_w, ada_b)
    mods = mods.reshape(depth, V7X_SUBLANES, N_MOD, 1, d).transpose(0, 2, 1, 3, 4)

    cos_a, sin_a = _rope_tables(seq, MLA_ROPE, batch, t_ctx)
    cos_c, sin_c = _rope_tables(seq, GQA_HD, batch, t_ctx)

    u_all = _pack_table(p_u, transpose=False)
    vt_all = _pack_table(p_v, transpose=True)
    xs = jnp.concatenate([x.reshape(t_lat, d), ctx.reshape(t_ctx, d)], axis=0)
    for l in range(depth):
        sh1, sc1, g1, sh2, sc2, g2 = (mods[l, k] for k in range(N_MOD))
        i = l // 2
        if l % 2 == 0:
            w_in_ext, wq_ext, gains = _mla_weights(a_w_in[i], a_w_uq[i], a_q_g[i], a_k_g[i], q_lora, kv_lora)
            p = _normproj(xs, norm1_g[l], sc1, sh1, w_in_ext, group_of, emit_h=False)
            q, k, v = _mla_qkv(p, cos_a, sin_a, wq_ext, a_w_ukv[i].astype(BF16),
                               [a_q_lora_g[i].reshape(1, -1), a_kv_lora_g[i].reshape(1, -1)] + gains,
                               q_lora, kv_lora)
            att = _attention(q, k, v, batch=batch, seq=seq, ctx_len=ctx_len, kv_heads=MLA_HEADS, group=1,
                             dk=2 * V7X_LANES, dv=MLA_V)
            z_col0 = q_lora + kv_lora + 2 * V7X_LANES
            xs = _mixout(att, e_w_o[i].astype(BF16), xs, g1, group_of,
                         conv=(p, z_col0, b_conv_w[i], b_conv_b[i]), t_lat=t_lat, seq=seq, ctx_len=ctx_len)
        else:
            p = _normproj(xs, norm1_g[l], sc1, sh1, c_w_qkv[i].astype(BF16), group_of, emit_h=False)
            q, k, v = _gqa_qkv(p, cos_c, sin_c, c_q_g[i], c_k_g[i])
            att = _attention(q, k, v, batch=batch, seq=seq, ctx_len=ctx_len, kv_heads=GQA_KV_HEADS,
                             group=GQA_HEADS // GQA_KV_HEADS, dk=GQA_HD, dv=GQA_HD)
            xs = _mixout(att, c_w_o[i].astype(BF16), xs, g1, group_of, t_lat=t_lat, seq=seq, ctx_len=ctx_len)
        qp, h2 = _normproj(xs, norm2_g[l], sc2, sh2, p_w_q[l].astype(BF16), group_of, emit_h=True)
        r1, p1, n0p0 = _route(qp, p_sub_keys[l].astype(BF16))
        xs = _peer(h2, u_all, vt_all, l, r1, p1, n0p0, xs, g2, group_of)
    return xs[:t_lat].reshape(batch, seq, d)
```

```python
import functools
import math

import jax
import jax.numpy as jnp
from jax import lax
from jax.experimental import pallas as pl
from jax.experimental.pallas import tpu as pltpu

F32 = jnp.float32
BF16 = jnp.bfloat16

EPS = 1e-6
ROPE_THETA = 10000.0
GRID_W = 64
MLA_HEADS = 8
MLA_NOPE = 128
MLA_ROPE = 64
MLA_V = 128
GQA_HEADS = 16
GQA_KV_HEADS = 4
GQA_HD = 128
PEER_HEADS = 8
N_KEYS = 128
PEER_TOPK = 16
N_MOD = 6

V7X_LANES = 128
V7X_SUBLANES = 8
V7X_VMEM_LIMIT_BYTES = 56 * 1024 * 1024

ROW_BLOCK = 512
COL_BLOCK = 1024
ATT_Q_ROWS = 2048
ATT_CHAIN_ROWS = 256
PEER_TB = 512
PEER_EB = 1024
PEER_ROWS = PEER_EB // N_KEYS
SQRT_HALF = 0.7071067811865476
NEG_INF = float("-inf")


def _cparams(sem):
    return pltpu.CompilerParams(dimension_semantics=sem, vmem_limit_bytes=V7X_VMEM_LIMIT_BYTES)


def _nt_dot(a, b):
    return lax.dot_general(a, b, (((1,), (1,)), ((), ())), preferred_element_type=F32)


def _ada_kernel(c_ref, w_ref, b_ref, o_ref):
    c = c_ref[...]
    s = c / (1.0 + jnp.exp(-c))
    o_ref[0] = jnp.dot(s.astype(BF16), w_ref[0].astype(BF16), preferred_element_type=F32) + b_ref[0]


def _ada(cvec, ada_w, ada_b):
    depth, d, n = ada_w.shape
    nb = 1024
    rows = cvec.shape[0]
    return pl.pallas_call(
        _ada_kernel,
        grid=(depth, n // nb),
        in_specs=[
            pl.BlockSpec((rows, d), lambda l, j: (0, 0)),
            pl.BlockSpec((1, d, nb), lambda l, j: (l, 0, j)),
            pl.BlockSpec((1, 1, nb), lambda l, j: (l, 0, j)),
        ],
        out_specs=pl.BlockSpec((1, rows, nb), lambda l, j: (l, 0, j)),
        out_shape=jax.ShapeDtypeStruct((depth, rows, n), F32),
        compiler_params=_cparams(("parallel", "parallel")),
        name="ada_mod",
    )(cvec, ada_w, ada_b.reshape(depth, 1, n))


def _normproj_kernel(x_ref, g_ref, sc_ref, sh_ref, w_ref, o_ref, *rest, emit_h):
    hs_ref = rest[-1]

    @pl.when(pl.program_id(1) == 0)
    def _():
        x = x_ref[...]
        y = x * lax.rsqrt(jnp.mean(x * x, axis=-1, keepdims=True) + EPS) * g_ref[...]
        h = (y * (1.0 + sc_ref[0]) + sh_ref[0]).astype(BF16)
        hs_ref[...] = h
        if emit_h:
            rest[0][...] = pltpu.bitcast(h, jnp.uint32)

    o_ref[...] = jnp.dot(hs_ref[...], w_ref[...], preferred_element_type=F32)


def _normproj(x, g, sc, sh, w, group_of, emit_h):
    t, d = x.shape
    n = w.shape[1]
    bm, bn = ROW_BLOCK, COL_BLOCK
    out_shape = [jax.ShapeDtypeStruct((t, n), F32)]
    out_specs = [pl.BlockSpec((bm, bn), lambda i, j: (i, j))]
    if emit_h:
        out_shape.append(jax.ShapeDtypeStruct((t // 2, d), jnp.uint32))
        out_specs.append(pl.BlockSpec((bm // 2, d), lambda i, j: (i, 0)))
    res = pl.pallas_call(
        functools.partial(_normproj_kernel, emit_h=emit_h),
        grid=(t // bm, n // bn),
        in_specs=[
            pl.BlockSpec((bm, d), lambda i, j: (i, 0)),
            pl.BlockSpec((1, d), lambda i, j: (0, 0)),
            pl.BlockSpec((1, 1, d), lambda i, j: (group_of(i), 0, 0)),
            pl.BlockSpec((1, 1, d), lambda i, j: (group_of(i), 0, 0)),
            pl.BlockSpec((d, bn), lambda i, j: (0, j)),
        ],
        out_specs=out_specs,
        out_shape=out_shape,
        scratch_shapes=[pltpu.VMEM((bm, d), BF16)],
        compiler_params=_cparams(("parallel", "arbitrary")),
        name="normproj_h" if emit_h else "normproj",
    )(x, g.reshape(1, d), sc, sh, w)
    return res if emit_h else res[0]


def _mla_qkv_kernel(p_ref, cos_ref, sin_ref, wuq_ref, wukv_ref, qlg_ref, kvlg_ref,
                    qgn_ref, qgr_ref, qgs_ref, kgn_ref, kgr_ref, kgs_ref,
                    q_ref, k_ref, v_ref, *, q_lora, kv_lora, scale):
    qk_dim = MLA_NOPE + MLA_ROPE
    lanes = V7X_LANES
    cq = p_ref[:, 0:q_lora]
    ckv = p_ref[:, q_lora:q_lora + kv_lora]
    kr = p_ref[:, q_lora + kv_lora:q_lora + kv_lora + lanes]
    krs = p_ref[:, q_lora + kv_lora + lanes:q_lora + kv_lora + 2 * lanes]
    cos = cos_ref[...]
    sin = sin_ref[...]

    cqn = cq * lax.rsqrt(jnp.mean(cq * cq, axis=-1, keepdims=True) + EPS) * qlg_ref[...]
    qraw = jnp.dot(cqn.astype(BF16), wuq_ref[...], preferred_element_type=F32)
    ckvn = ckv * lax.rsqrt(jnp.mean(ckv * ckv, axis=-1, keepdims=True) + EPS) * kvlg_ref[...]
    kvraw = jnp.dot(ckvn.astype(BF16), wukv_ref[...], preferred_element_type=F32)

    kr_rot = kr * kgr_ref[...] * cos + krs * kgs_ref[...] * sin
    kr_ssq = jnp.sum(kr * kr, axis=-1, keepdims=True)
    for h in range(MLA_HEADS):
        nope = qraw[:, h * 3 * lanes:h * 3 * lanes + lanes]
        rope = qraw[:, h * 3 * lanes + lanes:h * 3 * lanes + 2 * lanes]
        rope_sw = qraw[:, h * 3 * lanes + 2 * lanes:h * 3 * lanes + 3 * lanes]
        ssq = jnp.sum(nope * nope, axis=-1, keepdims=True) + jnp.sum(rope * rope, axis=-1, keepdims=True)
        r = lax.rsqrt(ssq * (1.0 / qk_dim) + EPS) * scale
        q_ref[:, h * 2 * lanes:h * 2 * lanes + lanes] = (nope * r * qgn_ref[...]).astype(BF16)
        q_ref[:, h * 2 * lanes + lanes:(h + 1) * 2 * lanes] = (
            (rope * qgr_ref[...] * cos + rope_sw * qgs_ref[...] * sin) * r).astype(BF16)

        k_nope = kvraw[:, h * 2 * lanes:h * 2 * lanes + lanes]
        v = kvraw[:, h * 2 * lanes + lanes:(h + 1) * 2 * lanes]
        kssq = jnp.sum(k_nope * k_nope, axis=-1, keepdims=True) + kr_ssq
        rk = lax.rsqrt(kssq * (1.0 / qk_dim) + EPS)
        k_ref[:, h * 2 * lanes:h * 2 * lanes + lanes] = (k_nope * rk * kgn_ref[...]).astype(BF16)
        k_ref[:, h * 2 * lanes + lanes:(h + 1) * 2 * lanes] = (kr_rot * rk).astype(BF16)
        v_ref[:, h * 2 * lanes:h * 2 * lanes + lanes] = v.astype(BF16)
        v_ref[:, h * 2 * lanes + lanes:(h + 1) * 2 * lanes] = jnp.ones((v.shape[0], lanes), BF16)


def _mla_qkv(p, cos, sin, wuq, wukv, gains, q_lora, kv_lora):
    t = p.shape[0]
    bm = ROW_BLOCK
    lanes = V7X_LANES
    head_cols = q_lora + kv_lora + 2 * lanes
    full = lambda a: pl.BlockSpec(a.shape, lambda i: (0,) * a.ndim)
    return pl.pallas_call(
        functools.partial(_mla_qkv_kernel, q_lora=q_lora, kv_lora=kv_lora,
                          scale=float(MLA_NOPE + MLA_ROPE) ** -0.5),
        grid=(t // bm,),
        in_specs=[
            pl.BlockSpec((bm, head_cols), lambda i: (i, 0)),
            pl.BlockSpec((bm, lanes), lambda i: (i, 0)),
            pl.BlockSpec((bm, lanes), lambda i: (i, 0)),
            full(wuq), full(wukv)] + [full(g) for g in gains],
        out_specs=[
            pl.BlockSpec((bm, MLA_HEADS * 2 * lanes), lambda i: (i, 0)),
            pl.BlockSpec((bm, MLA_HEADS * 2 * lanes), lambda i: (i, 0)),
            pl.BlockSpec((bm, MLA_HEADS * 2 * lanes), lambda i: (i, 0)),
        ],
        out_shape=[
            jax.ShapeDtypeStruct((t, MLA_HEADS * 2 * lanes), BF16),
            jax.ShapeDtypeStruct((t, MLA_HEADS * 2 * lanes), BF16),
            jax.ShapeDtypeStruct((t, MLA_HEADS * 2 * lanes), BF16),
        ],
        compiler_params=_cparams(("parallel",)),
        name="mla_qkv",
    )(p, cos, sin, wuq, wukv, *gains)


def _gqa_qkv_kernel(p_ref, cos_ref, sin_ref, qg_ref, kg_ref, q_ref, k_ref, v_ref, *, scale):
    hd = GQA_HD
    cos = cos_ref[...]
    sin = sin_ref[...]

    def head(x, g):
        y = x * lax.rsqrt(jnp.mean(x * x, axis=-1, keepdims=True) + EPS) * g
        return y * cos + pltpu.roll(y, hd // 2, 1) * sin

    for h in range(GQA_HEADS):
        q_ref[:, h * hd:(h + 1) * hd] = (head(p_ref[:, h * hd:(h + 1) * hd], qg_ref[...]) * scale).astype(BF16)
    k0 = GQA_HEADS * hd
    v0 = k0 + GQA_KV_HEADS * hd
    for h in range(GQA_KV_HEADS):
        k_ref[:, h * hd:(h + 1) * hd] = head(p_ref[:, k0 + h * hd:k0 + (h + 1) * hd], kg_ref[...]).astype(BF16)
    for h in range(GQA_KV_HEADS):
        v_ref[:, h * 2 * hd:h * 2 * hd + hd] = p_ref[:, v0 + h * hd:v0 + (h + 1) * hd].astype(BF16)
        v_ref[:, h * 2 * hd + hd:(h + 1) * 2 * hd] = jnp.ones((v_ref.shape[0], hd), BF16)


def _gqa_qkv(p, cos, sin, qg, kg):
    t, n = p.shape
    bm = ROW_BLOCK
    hd = GQA_HD
    return pl.pallas_call(
        functools.partial(_gqa_qkv_kernel, scale=float(hd) ** -0.5),
        grid=(t // bm,),
        in_specs=[
            pl.BlockSpec((bm, n), lambda i: (i, 0)),
            pl.BlockSpec((bm, hd), lambda i: (i, 0)),
            pl.BlockSpec((bm, hd), lambda i: (i, 0)),
            pl.BlockSpec((1, hd), lambda i: (0, 0)),
            pl.BlockSpec((1, hd), lambda i: (0, 0)),
        ],
        out_specs=[
            pl.BlockSpec((bm, GQA_HEADS * hd), lambda i: (i, 0)),
            pl.BlockSpec((bm, GQA_KV_HEADS * hd), lambda i: (i, 0)),
            pl.BlockSpec((bm, GQA_KV_HEADS * 2 * hd), lambda i: (i, 0)),
        ],
        out_shape=[
            jax.ShapeDtypeStruct((t, GQA_HEADS * hd), BF16),
            jax.ShapeDtypeStruct((t, GQA_KV_HEADS * hd), BF16),
            jax.ShapeDtypeStruct((t, GQA_KV_HEADS * 2 * hd), BF16),
        ],
        compiler_params=_cparams(("parallel",)),
        name="gqa_qkv",
    )(p, cos, sin, qg.reshape(1, hd), kg.reshape(1, hd))


def _attn_kernel(q_ref, kc_ref, vc_ref, *rest, group, dk, dv, with_latent):
    if with_latent:
        kl_ref, vl_ref, o_ref = rest
    else:
        (o_ref,) = rest
    tq = q_ref.shape[0]
    chain_rows = min(ATT_CHAIN_ROWS, group * tq)
    per_head = tq // chain_rows if chain_rows < tq else 0
    for c in range(group * tq // chain_rows):
        if per_head:
            g, part = divmod(c, per_head)
            rows = slice(part * chain_rows, (part + 1) * chain_rows)
            q = q_ref[rows, g * dk:(g + 1) * dk]
            dst = [(rows, g, slice(0, chain_rows))]
        else:
            heads = range(c * chain_rows // tq, (c + 1) * chain_rows // tq)
            q = jnp.concatenate([q_ref[:, g * dk:(g + 1) * dk] for g in heads], axis=0)
            dst = [(slice(0, tq), g, slice(n * tq, (n + 1) * tq)) for n, g in enumerate(heads)]
        sc = _nt_dot(q, kc_ref[...])
        m = jnp.max(sc, axis=-1, keepdims=True)
        if with_latent:
            sl = _nt_dot(q, kl_ref[...])
            m = jnp.maximum(m, jnp.max(sl, axis=-1, keepdims=True))
        o = jnp.dot(jnp.exp((sc - m).astype(BF16)), vc_ref[...], preferred_element_type=F32)
        if with_latent:
            o = o + jnp.dot(jnp.exp((sl - m).astype(BF16)), vl_ref[...], preferred_element_type=F32)
        o = (o[:, :dv] / o[:, dv:dv + 1]).astype(BF16)
        for rows, g, src in dst:
            o_ref[rows, g * dv:(g + 1) * dv] = o[src, :]


def _attention(q, k, v, *, batch, seq, ctx_len, kv_heads, group, dk, dv):
    t_lat = batch * seq
    tq = ATT_Q_ROWS // group
    nq = seq // tq
    ctx_blk0 = t_lat // ctx_len
    common = dict(group=group, dk=dk, dv=dv)
    lat = pl.pallas_call(
        functools.partial(_attn_kernel, with_latent=True, **common),
        grid=(batch, kv_heads, nq),
        in_specs=[
            pl.BlockSpec((tq, group * dk), lambda b, h, j: (b * nq + j, h)),
            pl.BlockSpec((ctx_len, dk), lambda b, h, j: (ctx_blk0 + b, h)),
            pl.BlockSpec((ctx_len, 2 * dv), lambda b, h, j: (ctx_blk0 + b, h)),
            pl.BlockSpec((seq, dk), lambda b, h, j: (b, h)),
            pl.BlockSpec((seq, 2 * dv), lambda b, h, j: (b, h)),
        ],
        out_specs=pl.BlockSpec((tq, group * dv), lambda b, h, j: (b * nq + j, h)),
        out_shape=jax.ShapeDtypeStruct((t_lat, kv_heads * group * dv), BF16),
        compiler_params=_cparams(("parallel", "parallel", "arbitrary")),
        name="attn_latent",
    )(q, k, v, k, v)
    ctx = pl.pallas_call(
        functools.partial(_attn_kernel, with_latent=False, **common),
        grid=(batch, kv_heads),
        in_specs=[
            pl.BlockSpec((ctx_len, group * dk), lambda b, h: (ctx_blk0 + b, h)),
            pl.BlockSpec((ctx_len, dk), lambda b, h: (ctx_blk0 + b, h)),
            pl.BlockSpec((ctx_len, 2 * dv), lambda b, h: (ctx_blk0 + b, h)),
        ],
        out_specs=pl.BlockSpec((ctx_len, group * dv), lambda b, h: (b, h)),
        out_shape=jax.ShapeDtypeStruct((batch * ctx_len, kv_heads * group * dv), BF16),
        compiler_params=_cparams(("parallel", "parallel")),
        name="attn_ctx",
    )(q, k, v)
    return jnp.concatenate([lat, ctx], axis=0)


def _mixout_kernel(*refs, with_conv, t_lat, seq, ctx_len, a_width):
    if with_conv:
        (a_ref, bg_ref, cg_ref, hz_ref, cgp_ref, hzp_ref, cgn_ref, hzn_ref, cw_ref, cb_ref,
         w_ref, x_ref, gate_ref, o_ref, cs_ref) = refs
    else:
        a_ref, w_ref, x_ref, gate_ref, o_ref = refs
    i = pl.program_id(0)

    if with_conv:
        @pl.when(pl.program_id(1) == 0)
        def _():
            bm = cg_ref.shape[0]
            u = cg_ref[...] * hz_ref[...]
            u_before = cgp_ref[V7X_SUBLANES - 1:V7X_SUBLANES, :] * hzp_ref[V7X_SUBLANES - 1:V7X_SUBLANES, :]
            u_after = cgn_ref[0:1, :] * hzn_ref[0:1, :]
            local = lax.broadcasted_iota(jnp.int32, (bm, 1), 0)
            row = local + i * bm
            in_lat = row < t_lat
            seg_pos = jnp.where(in_lat, jnp.bitwise_and(row, seq - 1), jnp.bitwise_and(row - t_lat, ctx_len - 1))
            seg_len = jnp.where(in_lat, seq, ctx_len)
            up = jnp.where(local == 0, u_before, pltpu.roll(u, 1, 0))
            up = jnp.where(seg_pos == 0, 0.0, up)
            un = jnp.where(local == bm - 1, u_after, pltpu.roll(u, bm - 1, 0))
            un = jnp.where(seg_pos == seg_len - 1, 0.0, un)
            y = up * cw_ref[0:1, :] + u * cw_ref[1:2, :] + un * cw_ref[2:3, :] + cb_ref[...]
            cs_ref[...] = (bg_ref[...] * y).astype(BF16)

        acc = jnp.dot(a_ref[...], w_ref[0:a_width, :], preferred_element_type=F32)
        acc = acc + jnp.dot(cs_ref[...], w_ref[a_width:, :], preferred_element_type=F32)
    else:
        acc = jnp.dot(a_ref[...], w_ref[...], preferred_element_type=F32)
    o_ref[...] = x_ref[...] + gate_ref[0] * acc


def _mixout(a, w, x, gate, group_of, conv=None, *, t_lat, seq, ctx_len):
    t, d = x.shape
    bm, bn = ROW_BLOCK, COL_BLOCK
    a_width = a.shape[1]
    kw = dict(t_lat=t_lat, seq=seq, ctx_len=ctx_len, a_width=a_width)
    in_specs = [pl.BlockSpec((bm, a_width), lambda i, j: (i, 0))]
    args = [a]
    scratch = []
    if conv is not None:
        p, z_col0, cw, cb = conv
        cwid = cw.shape[1]
        assert z_col0 % cwid == 0
        zb = z_col0 // cwid
        sub = V7X_SUBLANES
        last_halo = t // sub - 1
        prev_idx = lambda i: jnp.maximum(i * (bm // sub) - 1, 0)
        next_idx = lambda i: jnp.minimum((i + 1) * (bm // sub), last_halo)
        in_specs += [
            pl.BlockSpec((bm, cwid), lambda i, j: (i, zb)),
            pl.BlockSpec((bm, cwid), lambda i, j: (i, zb + 1)),
            pl.BlockSpec((bm, cwid), lambda i, j: (i, zb + 2)),
            pl.BlockSpec((sub, cwid), lambda i, j: (prev_idx(i), zb + 1)),
            pl.BlockSpec((sub, cwid), lambda i, j: (prev_idx(i), zb + 2)),
            pl.BlockSpec((sub, cwid), lambda i, j: (next_idx(i), zb + 1)),
            pl.BlockSpec((sub, cwid), lambda i, j: (next_idx(i), zb + 2)),
            pl.BlockSpec(cw.shape, lambda i, j: (0, 0)),
            pl.BlockSpec((1, cwid), lambda i, j: (0, 0)),
        ]
        args += [p] * 7 + [cw, cb.reshape(1, cwid)]
        scratch = [pltpu.VMEM((bm, cwid), BF16)]
    in_specs += [
        pl.BlockSpec((w.shape[0], bn), lambda i, j: (0, j)),
        pl.BlockSpec((bm, bn), lambda i, j: (i, j)),
        pl.BlockSpec((1, 1, bn), lambda i, j: (group_of(i), 0, j)),
    ]
    args += [w, x, gate]
    return pl.pallas_call(
        functools.partial(_mixout_kernel, with_conv=conv is not None, **kw),
        grid=(t // bm, d // bn),
        in_specs=in_specs,
        out_specs=pl.BlockSpec((bm, bn), lambda i, j: (i, j)),
        out_shape=jax.ShapeDtypeStruct((t, d), F32),
        scratch_shapes=scratch,
        compiler_params=_cparams(("parallel", "arbitrary")),
        name="mixout_conv" if conv is not None else "mixout",
    )(*args)


_CAND_PAIRS = [(a, b) for a in range(PEER_TOPK) for b in range(PEER_TOPK // (a + 1))]
_CAND_ROWS = -(-len(_CAND_PAIRS) // V7X_SUBLANES) * V7X_SUBLANES
BF16_ROWS = 2 * V7X_SUBLANES


def _bf16_pair_words(v):
    bits = pltpu.bitcast(v.astype(BF16).astype(F32), jnp.uint32)
    return bits | (bits >> 16)


def _route_kernel(q_ref, sk_ref, r1_ref, p1_ref, np_ref, cand_ref):
    nk = N_KEYS
    k_top = PEER_TOPK
    lanes = V7X_LANES
    tb = q_ref.shape[0]
    cand_ref[len(_CAND_PAIRS):, :] = jnp.full((_CAND_ROWS - len(_CAND_PAIRS), lanes), NEG_INF, F32)
    for c in range(tb // lanes):
        cs = slice(c * lanes, (c + 1) * lanes)
        s0, s1 = (_nt_dot(sk_ref[0, half], q_ref[cs, half * nk:(half + 1) * nk].astype(BF16))
                  for half in range(2))

        cur = s0
        tops0 = []
        for k in range(k_top):
            m = jnp.max(cur, axis=0, keepdims=True)
            tops0.append(m)
            cur = jnp.where(cur == m, NEG_INF, cur)
        cur = s1
        rank = jnp.full((nk, lanes), float(k_top), F32)
        tops1 = []
        for k in range(k_top):
            m = jnp.max(cur, axis=0, keepdims=True)
            tops1.append(m)
            hit = cur == m
            rank = jnp.where(hit, float(k), rank)
            cur = jnp.where(hit, NEG_INF, cur)

        for r, (a, b) in enumerate(_CAND_PAIRS):
            cand_ref[r:r + 1, :] = tops0[a] + tops1[b]
        cand = cand_ref[...]
        top = tops0[0] + tops1[0]
        z = jnp.zeros((1, lanes), F32)
        for k in range(k_top):
            tau = jnp.max(cand, axis=0, keepdims=True)
            z = z + jnp.exp(tau - top)
            cand = jnp.where(cand == tau, NEG_INF, cand)

        n0 = jnp.zeros((nk, lanes), F32)
        for b in range(k_top):
            n0 = n0 + jnp.where(s0 + tops1[b] >= tau, 1.0, 0.0)
        n0 = _bf16_pair_words(n0)
        p0 = _bf16_pair_words(jnp.exp(s0 - tops0[0]))
        for g in range(nk // PEER_ROWS):
            np_ref[0, g, 0:PEER_ROWS, cs] = n0[g * PEER_ROWS:(g + 1) * PEER_ROWS, :]
            np_ref[0, g, PEER_ROWS:, cs] = p0[g * PEER_ROWS:(g + 1) * PEER_ROWS, :]
        rank_b = rank.astype(BF16)
        p1_b = (jnp.exp(s1 - tops1[0]) / z).astype(BF16)
        for g in range(nk // BF16_ROWS):
            r1_ref[0, g, :, cs] = pltpu.bitcast(rank_b[g * BF16_ROWS:(g + 1) * BF16_ROWS, :], jnp.int32)
            p1_ref[0, g, :, cs] = pltpu.bitcast(p1_b[g * BF16_ROWS:(g + 1) * BF16_ROWS, :], jnp.int32)


def _route(qp, sub_keys):
    t = qp.shape[0]
    tb = PEER_TB
    nk = N_KEYS
    heads = PEER_HEADS
    packed = jax.ShapeDtypeStruct((heads, nk // BF16_ROWS, V7X_SUBLANES, t), jnp.int32)
    packed_spec = pl.BlockSpec((1, nk // BF16_ROWS, V7X_SUBLANES, tb), lambda i, h: (h, 0, 0, i))
    plain = jax.ShapeDtypeStruct((heads, nk // PEER_ROWS, 2 * PEER_ROWS, t), jnp.uint32)
    plain_spec = pl.BlockSpec((1, nk // PEER_ROWS, 2 * PEER_ROWS, tb), lambda i, h: (h, 0, 0, i))
    return pl.pallas_call(
        _route_kernel,
        grid=(t // tb, heads),
        in_specs=[
            pl.BlockSpec((tb, 2 * nk), lambda i, h: (i, h)),
            pl.BlockSpec((1, 2, nk, sub_keys.shape[-1]), lambda i, h: (h, 0, 0, 0)),
        ],
        out_specs=[packed_spec, packed_spec, plain_spec],
        out_shape=[packed, packed, plain],
        scratch_shapes=[pltpu.VMEM((_CAND_ROWS, V7X_LANES), F32)],
        compiler_params=_cparams(("parallel", "arbitrary")),
        name="peer_route",
    )(qp, sub_keys)


def _peer_kernel(h_ref, u_ref, vt_ref, r1_ref, p1_ref, np_ref, x_ref, gate_ref,
                 o_ref, acc_ref, at0_ref, at1_ref, hs_ref, *, n_blocks):
    j = pl.program_id(1)
    nk = N_KEYS
    lanes = V7X_LANES
    eb, tb = at0_ref.shape
    at_refs = (at0_ref, at1_ref)
    jm = jnp.maximum(j - 1, 0)

    def key_row(hd, row, cs):
        words = jnp.broadcast_to(np_ref[hd, jm, row:row + 1, cs], (V7X_SUBLANES, lanes))
        return pltpu.bitcast(words, BF16)[None]

    def gate_rows(il, prev):
        for c in range(tb // lanes):
            cs = slice(c * lanes, (c + 1) * lanes)
            w = None
            for hd in range(PEER_HEADS):
                r1 = pltpu.bitcast(r1_ref[hd, :, :, cs], BF16)
                p1 = pltpu.bitcast(p1_ref[hd, :, :, cs], BF16)
                term = jnp.where(r1 < key_row(hd, il, cs), p1 * key_row(hd, PEER_ROWS + il, cs),
                                 jnp.zeros((), BF16))
                w = term if w is None else w + term
            a = at_refs[prev][il * nk:(il + 1) * nk, cs]
            gelu = (0.5 * a * (1.0 + lax.erf(a * SQRT_HALF))).astype(BF16)
            for g in range(nk // BF16_ROWS):
                r0 = il * nk + g * BF16_ROWS
                hs_ref[r0:r0 + BF16_ROWS, cs] = w[g] * gelu[g * BF16_ROWS:(g + 1) * BF16_ROWS, :]

    def first_matmul(cur):
        at_refs[cur][...] = _nt_dot(pltpu.bitcast(u_ref[0], BF16), pltpu.bitcast(h_ref[...], BF16))

    def mix(prev):
        for il in range(PEER_ROWS):
            gate_rows(il, prev)
        acc_ref[...] += jnp.dot(pltpu.bitcast(vt_ref[0, 0], BF16), hs_ref[...], preferred_element_type=F32)

    @pl.when(j == 0)
    def _():
        acc_ref[...] = jnp.zeros_like(acc_ref)
        first_matmul(0)

    for parity in range(2):
        @pl.when(jnp.logical_and(jnp.logical_and(j > 0, j < n_blocks), j % 2 == parity))
        def _(parity=parity):
            first_matmul(parity)
            mix(1 - parity)

    @pl.when(j == n_blocks)
    def _():
        mix((n_blocks - 1) % 2)
        o_ref[...] = x_ref[...] + gate_ref[0] * acc_ref[...].T


def _peer(h, u, vt, layer, r1, p1, n0p0, x, gate, group_of):
    t, d = x.shape
    tb, eb = PEER_TB, PEER_EB
    n_blocks = vt.shape[1]
    nk = N_KEYS
    heads = PEER_HEADS
    packed_spec = pl.BlockSpec((heads, nk // BF16_ROWS, V7X_SUBLANES, tb), lambda i, j: (0, 0, 0, i))
    plain_spec = pl.BlockSpec((heads, nk // PEER_ROWS, 2 * PEER_ROWS, tb), lambda i, j: (0, 0, 0, i))
    once = dict(pipeline_mode=pl.Buffered(1))
    return pl.pallas_call(
        functools.partial(_peer_kernel, n_blocks=n_blocks),
        grid=(t // tb, n_blocks + 1),
        in_specs=[
            pl.BlockSpec((tb // 2, d), lambda i, j: (i, 0), **once),
            pl.BlockSpec((1, eb // 2, d), lambda i, j: (layer, jnp.minimum(j, n_blocks - 1), 0)),
            pl.BlockSpec((1, 1, d // 2, eb), lambda i, j: (layer, jnp.maximum(j - 1, 0), 0, 0)),
            pl.BlockSpec(packed_spec.block_shape, packed_spec.index_map, **once),
            pl.BlockSpec(packed_spec.block_shape, packed_spec.index_map, **once),
            pl.BlockSpec(plain_spec.block_shape, plain_spec.index_map, **once),
            pl.BlockSpec((tb, d), lambda i, j: (i, 0), **once),
            pl.BlockSpec((1, 1, d), lambda i, j: (group_of(i), 0, 0)),
        ],
        out_specs=pl.BlockSpec((tb, d), lambda i, j: (i, 0)),
        out_shape=jax.ShapeDtypeStruct((t, d), F32),
        scratch_shapes=[
            pltpu.VMEM((d, tb), F32),
            pltpu.VMEM((eb, tb), F32),
            pltpu.VMEM((eb, tb), F32),
            pltpu.VMEM((eb, tb), BF16),
        ],
        compiler_params=_cparams(("parallel", "arbitrary")),
        name="peer_mix",
    )(h, u, vt, r1, p1, n0p0, x, gate)


def _pack_table_kernel(x_ref, o_ref, *, transpose):
    x = x_ref[0]
    if transpose:
        o_ref[0, 0] = pltpu.bitcast(x.T.astype(BF16), jnp.uint32)
    else:
        o_ref[0] = pltpu.bitcast(x.astype(BF16), jnp.uint32)


def _pack_table(tab, transpose):
    layers, ne, d = tab.shape
    eb = PEER_EB
    if transpose:
        out_shape = jax.ShapeDtypeStruct((layers, ne // eb, d // 2, eb), jnp.uint32)
        out_spec = pl.BlockSpec((1, 1, d // 2, eb), lambda l, j: (l, j, 0, 0))
    else:
        out_shape = jax.ShapeDtypeStruct((layers, ne // 2, d), jnp.uint32)
        out_spec = pl.BlockSpec((1, eb // 2, d), lambda l, j: (l, j, 0))
    return pl.pallas_call(
        functools.partial(_pack_table_kernel, transpose=transpose),
        grid=(layers, ne // eb),
        in_specs=[pl.BlockSpec((1, eb, d), lambda l, j: (l, j, 0))],
        out_specs=out_spec,
        out_shape=out_shape,
        compiler_params=_cparams(("parallel", "parallel")),
        name="pack_table_t" if transpose else "pack_table",
    )(tab)


def _rope_tables(seq, rope_dim, batch, ctx_rows):
    rows = seq // GRID_W
    row = jnp.repeat(jnp.arange(rows, dtype=F32), GRID_W)
    col = jnp.tile(jnp.arange(GRID_W, dtype=F32), rows)
    quarter = rope_dim // 4
    freqs = ROPE_THETA ** (-jnp.arange(quarter, dtype=F32) / quarter)
    ang = jnp.concatenate([row[:, None] * freqs, col[:, None] * freqs], axis=-1)
    cos, sin = jnp.cos(ang), jnp.sin(ang)
    pad = V7X_LANES - rope_dim
    cos_l = jnp.pad(jnp.concatenate([cos, cos], axis=-1), ((0, 0), (0, pad)))
    sin_l = jnp.pad(jnp.concatenate([-sin, sin], axis=-1), ((0, 0), (0, pad)))
    cos_c = jnp.pad(jnp.ones((ctx_rows, rope_dim), F32), ((0, 0), (0, pad)))
    sin_c = jnp.zeros((ctx_rows, V7X_LANES), F32)
    return (jnp.concatenate([jnp.tile(cos_l, (batch, 1)), cos_c], axis=0),
            jnp.concatenate([jnp.tile(sin_l, (batch, 1)), sin_c], axis=0))


def _swap_halves(a):
    half = a.shape[-1] // 2
    return jnp.concatenate([a[..., half:], a[..., :half]], axis=-1)


def _pad_lanes(a):
    return jnp.pad(a, [(0, 0)] * (a.ndim - 1) + [(0, V7X_LANES - a.shape[-1])])


def _mla_weights(w_in, w_uq, q_g, k_g, q_lora, kv_lora):
    d = w_in.shape[0]
    c_kr = q_lora + kv_lora
    w_kr = w_in[:, c_kr:c_kr + MLA_ROPE]
    w_in_ext = jnp.concatenate([
        w_in[:, :c_kr], _pad_lanes(w_kr), _pad_lanes(_swap_halves(w_kr)), w_in[:, c_kr + MLA_ROPE:]], axis=1)
    wq = w_uq.reshape(q_lora, MLA_HEADS, MLA_NOPE + MLA_ROPE)
    wq_rope = wq[:, :, MLA_NOPE:]
    wq_ext = jnp.concatenate([wq[:, :, :MLA_NOPE], _pad_lanes(wq_rope), _pad_lanes(_swap_halves(wq_rope))], axis=-1)
    wq_ext = wq_ext.reshape(q_lora, MLA_HEADS * 3 * V7X_LANES)

    def gains(g):
        g_r = g[MLA_NOPE:]
        return [g[:MLA_NOPE].reshape(1, -1), _pad_lanes(g_r).reshape(1, -1), _pad_lanes(_swap_halves(g_r)).reshape(1, -1)]

    return w_in_ext.astype(BF16), wq_ext.astype(BF16), gains(q_g) + gains(k_g)


def kernel(x, c, ctx, c_ctx, ada_w, ada_b, norm1_g, norm2_g, a_w_in, a_q_lora_g, a_kv_lora_g, a_w_uq, a_w_ukv, a_q_g, a_k_g, b_conv_w, b_conv_b, e_w_o, c_w_qkv, c_q_g, c_k_g, c_w_o, p_w_q, p_sub_keys, p_u, p_v):
    batch, seq, d = x.shape
    ctx_len = ctx.shape[1]
    depth = ada_w.shape[0]
    q_lora = a_q_lora_g.shape[1]
    kv_lora = a_kv_lora_g.shape[1]
    conv_width = b_conv_w.shape[2]
    t_lat = batch * seq
    t_ctx = batch * ctx_len
    t = t_lat + t_ctx
    bm = ROW_BLOCK
    assert seq % bm == 0 and t_ctx % bm == 0 and t % PEER_TB == 0 and seq % ATT_Q_ROWS == 0
    assert seq & (seq - 1) == 0 and ctx_len & (ctx_len - 1) == 0 and seq % GRID_W == 0
    assert batch + 1 <= V7X_SUBLANES

    blocks_per_batch = seq // bm
    group_of = lambda i: jnp.minimum(i // blocks_per_batch, batch)

    cvec = jnp.zeros((V7X_SUBLANES, d), F32).at[:batch].set(c).at[batch].set(c_ctx)
    mods = _ada(cvec, ada_w, ada_b)
    mods = mods.reshape(depth, V7X_SUBLANES, N_MOD, 1, d).transpose(0, 2, 1, 3, 4)

    cos_a, sin_a = _rope_tables(seq, MLA_ROPE, batch, t_ctx)
    cos_c, sin_c = _rope_tables(seq, GQA_HD, batch, t_ctx)

    u_all = _pack_table(p_u, transpose=False)
    vt_all = _pack_table(p_v, transpose=True)
    xs = jnp.concatenate([x.reshape(t_lat, d), ctx.reshape(t_ctx, d)], axis=0)
    for l in range(depth):
        sh1, sc1, g1, sh2, sc2, g2 = (mods[l, k] for k in range(N_MOD))
        i = l // 2
        if l % 2 == 0:
            w_in_ext, wq_ext, gains = _mla_weights(a_w_in[i], a_w_uq[i], a_q_g[i], a_k_g[i], q_lora, kv_lora)
            p = _normproj(xs, norm1_g[l], sc1, sh1, w_in_ext, group_of, emit_h=False)
            q, k, v = _mla_qkv(p, cos_a, sin_a, wq_ext, a_w_ukv[i].astype(BF16),
                               [a_q_lora_g[i].reshape(1, -1), a_kv_lora_g[i].reshape(1, -1)] + gains,
                               q_lora, kv_lora)
            att = _attention(q, k, v, batch=batch, seq=seq, ctx_len=ctx_len, kv_heads=MLA_HEADS, group=1,
                             dk=2 * V7X_LANES, dv=MLA_V)
            z_col0 = q_lora + kv_lora + 2 * V7X_LANES
            xs = _mixout(att, e_w_o[i].astype(BF16), xs, g1, group_of,
                         conv=(p, z_col0, b_conv_w[i], b_conv_b[i]), t_lat=t_lat, seq=seq, ctx_len=ctx_len)
        else:
            p = _normproj(xs, norm1_g[l], sc1, sh1, c_w_qkv[i].astype(BF16), group_of, emit_h=False)
            q, k, v = _gqa_qkv(p, cos_c, sin_c, c_q_g[i], c_k_g[i])
            att = _attention(q, k, v, batch=batch, seq=seq, ctx_len=ctx_len, kv_heads=GQA_KV_HEADS,
                             group=GQA_HEADS // GQA_KV_HEADS, dk=GQA_HD, dv=GQA_HD)
            xs = _mixout(att, c_w_o[i].astype(BF16), xs, g1, group_of, t_lat=t_lat, seq=seq, ctx_len=ctx_len)
        qp, h2 = _normproj(xs, norm2_g[l], sc2, sh2, p_w_q[l].astype(BF16), group_of, emit_h=True)
        r1, p1, n0p0 = _route(qp, p_sub_keys[l].astype(BF16))
        xs = _peer(h2, u_all, vt_all, l, r1, p1, n0p0, xs, g2, group_of)
    return xs[:t_lat].reshape(batch, seq, d)
```

```python
import functools
import math

import jax
import jax.numpy as jnp
from jax import lax
from jax.experimental import pallas as pl
from jax.experimental.pallas import tpu as pltpu

F32 = jnp.float32
BF16 = jnp.bfloat16

EPS = 1e-6
ROPE_THETA = 10000.0
GRID_W = 64
MLA_HEADS = 8
MLA_NOPE = 128
MLA_ROPE = 64
MLA_V = 128
GQA_HEADS = 16
GQA_KV_HEADS = 4
GQA_HD = 128
PEER_HEADS = 8
N_KEYS = 128
PEER_TOPK = 16
N_MOD = 6

V7X_LANES = 128
V7X_SUBLANES = 8
V7X_VMEM_LIMIT_BYTES = 56 * 1024 * 1024

ROW_BLOCK = 512
COL_BLOCK = 1024
ATT_Q_ROWS = 2048
ATT_CHAIN_ROWS = 256
PEER_TB = 512
PEER_EB = 1024
PEER_ROWS = PEER_EB // N_KEYS
SQRT_HALF = 0.7071067811865476
NEG_INF = float("-inf")


def _cparams(sem):
    return pltpu.CompilerParams(dimension_semantics=sem, vmem_limit_bytes=V7X_VMEM_LIMIT_BYTES)


def _nt_dot(a, b):
    return lax.dot_general(a, b, (((1,), (1,)), ((), ())), preferred_element_type=F32)


def _ada_kernel(c_ref, w_ref, b_ref, o_ref):
    c = c_ref[...]
    s = c / (1.0 + jnp.exp(-c))
    o_ref[0] = jnp.dot(s.astype(BF16), w_ref[0].astype(BF16), preferred_element_type=F32) + b_ref[0]


def _ada(cvec, ada_w, ada_b):
    depth, d, n = ada_w.shape
    nb = 1024
    rows = cvec.shape[0]
    return pl.pallas_call(
        _ada_kernel,
        grid=(depth, n // nb),
        in_specs=[
            pl.BlockSpec((rows, d), lambda l, j: (0, 0)),
            pl.BlockSpec((1, d, nb), lambda l, j: (l, 0, j)),
            pl.BlockSpec((1, 1, nb), lambda l, j: (l, 0, j)),
        ],
        out_specs=pl.BlockSpec((1, rows, nb), lambda l, j: (l, 0, j)),
        out_shape=jax.ShapeDtypeStruct((depth, rows, n), F32),
        compiler_params=_cparams(("parallel", "parallel")),
        name="ada_mod",
    )(cvec, ada_w, ada_b.reshape(depth, 1, n))


def _normproj_kernel(x_ref, g_ref, sc_ref, sh_ref, w_ref, o_ref, *rest, emit_h):
    hs_ref = rest[-1]

    @pl.when(pl.program_id(1) == 0)
    def _():
        x = x_ref[...]
        y = x * lax.rsqrt(jnp.mean(x * x, axis=-1, keepdims=True) + EPS) * g_ref[...]
        h = (y * (1.0 + sc_ref[0]) + sh_ref[0]).astype(BF16)
        hs_ref[...] = h
        if emit_h:
            rest[0][...] = pltpu.bitcast(h, jnp.uint32)

    o_ref[...] = jnp.dot(hs_ref[...], w_ref[...], preferred_element_type=F32)


def _normproj(x, g, sc, sh, w, group_of, emit_h):
    t, d = x.shape
    n = w.shape[1]
    bm, bn = ROW_BLOCK, COL_BLOCK
    out_shape = [jax.ShapeDtypeStruct((t, n), F32)]
    out_specs = [pl.BlockSpec((bm, bn), lambda i, j: (i, j))]
    if emit_h:
        out_shape.append(jax.ShapeDtypeStruct((t // 2, d), jnp.uint32))
        out_specs.append(pl.BlockSpec((bm // 2, d), lambda i, j: (i, 0)))
    res = pl.pallas_call(
        functools.partial(_normproj_kernel, emit_h=emit_h),
        grid=(t // bm, n // bn),
        in_specs=[
            pl.BlockSpec((bm, d), lambda i, j: (i, 0)),
            pl.BlockSpec((1, d), lambda i, j: (0, 0)),
            pl.BlockSpec((1, 1, d), lambda i, j: (group_of(i), 0, 0)),
            pl.BlockSpec((1, 1, d), lambda i, j: (group_of(i), 0, 0)),
            pl.BlockSpec((d, bn), lambda i, j: (0, j)),
        ],
        out_specs=out_specs,
        out_shape=out_shape,
        scratch_shapes=[pltpu.VMEM((bm, d), BF16)],
        compiler_params=_cparams(("parallel", "arbitrary")),
        name="normproj_h" if emit_h else "normproj",
    )(x, g.reshape(1, d), sc, sh, w)
    return res if emit_h else res[0]


def _mla_qkv_kernel(p_ref, cos_ref, sin_ref, wuq_ref, wukv_ref, qlg_ref, kvlg_ref,
                    qgn_ref, qgr_ref, qgs_ref, kgn_ref, kgr_ref, kgs_ref,
                    q_ref, k_ref, v_ref, *, q_lora, kv_lora, scale):
    qk_dim = MLA_NOPE + MLA_ROPE
    lanes = V7X_LANES
    cq = p_ref[:, 0:q_lora]
    ckv = p_ref[:, q_lora:q_lora + kv_lora]
    kr = p_ref[:, q_lora + kv_lora:q_lora + kv_lora + lanes]
    krs = p_ref[:, q_lora + kv_lora + lanes:q_lora + kv_lora + 2 * lanes]
    cos = cos_ref[...]
    sin = sin_ref[...]

    cqn = cq * lax.rsqrt(jnp.mean(cq * cq, axis=-1, keepdims=True) + EPS) * qlg_ref[...]
    qraw = jnp.dot(cqn.astype(BF16), wuq_ref[...], preferred_element_type=F32)
    ckvn = ckv * lax.rsqrt(jnp.mean(ckv * ckv, axis=-1, keepdims=True) + EPS) * kvlg_ref[...]
    kvraw = jnp.dot(ckvn.astype(BF16), wukv_ref[...], preferred_element_type=F32)

    kr_rot = kr * kgr_ref[...] * cos + krs * kgs_ref[...] * sin
    kr_ssq = jnp.sum(kr * kr, axis=-1, keepdims=True)
    for h in range(MLA_HEADS):
        nope = qraw[:, h * 3 * lanes:h * 3 * lanes + lanes]
        rope = qraw[:, h * 3 * lanes + lanes:h * 3 * lanes + 2 * lanes]
        rope_sw = qraw[:, h * 3 * lanes + 2 * lanes:h * 3 * lanes + 3 * lanes]
        ssq = jnp.sum(nope * nope, axis=-1, keepdims=True) + jnp.sum(rope * rope, axis=-1, keepdims=True)
        r = lax.rsqrt(ssq * (1.0 / qk_dim) + EPS) * scale
        q_ref[:, h * 2 * lanes:h * 2 * lanes + lanes] = (nope * r * qgn_ref[...]).astype(BF16)
        q_ref[:, h * 2 * lanes + lanes:(h + 1) * 2 * lanes] = (
            (rope * qgr_ref[...] * cos + rope_sw * qgs_ref[...] * sin) * r).astype(BF16)

        k_nope = kvraw[:, h * 2 * lanes:h * 2 * lanes + lanes]
        v = kvraw[:, h * 2 * lanes + lanes:(h + 1) * 2 * lanes]
        kssq = jnp.sum(k_nope * k_nope, axis=-1, keepdims=True) + kr_ssq
        rk = lax.rsqrt(kssq * (1.0 / qk_dim) + EPS)
        k_ref[:, h * 2 * lanes:h * 2 * lanes + lanes] = (k_nope * rk * kgn_ref[...]).astype(BF16)
        k_ref[:, h * 2 * lanes + lanes:(h + 1) * 2 * lanes] = (kr_rot * rk).astype(BF16)
        v_ref[:, h * 2 * lanes:h * 2 * lanes + lanes] = v.astype(BF16)
        v_ref[:, h * 2 * lanes + lanes:(h + 1) * 2 * lanes] = jnp.ones((v.shape[0], lanes), BF16)


def _mla_qkv(p, cos, sin, wuq, wukv, gains, q_lora, kv_lora):
    t = p.shape[0]
    bm = ROW_BLOCK
    lanes = V7X_LANES
    head_cols = q_lora + kv_lora + 2 * lanes
    full = lambda a: pl.BlockSpec(a.shape, lambda i: (0,) * a.ndim)
    return pl.pallas_call(
        functools.partial(_mla_qkv_kernel, q_lora=q_lora, kv_lora=kv_lora,
                          scale=float(MLA_NOPE + MLA_ROPE) ** -0.5),
        grid=(t // bm,),
        in_specs=[
            pl.BlockSpec((bm, head_cols), lambda i: (i, 0)),
            pl.BlockSpec((bm, lanes), lambda i: (i, 0)),
            pl.BlockSpec((bm, lanes), lambda i: (i, 0)),
            full(wuq), full(wukv)] + [full(g) for g in gains],
        out_specs=[
            pl.BlockSpec((bm, MLA_HEADS * 2 * lanes), lambda i: (i, 0)),
            pl.BlockSpec((bm, MLA_HEADS * 2 * lanes), lambda i: (i, 0)),
            pl.BlockSpec((bm, MLA_HEADS * 2 * lanes), lambda i: (i, 0)),
        ],
        out_shape=[
            jax.ShapeDtypeStruct((t, MLA_HEADS * 2 * lanes), BF16),
            jax.ShapeDtypeStruct((t, MLA_HEADS * 2 * lanes), BF16),
            jax.ShapeDtypeStruct((t, MLA_HEADS * 2 * lanes), BF16),
        ],
        compiler_params=_cparams(("parallel",)),
        name="mla_qkv",
    )(p, cos, sin, wuq, wukv, *gains)


def _gqa_qkv_kernel(p_ref, cos_ref, sin_ref, qg_ref, kg_ref, q_ref, k_ref, v_ref, *, scale):
    hd = GQA_HD
    cos = cos_ref[...]
    sin = sin_ref[...]

    def head(x, g):
        y = x * lax.rsqrt(jnp.mean(x * x, axis=-1, keepdims=True) + EPS) * g
        return y * cos + pltpu.roll(y, hd // 2, 1) * sin

    for h in range(GQA_HEADS):
        q_ref[:, h * hd:(h + 1) * hd] = (head(p_ref[:, h * hd:(h + 1) * hd], qg_ref[...]) * scale).astype(BF16)
    k0 = GQA_HEADS * hd
    v0 = k0 + GQA_KV_HEADS * hd
    for h in range(GQA_KV_HEADS):
        k_ref[:, h * hd:(h + 1) * hd] = head(p_ref[:, k0 + h * hd:k0 + (h + 1) * hd], kg_ref[...]).astype(BF16)
    for h in range(GQA_KV_HEADS):
        v_ref[:, h * 2 * hd:h * 2 * hd + hd] = p_ref[:, v0 + h * hd:v0 + (h + 1) * hd].astype(BF16)
        v_ref[:, h * 2 * hd + hd:(h + 1) * 2 * hd] = jnp.ones((v_ref.shape[0], hd), BF16)


def _gqa_qkv(p, cos, sin, qg, kg):
    t, n = p.shape
    bm = ROW_BLOCK
    hd = GQA_HD
    return pl.pallas_call(
        functools.partial(_gqa_qkv_kernel, scale=float(hd) ** -0.5),
        grid=(t // bm,),
        in_specs=[
            pl.BlockSpec((bm, n), lambda i: (i, 0)),
            pl.BlockSpec((bm, hd), lambda i: (i, 0)),
            pl.BlockSpec((bm, hd), lambda i: (i, 0)),
            pl.BlockSpec((1, hd), lambda i: (0, 0)),
            pl.BlockSpec((1, hd), lambda i: (0, 0)),
        ],
        out_specs=[
            pl.BlockSpec((bm, GQA_HEADS * hd), lambda i: (i, 0)),
            pl.BlockSpec((bm, GQA_KV_HEADS * hd), lambda i: (i, 0)),
            pl.BlockSpec((bm, GQA_KV_HEADS * 2 * hd), lambda i: (i, 0)),
        ],
        out_shape=[
            jax.ShapeDtypeStruct((t, GQA_HEADS * hd), BF16),
            jax.ShapeDtypeStruct((t, GQA_KV_HEADS * hd), BF16),
            jax.ShapeDtypeStruct((t, GQA_KV_HEADS * 2 * hd), BF16),
        ],
        compiler_params=_cparams(("parallel",)),
        name="gqa_qkv",
    )(p, cos, sin, qg.reshape(1, hd), kg.reshape(1, hd))


def _attn_kernel(q_ref, kc_ref, vc_ref, *rest, group, dk, dv, with_latent):
    if with_latent:
        kl_ref, vl_ref, o_ref = rest
    else:
        (o_ref,) = rest
    tq = q_ref.shape[0]
    chain_rows = min(ATT_CHAIN_ROWS, group * tq)
    per_head = tq // chain_rows if chain_rows < tq else 0
    for c in range(group * tq // chain_rows):
        if per_head:
            g, part = divmod(c, per_head)
            rows = slice(part * chain_rows, (part + 1) * chain_rows)
            q = q_ref[rows, g * dk:(g + 1) * dk]
            dst = [(rows, g, slice(0, chain_rows))]
        else:
            heads = range(c * chain_rows // tq, (c + 1) * chain_rows // tq)
            q = jnp.concatenate([q_ref[:, g * dk:(g + 1) * dk] for g in heads], axis=0)
            dst = [(slice(0, tq), g, slice(n * tq, (n + 1) * tq)) for n, g in enumerate(heads)]
        sc = _nt_dot(q, kc_ref[...])
        m = jnp.max(sc, axis=-1, keepdims=True)
        if with_latent:
            sl = _nt_dot(q, kl_ref[...])
            m = jnp.maximum(m, jnp.max(sl, axis=-1, keepdims=True))
        o = jnp.dot(jnp.exp((sc - m).astype(BF16)), vc_ref[...], preferred_element_type=F32)
        if with_latent:
            o = o + jnp.dot(jnp.exp((sl - m).astype(BF16)), vl_ref[...], preferred_element_type=F32)
        o = (o[:, :dv] / o[:, dv:dv + 1]).astype(BF16)
        for rows, g, src in dst:
            o_ref[rows, g * dv:(g + 1) * dv] = o[src, :]


def _attention(q, k, v, *, batch, seq, ctx_len, kv_heads, group, dk, dv, need_ctx):
    t_lat = batch * seq
    tq = min(ATT_Q_ROWS // group, seq)
    nq = seq // tq
    ctx_blk0 = t_lat // ctx_len
    common = dict(group=group, dk=dk, dv=dv)
    lat = pl.pallas_call(
        functools.partial(_attn_kernel, with_latent=True, **common),
        grid=(batch, kv_heads, nq),
        in_specs=[
            pl.BlockSpec((tq, group * dk), lambda b, h, j: (b * nq + j, h)),
            pl.BlockSpec((ctx_len, dk), lambda b, h, j: (ctx_blk0 + b, h)),
            pl.BlockSpec((ctx_len, 2 * dv), lambda b, h, j: (ctx_blk0 + b, h)),
            pl.BlockSpec((seq, dk), lambda b, h, j: (b, h)),
            pl.BlockSpec((seq, 2 * dv), lambda b, h, j: (b, h)),
        ],
        out_specs=pl.BlockSpec((tq, group * dv), lambda b, h, j: (b * nq + j, h)),
        out_shape=jax.ShapeDtypeStruct((t_lat, kv_heads * group * dv), BF16),
        compiler_params=_cparams(("parallel", "parallel", "arbitrary")),
        name="attn_latent",
    )(q, k, v, k, v)
    if not need_ctx:
        return lat, None
    ctx = pl.pallas_call(
        functools.partial(_attn_kernel, with_latent=False, **common),
        grid=(batch, kv_heads),
        in_specs=[
            pl.BlockSpec((ctx_len, group * dk), lambda b, h: (ctx_blk0 + b, h)),
            pl.BlockSpec((ctx_len, dk), lambda b, h: (ctx_blk0 + b, h)),
            pl.BlockSpec((ctx_len, 2 * dv), lambda b, h: (ctx_blk0 + b, h)),
        ],
        out_specs=pl.BlockSpec((ctx_len, group * dv), lambda b, h: (b, h)),
        out_shape=jax.ShapeDtypeStruct((batch * ctx_len, kv_heads * group * dv), BF16),
        compiler_params=_cparams(("parallel", "parallel")),
        name="attn_ctx",
    )(q, k, v)
    return lat, ctx


def _mixout_kernel(*refs, with_conv, with_ctx, t_lat, seq, ctx_len, a_width):
    a_ref, refs = refs[0], refs[1:]
    actx_ref = None
    if with_ctx:
        actx_ref, refs = refs[0], refs[1:]
    if with_conv:
        (bg_ref, cg_ref, hz_ref, cgp_ref, hzp_ref, cgn_ref, hzn_ref, cw_ref, cb_ref,
         w_ref, x_ref, gate_ref, o_ref, cs_ref) = refs
    else:
        w_ref, x_ref, gate_ref, o_ref = refs
    i = pl.program_id(0)

    if with_conv:
        @pl.when(pl.program_id(1) == 0)
        def _():
            bm = cg_ref.shape[0]
            u = cg_ref[...] * hz_ref[...]
            u_before = cgp_ref[V7X_SUBLANES - 1:V7X_SUBLANES, :] * hzp_ref[V7X_SUBLANES - 1:V7X_SUBLANES, :]
            u_after = cgn_ref[0:1, :] * hzn_ref[0:1, :]
            local = lax.broadcasted_iota(jnp.int32, (bm, 1), 0)
            row = local + i * bm
            in_lat = row < t_lat
            seg_pos = jnp.where(in_lat, jnp.bitwise_and(row, seq - 1), jnp.bitwise_and(row - t_lat, ctx_len - 1))
            seg_len = jnp.where(in_lat, seq, ctx_len)
            up = jnp.where(local == 0, u_before, pltpu.roll(u, 1, 0))
            up = jnp.where(seg_pos == 0, 0.0, up)
            un = jnp.where(local == bm - 1, u_after, pltpu.roll(u, bm - 1, 0))
            un = jnp.where(seg_pos == seg_len - 1, 0.0, un)
            y = up * cw_ref[0:1, :] + u * cw_ref[1:2, :] + un * cw_ref[2:3, :] + cb_ref[...]
            cs_ref[...] = (bg_ref[...] * y).astype(BF16)

    def finish(a):
        acc = jnp.dot(a, w_ref[0:a_width, :], preferred_element_type=F32)
        if with_conv:
            acc = acc + jnp.dot(cs_ref[...], w_ref[a_width:, :], preferred_element_type=F32)
        o_ref[...] = x_ref[...] + gate_ref[0] * acc

    if with_ctx:
        lat_blocks = t_lat // a_ref.shape[0]
        pl.when(i < lat_blocks)(lambda: finish(a_ref[...]))
        pl.when(i >= lat_blocks)(lambda: finish(actx_ref[...]))
    else:
        finish(a_ref[...])


def _mixout(a, a_ctx, w, x, gate, group_of, conv=None, *, t_lat, seq, ctx_len):
    d = x.shape[1]
    t = x.shape[0] if a_ctx is not None else t_lat
    bm, bn = ROW_BLOCK, COL_BLOCK
    a_width = a.shape[1]
    lat_blocks = t_lat // bm
    kw = dict(t_lat=t_lat, seq=seq, ctx_len=ctx_len, a_width=a_width, with_ctx=a_ctx is not None)
    in_specs = [pl.BlockSpec((bm, a_width), lambda i, j: (jnp.minimum(i, lat_blocks - 1), 0))]
    args = [a]
    if a_ctx is not None:
        in_specs.append(pl.BlockSpec((bm, a_width), lambda i, j: (jnp.maximum(i - lat_blocks, 0), 0)))
        args.append(a_ctx)
    scratch = []
    if conv is not None:
        p, z_col0, cw, cb = conv
        cwid = cw.shape[1]
        assert z_col0 % cwid == 0
        zb = z_col0 // cwid
        sub = V7X_SUBLANES
        last_halo = p.shape[0] // sub - 1
        prev_idx = lambda i: jnp.maximum(i * (bm // sub) - 1, 0)
        next_idx = lambda i: jnp.minimum((i + 1) * (bm // sub), last_halo)
        in_specs += [
            pl.BlockSpec((bm, cwid), lambda i, j: (i, zb)),
            pl.BlockSpec((bm, cwid), lambda i, j: (i, zb + 1)),
            pl.BlockSpec((bm, cwid), lambda i, j: (i, zb + 2)),
            pl.BlockSpec((sub, cwid), lambda i, j: (prev_idx(i), zb + 1)),
            pl.BlockSpec((sub, cwid), lambda i, j: (prev_idx(i), zb + 2)),
            pl.BlockSpec((sub, cwid), lambda i, j: (next_idx(i), zb + 1)),
            pl.BlockSpec((sub, cwid), lambda i, j: (next_idx(i), zb + 2)),
            pl.BlockSpec(cw.shape, lambda i, j: (0, 0)),
            pl.BlockSpec((1, cwid), lambda i, j: (0, 0)),
        ]
        args += [p] * 7 + [cw, cb.reshape(1, cwid)]
        scratch = [pltpu.VMEM((bm, cwid), BF16)]
    in_specs += [
        pl.BlockSpec((w.shape[0], bn), lambda i, j: (0, j)),
        pl.BlockSpec((bm, bn), lambda i, j: (i, j)),
        pl.BlockSpec((1, 1, bn), lambda i, j: (group_of(i), 0, j)),
    ]
    args += [w, x, gate]
    return pl.pallas_call(
        functools.partial(_mixout_kernel, with_conv=conv is not None, **kw),
        grid=(t // bm, d // bn),
        in_specs=in_specs,
        out_specs=pl.BlockSpec((bm, bn), lambda i, j: (i, j)),
        out_shape=jax.ShapeDtypeStruct((t, d), F32),
        scratch_shapes=scratch,
        compiler_params=_cparams(("parallel", "arbitrary")),
        name="mixout_conv" if conv is not None else "mixout",
    )(*args)


_CAND_PAIRS = [(a, b) for a in range(PEER_TOPK) for b in range(PEER_TOPK // (a + 1))]
_CAND_ROWS = -(-len(_CAND_PAIRS) // V7X_SUBLANES) * V7X_SUBLANES
BF16_ROWS = 2 * V7X_SUBLANES


def _bf16_pair_words(v):
    bits = pltpu.bitcast(v.astype(BF16).astype(F32), jnp.uint32)
    return bits | (bits >> 16)


def _route_kernel(q_ref, sk_ref, r1_ref, p1_ref, np_ref, cand_ref):
    nk = N_KEYS
    k_top = PEER_TOPK
    lanes = V7X_LANES
    tb = q_ref.shape[0]
    cand_ref[len(_CAND_PAIRS):, :] = jnp.full((_CAND_ROWS - len(_CAND_PAIRS), lanes), NEG_INF, F32)
    for c in range(tb // lanes):
        cs = slice(c * lanes, (c + 1) * lanes)
        s0, s1 = (_nt_dot(sk_ref[0, half], q_ref[cs, half * nk:(half + 1) * nk].astype(BF16))
                  for half in range(2))

        cur = s0
        tops0 = []
        for k in range(k_top):
            m = jnp.max(cur, axis=0, keepdims=True)
            tops0.append(m)
            cur = jnp.where(cur == m, NEG_INF, cur)
        cur = s1
        rank = jnp.full((nk, lanes), float(k_top), F32)
        tops1 = []
        for k in range(k_top):
            m = jnp.max(cur, axis=0, keepdims=True)
            tops1.append(m)
            hit = cur == m
            rank = jnp.where(hit, float(k), rank)
            cur = jnp.where(hit, NEG_INF, cur)

        for r, (a, b) in enumerate(_CAND_PAIRS):
            cand_ref[r:r + 1, :] = tops0[a] + tops1[b]
        cand = cand_ref[...]
        top = tops0[0] + tops1[0]
        z = jnp.zeros((1, lanes), F32)
        for k in range(k_top):
            tau = jnp.max(cand, axis=0, keepdims=True)
            z = z + jnp.exp(tau - top)
            cand = jnp.where(cand == tau, NEG_INF, cand)

        n0 = jnp.zeros((nk, lanes), F32)
        for b in range(k_top):
            n0 = n0 + jnp.where(s0 + tops1[b] >= tau, 1.0, 0.0)
        n0 = _bf16_pair_words(n0)
        p0 = _bf16_pair_words(jnp.exp(s0 - tops0[0]))
        for g in range(nk // PEER_ROWS):
            np_ref[0, g, 0:PEER_ROWS, cs] = n0[g * PEER_ROWS:(g + 1) * PEER_ROWS, :]
            np_ref[0, g, PEER_ROWS:, cs] = p0[g * PEER_ROWS:(g + 1) * PEER_ROWS, :]
        rank_b = rank.astype(BF16)
        p1_b = (jnp.exp(s1 - tops1[0]) / z).astype(BF16)
        for g in range(nk // BF16_ROWS):
            r1_ref[0, g, :, cs] = pltpu.bitcast(rank_b[g * BF16_ROWS:(g + 1) * BF16_ROWS, :], jnp.int32)
            p1_ref[0, g, :, cs] = pltpu.bitcast(p1_b[g * BF16_ROWS:(g + 1) * BF16_ROWS, :], jnp.int32)


def _route(qp, sub_keys):
    t = qp.shape[0]
    tb = PEER_TB
    nk = N_KEYS
    heads = PEER_HEADS
    packed = jax.ShapeDtypeStruct((heads, nk // BF16_ROWS, V7X_SUBLANES, t), jnp.int32)
    packed_spec = pl.BlockSpec((1, nk // BF16_ROWS, V7X_SUBLANES, tb), lambda i, h: (h, 0, 0, i))
    plain = jax.ShapeDtypeStruct((heads, nk // PEER_ROWS, 2 * PEER_ROWS, t), jnp.uint32)
    plain_spec = pl.BlockSpec((1, nk // PEER_ROWS, 2 * PEER_ROWS, tb), lambda i, h: (h, 0, 0, i))
    return pl.pallas_call(
        _route_kernel,
        grid=(t // tb, heads),
        in_specs=[
            pl.BlockSpec((tb, 2 * nk), lambda i, h: (i, h)),
            pl.BlockSpec((1, 2, nk, sub_keys.shape[-1]), lambda i, h: (h, 0, 0, 0)),
        ],
        out_specs=[packed_spec, packed_spec, plain_spec],
        out_shape=[packed, packed, plain],
        scratch_shapes=[pltpu.VMEM((_CAND_ROWS, V7X_LANES), F32)],
        compiler_params=_cparams(("parallel", "arbitrary")),
        name="peer_route",
    )(qp, sub_keys)


def _peer_kernel(h_ref, u_ref, vt_ref, r1_ref, p1_ref, np_ref, x_ref, gate_ref,
                 o_ref, acc_ref, at0_ref, at1_ref, hs_ref, *, n_blocks):
    j = pl.program_id(1)
    nk = N_KEYS
    lanes = V7X_LANES
    eb, tb = at0_ref.shape
    at_refs = (at0_ref, at1_ref)
    jm = jnp.maximum(j - 1, 0)

    def key_row(hd, row, cs):
        words = jnp.broadcast_to(np_ref[hd, jm, row:row + 1, cs], (V7X_SUBLANES, lanes))
        return pltpu.bitcast(words, BF16)[None]

    def gate_rows(il, prev):
        for c in range(tb // lanes):
            cs = slice(c * lanes, (c + 1) * lanes)
            w = None
            for hd in range(PEER_HEADS):
                r1 = pltpu.bitcast(r1_ref[hd, :, :, cs], BF16)
                p1 = pltpu.bitcast(p1_ref[hd, :, :, cs], BF16)
                term = jnp.where(r1 < key_row(hd, il, cs), p1 * key_row(hd, PEER_ROWS + il, cs),
                                 jnp.zeros((), BF16))
                w = term if w is None else w + term
            a = at_refs[prev][il * nk:(il + 1) * nk, cs]
            gelu = (0.5 * a * (1.0 + lax.erf(a * SQRT_HALF))).astype(BF16)
            for g in range(nk // BF16_ROWS):
                r0 = il * nk + g * BF16_ROWS
                hs_ref[r0:r0 + BF16_ROWS, cs] = w[g] * gelu[g * BF16_ROWS:(g + 1) * BF16_ROWS, :]

    def first_matmul(cur):
        at_refs[cur][...] = _nt_dot(pltpu.bitcast(u_ref[0], BF16), pltpu.bitcast(h_ref[...], BF16))

    def mix(prev):
        for il in range(PEER_ROWS):
            gate_rows(il, prev)
        acc_ref[...] += jnp.dot(pltpu.bitcast(vt_ref[0, 0], BF16), hs_ref[...], preferred_element_type=F32)

    @pl.when(j == 0)
    def _():
        acc_ref[...] = jnp.zeros_like(acc_ref)
        first_matmul(0)

    for parity in range(2):
        @pl.when(jnp.logical_and(jnp.logical_and(j > 0, j < n_blocks), j % 2 == parity))
        def _(parity=parity):
            first_matmul(parity)
            mix(1 - parity)

    @pl.when(j == n_blocks)
    def _():
        mix((n_blocks - 1) % 2)
        o_ref[...] = x_ref[...] + gate_ref[0] * acc_ref[...].T


def _peer(h, u, vt, layer, r1, p1, n0p0, x, gate, group_of):
    t, d = x.shape
    tb, eb = PEER_TB, PEER_EB
    n_blocks = vt.shape[1]
    nk = N_KEYS
    heads = PEER_HEADS
    packed_spec = pl.BlockSpec((heads, nk // BF16_ROWS, V7X_SUBLANES, tb), lambda i, j: (0, 0, 0, i))
    plain_spec = pl.BlockSpec((heads, nk // PEER_ROWS, 2 * PEER_ROWS, tb), lambda i, j: (0, 0, 0, i))
    once = dict(pipeline_mode=pl.Buffered(1))
    return pl.pallas_call(
        functools.partial(_peer_kernel, n_blocks=n_blocks),
        grid=(t // tb, n_blocks + 1),
        in_specs=[
            pl.BlockSpec((tb // 2, d), lambda i, j: (i, 0), **once),
            pl.BlockSpec((1, eb // 2, d), lambda i, j: (layer, jnp.minimum(j, n_blocks - 1), 0)),
            pl.BlockSpec((1, 1, d // 2, eb), lambda i, j: (layer, jnp.maximum(j - 1, 0), 0, 0)),
            pl.BlockSpec(packed_spec.block_shape, packed_spec.index_map, **once),
            pl.BlockSpec(packed_spec.block_shape, packed_spec.index_map, **once),
            pl.BlockSpec(plain_spec.block_shape, plain_spec.index_map, **once),
            pl.BlockSpec((tb, d), lambda i, j: (i, 0), **once),
            pl.BlockSpec((1, 1, d), lambda i, j: (group_of(i), 0, 0)),
        ],
        out_specs=pl.BlockSpec((tb, d), lambda i, j: (i, 0)),
        out_shape=jax.ShapeDtypeStruct((t, d), F32),
        scratch_shapes=[
            pltpu.VMEM((d, tb), F32),
            pltpu.VMEM((eb, tb), F32),
            pltpu.VMEM((eb, tb), F32),
            pltpu.VMEM((eb, tb), BF16),
        ],
        compiler_params=_cparams(("parallel", "arbitrary")),
        name="peer_mix",
    )(h, u, vt, r1, p1, n0p0, x, gate)


def _pack_table_kernel(x_ref, o_ref, *, transpose):
    x = x_ref[0]
    if transpose:
        o_ref[0, 0] = pltpu.bitcast(x.T.astype(BF16), jnp.uint32)
    else:
        o_ref[0] = pltpu.bitcast(x.astype(BF16), jnp.uint32)


def _pack_table(tab, transpose):
    layers, ne, d = tab.shape
    eb = PEER_EB
    if transpose:
        out_shape = jax.ShapeDtypeStruct((layers, ne // eb, d // 2, eb), jnp.uint32)
        out_spec = pl.BlockSpec((1, 1, d // 2, eb), lambda l, j: (l, j, 0, 0))
    else:
        out_shape = jax.ShapeDtypeStruct((layers, ne // 2, d), jnp.uint32)
        out_spec = pl.BlockSpec((1, eb // 2, d), lambda l, j: (l, j, 0))
    return pl.pallas_call(
        functools.partial(_pack_table_kernel, transpose=transpose),
        grid=(layers, ne // eb),
        in_specs=[pl.BlockSpec((1, eb, d), lambda l, j: (l, j, 0))],
        out_specs=out_spec,
        out_shape=out_shape,
        compiler_params=_cparams(("parallel", "parallel")),
        name="pack_table_t" if transpose else "pack_table",
    )(tab)


def _rope_tables(seq, rope_dim, batch, ctx_rows):
    rows = seq // GRID_W
    row = jnp.repeat(jnp.arange(rows, dtype=F32), GRID_W)
    col = jnp.tile(jnp.arange(GRID_W, dtype=F32), rows)
    quarter = rope_dim // 4
    freqs = ROPE_THETA ** (-jnp.arange(quarter, dtype=F32) / quarter)
    ang = jnp.concatenate([row[:, None] * freqs, col[:, None] * freqs], axis=-1)
    cos, sin = jnp.cos(ang), jnp.sin(ang)
    pad = V7X_LANES - rope_dim
    cos_l = jnp.pad(jnp.concatenate([cos, cos], axis=-1), ((0, 0), (0, pad)))
    sin_l = jnp.pad(jnp.concatenate([-sin, sin], axis=-1), ((0, 0), (0, pad)))
    cos_c = jnp.pad(jnp.ones((ctx_rows, rope_dim), F32), ((0, 0), (0, pad)))
    sin_c = jnp.zeros((ctx_rows, V7X_LANES), F32)
    return (jnp.concatenate([jnp.tile(cos_l, (batch, 1)), cos_c], axis=0),
            jnp.concatenate([jnp.tile(sin_l, (batch, 1)), sin_c], axis=0))


def _swap_halves(a):
    half = a.shape[-1] // 2
    return jnp.concatenate([a[..., half:], a[..., :half]], axis=-1)


def _pad_lanes(a):
    return jnp.pad(a, [(0, 0)] * (a.ndim - 1) + [(0, V7X_LANES - a.shape[-1])])


def _mla_weights(w_in, w_uq, q_g, k_g, q_lora, kv_lora):
    d = w_in.shape[0]
    c_kr = q_lora + kv_lora
    w_kr = w_in[:, c_kr:c_kr + MLA_ROPE]
    w_in_ext = jnp.concatenate([
        w_in[:, :c_kr], _pad_lanes(w_kr), _pad_lanes(_swap_halves(w_kr)), w_in[:, c_kr + MLA_ROPE:]], axis=1)
    wq = w_uq.reshape(q_lora, MLA_HEADS, MLA_NOPE + MLA_ROPE)
    wq_rope = wq[:, :, MLA_NOPE:]
    wq_ext = jnp.concatenate([wq[:, :, :MLA_NOPE], _pad_lanes(wq_rope), _pad_lanes(_swap_halves(wq_rope))], axis=-1)
    wq_ext = wq_ext.reshape(q_lora, MLA_HEADS * 3 * V7X_LANES)

    def gains(g):
        g_r = g[MLA_NOPE:]
        return [g[:MLA_NOPE].reshape(1, -1), _pad_lanes(g_r).reshape(1, -1), _pad_lanes(_swap_halves(g_r)).reshape(1, -1)]

    return w_in_ext.astype(BF16), wq_ext.astype(BF16), gains(q_g) + gains(k_g)


def kernel(x, c, ctx, c_ctx, ada_w, ada_b, norm1_g, norm2_g, a_w_in, a_q_lora_g, a_kv_lora_g, a_w_uq, a_w_ukv, a_q_g, a_k_g, b_conv_w, b_conv_b, e_w_o, c_w_qkv, c_q_g, c_k_g, c_w_o, p_w_q, p_sub_keys, p_u, p_v):
    batch, seq, d = x.shape
    ctx_len = ctx.shape[1]
    depth = ada_w.shape[0]
    q_lora = a_q_lora_g.shape[1]
    kv_lora = a_kv_lora_g.shape[1]
    conv_width = b_conv_w.shape[2]
    t_lat = batch * seq
    t_ctx = batch * ctx_len
    t = t_lat + t_ctx
    bm = ROW_BLOCK
    assert seq % bm == 0 and t_ctx % bm == 0 and t % PEER_TB == 0 and seq % min(ATT_Q_ROWS, seq) == 0
    assert seq & (seq - 1) == 0 and ctx_len & (ctx_len - 1) == 0 and seq % GRID_W == 0
    assert batch + 1 <= V7X_SUBLANES

    blocks_per_batch = seq // bm
    group_of = lambda i: jnp.minimum(i // blocks_per_batch, batch)

    cvec = jnp.zeros((V7X_SUBLANES, d), F32).at[:batch].set(c).at[batch].set(c_ctx)
    mods = _ada(cvec, ada_w, ada_b)
    mods = mods.reshape(depth, V7X_SUBLANES, N_MOD, 1, d).transpose(0, 2, 1, 3, 4)

    cos_a, sin_a = _rope_tables(seq, MLA_ROPE, batch, t_ctx)
    cos_c, sin_c = _rope_tables(seq, GQA_HD, batch, t_ctx)

    u_all = _pack_table(p_u, transpose=False)
    vt_all = _pack_table(p_v, transpose=True)
    xs = jnp.concatenate([x.reshape(t_lat, d), ctx.reshape(t_ctx, d)], axis=0)
    for l in range(depth):
        sh1, sc1, g1, sh2, sc2, g2 = (mods[l, k] for k in range(N_MOD))
        i = l // 2
        need_ctx = l < depth - 1
        if l % 2 == 0:
            w_in_ext, wq_ext, gains = _mla_weights(a_w_in[i], a_w_uq[i], a_q_g[i], a_k_g[i], q_lora, kv_lora)
            p = _normproj(xs, norm1_g[l], sc1, sh1, w_in_ext, group_of, emit_h=False)
            q, k, v = _mla_qkv(p, cos_a, sin_a, wq_ext, a_w_ukv[i].astype(BF16),
                               [a_q_lora_g[i].reshape(1, -1), a_kv_lora_g[i].reshape(1, -1)] + gains,
                               q_lora, kv_lora)
            att, att_ctx = _attention(q, k, v, batch=batch, seq=seq, ctx_len=ctx_len, kv_heads=MLA_HEADS,
                                      group=1, dk=2 * V7X_LANES, dv=MLA_V, need_ctx=need_ctx)
            z_col0 = q_lora + kv_lora + 2 * V7X_LANES
            xs = _mixout(att, att_ctx, e_w_o[i].astype(BF16), xs, g1, group_of,
                         conv=(p, z_col0, b_conv_w[i], b_conv_b[i]), t_lat=t_lat, seq=seq, ctx_len=ctx_len)
        else:
            p = _normproj(xs, norm1_g[l], sc1, sh1, c_w_qkv[i].astype(BF16), group_of, emit_h=False)
            q, k, v = _gqa_qkv(p, cos_c, sin_c, c_q_g[i], c_k_g[i])
            att, att_ctx = _attention(q, k, v, batch=batch, seq=seq, ctx_len=ctx_len, kv_heads=GQA_KV_HEADS,
                                      group=GQA_HEADS // GQA_KV_HEADS, dk=GQA_HD, dv=GQA_HD, need_ctx=need_ctx)
            xs = _mixout(att, att_ctx, c_w_o[i].astype(BF16), xs, g1, group_of,
                         t_lat=t_lat, seq=seq, ctx_len=ctx_len)
        qp, h2 = _normproj(xs, norm2_g[l], sc2, sh2, p_w_q[l].astype(BF16), group_of, emit_h=True)
        r1, p1, n0p0 = _route(qp, p_sub_keys[l].astype(BF16))
        xs = _peer(h2, u_all, vt_all, l, r1, p1, n0p0, xs, g2, group_of)
    return xs.reshape(batch, seq, d)
```

```python
import functools
import math

import jax
import jax.numpy as jnp
from jax import lax
from jax.experimental import pallas as pl
from jax.experimental.pallas import tpu as pltpu

F32 = jnp.float32
BF16 = jnp.bfloat16

EPS = 1e-6
ROPE_THETA = 10000.0
GRID_W = 64
MLA_HEADS = 8
MLA_NOPE = 128
MLA_ROPE = 64
MLA_V = 128
GQA_HEADS = 16
GQA_KV_HEADS = 4
GQA_HD = 128
PEER_HEADS = 8
N_KEYS = 128
PEER_TOPK = 16
N_MOD = 6

V7X_LANES = 128
V7X_SUBLANES = 8
V7X_VMEM_LIMIT_BYTES = 56 * 1024 * 1024

ROW_BLOCK = 512
COL_BLOCK = 1024
ATT_Q_ROWS = 2048
ATT_CHAIN_ROWS = 256
PEER_TB = 512
PEER_EB = 1024
PEER_ROWS = PEER_EB // N_KEYS
SQRT_HALF = 0.7071067811865476
NEG_INF = float("-inf")


def _cparams(sem):
    return pltpu.CompilerParams(dimension_semantics=sem, vmem_limit_bytes=V7X_VMEM_LIMIT_BYTES)


def _nt_dot(a, b):
    return lax.dot_general(a, b, (((1,), (1,)), ((), ())), preferred_element_type=F32)


def _ada_kernel(c_ref, w_ref, b_ref, o_ref):
    c = c_ref[...]
    s = c / (1.0 + jnp.exp(-c))
    o_ref[0] = jnp.dot(s.astype(BF16), w_ref[0].astype(BF16), preferred_element_type=F32) + b_ref[0]


def _ada(cvec, ada_w, ada_b):
    depth, d, n = ada_w.shape
    nb = 1024
    rows = cvec.shape[0]
    return pl.pallas_call(
        _ada_kernel,
        grid=(depth, n // nb),
        in_specs=[
            pl.BlockSpec((rows, d), lambda l, j: (0, 0)),
            pl.BlockSpec((1, d, nb), lambda l, j: (l, 0, j)),
            pl.BlockSpec((1, 1, nb), lambda l, j: (l, 0, j)),
        ],
        out_specs=pl.BlockSpec((1, rows, nb), lambda l, j: (l, 0, j)),
        out_shape=jax.ShapeDtypeStruct((depth, rows, n), F32),
        compiler_params=_cparams(("parallel", "parallel")),
        name="ada_mod",
    )(cvec, ada_w, ada_b.reshape(depth, 1, n))


def _normproj_kernel(x_ref, g_ref, sc_ref, sh_ref, w_ref, o_ref, *rest, emit_h):
    hs_ref = rest[-1]

    @pl.when(pl.program_id(1) == 0)
    def _():
        x = x_ref[...]
        y = x * lax.rsqrt(jnp.mean(x * x, axis=-1, keepdims=True) + EPS) * g_ref[...]
        h = (y * (1.0 + sc_ref[0]) + sh_ref[0]).astype(BF16)
        hs_ref[...] = h
        if emit_h:
            rest[0][...] = pltpu.bitcast(h, jnp.uint32)

    out = jnp.dot(hs_ref[...], w_ref[...], preferred_element_type=F32)
    o_ref[...] = pltpu.bitcast(out.astype(BF16), jnp.uint32) if emit_h else out


def _normproj(x, g, sc, sh, w, group_of, emit_h):
    t, d = x.shape
    n = w.shape[1]
    bm, bn = ROW_BLOCK, COL_BLOCK
    out_shape = [jax.ShapeDtypeStruct((t, n), F32)]
    out_specs = [pl.BlockSpec((bm, bn), lambda i, j: (i, j))]
    if emit_h:
        out_shape = [jax.ShapeDtypeStruct((t // 2, n), jnp.uint32)]
        out_specs = [pl.BlockSpec((bm // 2, bn), lambda i, j: (i, j))]
        out_shape.append(jax.ShapeDtypeStruct((t // 2, d), jnp.uint32))
        out_specs.append(pl.BlockSpec((bm // 2, d), lambda i, j: (i, 0)))
    res = pl.pallas_call(
        functools.partial(_normproj_kernel, emit_h=emit_h),
        grid=(t // bm, n // bn),
        in_specs=[
            pl.BlockSpec((bm, d), lambda i, j: (i, 0)),
            pl.BlockSpec((1, d), lambda i, j: (0, 0)),
            pl.BlockSpec((1, 1, d), lambda i, j: (group_of(i), 0, 0)),
            pl.BlockSpec((1, 1, d), lambda i, j: (group_of(i), 0, 0)),
            pl.BlockSpec((d, bn), lambda i, j: (0, j)),
        ],
        out_specs=out_specs,
        out_shape=out_shape,
        scratch_shapes=[pltpu.VMEM((bm, d), BF16)],
        compiler_params=_cparams(("parallel", "arbitrary")),
        name="normproj_h" if emit_h else "normproj",
    )(x, g.reshape(1, d), sc, sh, w)
    return res if emit_h else res[0]


def _mla_qkv_kernel(p_ref, cos_ref, sin_ref, wuq_ref, wukv_ref, qlg_ref, kvlg_ref,
                    qgn_ref, qgr_ref, qgs_ref, kgn_ref, kgr_ref, kgs_ref,
                    q_ref, k_ref, v_ref, *, q_lora, kv_lora, scale):
    qk_dim = MLA_NOPE + MLA_ROPE
    lanes = V7X_LANES
    cq = p_ref[:, 0:q_lora]
    ckv = p_ref[:, q_lora:q_lora + kv_lora]
    kr = p_ref[:, q_lora + kv_lora:q_lora + kv_lora + lanes]
    krs = p_ref[:, q_lora + kv_lora + lanes:q_lora + kv_lora + 2 * lanes]
    cos = cos_ref[...]
    sin = sin_ref[...]

    cqn = cq * lax.rsqrt(jnp.mean(cq * cq, axis=-1, keepdims=True) + EPS) * qlg_ref[...]
    qraw = jnp.dot(cqn.astype(BF16), wuq_ref[...], preferred_element_type=F32)
    ckvn = ckv * lax.rsqrt(jnp.mean(ckv * ckv, axis=-1, keepdims=True) + EPS) * kvlg_ref[...]
    kvraw = jnp.dot(ckvn.astype(BF16), wukv_ref[...], preferred_element_type=F32)

    kr_rot = kr * kgr_ref[...] * cos + krs * kgs_ref[...] * sin
    kr_ssq = jnp.sum(kr * kr, axis=-1, keepdims=True)
    for h in range(MLA_HEADS):
        nope = qraw[:, h * 3 * lanes:h * 3 * lanes + lanes]
        rope = qraw[:, h * 3 * lanes + lanes:h * 3 * lanes + 2 * lanes]
        rope_sw = qraw[:, h * 3 * lanes + 2 * lanes:h * 3 * lanes + 3 * lanes]
        ssq = jnp.sum(nope * nope, axis=-1, keepdims=True) + jnp.sum(rope * rope, axis=-1, keepdims=True)
        r = lax.rsqrt(ssq * (1.0 / qk_dim) + EPS) * scale
        q_ref[:, h * 2 * lanes:h * 2 * lanes + lanes] = (nope * r * qgn_ref[...]).astype(BF16)
        q_ref[:, h * 2 * lanes + lanes:(h + 1) * 2 * lanes] = (
            (rope * qgr_ref[...] * cos + rope_sw * qgs_ref[...] * sin) * r).astype(BF16)

        k_nope = kvraw[:, h * 2 * lanes:h * 2 * lanes + lanes]
        v = kvraw[:, h * 2 * lanes + lanes:(h + 1) * 2 * lanes]
        kssq = jnp.sum(k_nope * k_nope, axis=-1, keepdims=True) + kr_ssq
        rk = lax.rsqrt(kssq * (1.0 / qk_dim) + EPS)
        k_ref[:, h * 2 * lanes:h * 2 * lanes + lanes] = (k_nope * rk * kgn_ref[...]).astype(BF16)
        k_ref[:, h * 2 * lanes + lanes:(h + 1) * 2 * lanes] = (kr_rot * rk).astype(BF16)
        v_ref[:, h * 2 * lanes:h * 2 * lanes + lanes] = v.astype(BF16)
        v_ref[:, h * 2 * lanes + lanes:(h + 1) * 2 * lanes] = jnp.ones((v.shape[0], lanes), BF16)


def _mla_qkv(p, cos, sin, wuq, wukv, gains, q_lora, kv_lora):
    t = p.shape[0]
    bm = ROW_BLOCK
    lanes = V7X_LANES
    head_cols = q_lora + kv_lora + 2 * lanes
    full = lambda a: pl.BlockSpec(a.shape, lambda i: (0,) * a.ndim)
    return pl.pallas_call(
        functools.partial(_mla_qkv_kernel, q_lora=q_lora, kv_lora=kv_lora,
                          scale=float(MLA_NOPE + MLA_ROPE) ** -0.5),
        grid=(t // bm,),
        in_specs=[
            pl.BlockSpec((bm, head_cols), lambda i: (i, 0)),
            pl.BlockSpec((bm, lanes), lambda i: (i, 0)),
            pl.BlockSpec((bm, lanes), lambda i: (i, 0)),
            full(wuq), full(wukv)] + [full(g) for g in gains],
        out_specs=[
            pl.BlockSpec((bm, MLA_HEADS * 2 * lanes), lambda i: (i, 0)),
            pl.BlockSpec((bm, MLA_HEADS * 2 * lanes), lambda i: (i, 0)),
            pl.BlockSpec((bm, MLA_HEADS * 2 * lanes), lambda i: (i, 0)),
        ],
        out_shape=[
            jax.ShapeDtypeStruct((t, MLA_HEADS * 2 * lanes), BF16),
            jax.ShapeDtypeStruct((t, MLA_HEADS * 2 * lanes), BF16),
            jax.ShapeDtypeStruct((t, MLA_HEADS * 2 * lanes), BF16),
        ],
        compiler_params=_cparams(("parallel",)),
        name="mla_qkv",
    )(p, cos, sin, wuq, wukv, *gains)


def _gqa_qkv_kernel(p_ref, cos_ref, sin_ref, qg_ref, kg_ref, q_ref, k_ref, v_ref, *, scale):
    hd = GQA_HD
    cos = cos_ref[...]
    sin = sin_ref[...]

    def head(x, g):
        y = x * lax.rsqrt(jnp.mean(x * x, axis=-1, keepdims=True) + EPS) * g
        return y * cos + pltpu.roll(y, hd // 2, 1) * sin

    for h in range(GQA_HEADS):
        q_ref[:, h * hd:(h + 1) * hd] = (head(p_ref[:, h * hd:(h + 1) * hd], qg_ref[...]) * scale).astype(BF16)
    k0 = GQA_HEADS * hd
    v0 = k0 + GQA_KV_HEADS * hd
    for h in range(GQA_KV_HEADS):
        k_ref[:, h * hd:(h + 1) * hd] = head(p_ref[:, k0 + h * hd:k0 + (h + 1) * hd], kg_ref[...]).astype(BF16)
    for h in range(GQA_KV_HEADS):
        v_ref[:, h * 2 * hd:h * 2 * hd + hd] = p_ref[:, v0 + h * hd:v0 + (h + 1) * hd].astype(BF16)
        v_ref[:, h * 2 * hd + hd:(h + 1) * 2 * hd] = jnp.ones((v_ref.shape[0], hd), BF16)


def _gqa_qkv(p, cos, sin, qg, kg):
    t, n = p.shape
    bm = ROW_BLOCK
    hd = GQA_HD
    return pl.pallas_call(
        functools.partial(_gqa_qkv_kernel, scale=float(hd) ** -0.5),
        grid=(t // bm,),
        in_specs=[
            pl.BlockSpec((bm, n), lambda i: (i, 0)),
            pl.BlockSpec((bm, hd), lambda i: (i, 0)),
            pl.BlockSpec((bm, hd), lambda i: (i, 0)),
            pl.BlockSpec((1, hd), lambda i: (0, 0)),
            pl.BlockSpec((1, hd), lambda i: (0, 0)),
        ],
        out_specs=[
            pl.BlockSpec((bm, GQA_HEADS * hd), lambda i: (i, 0)),
            pl.BlockSpec((bm, GQA_KV_HEADS * hd), lambda i: (i, 0)),
            pl.BlockSpec((bm, GQA_KV_HEADS * 2 * hd), lambda i: (i, 0)),
        ],
        out_shape=[
            jax.ShapeDtypeStruct((t, GQA_HEADS * hd), BF16),
            jax.ShapeDtypeStruct((t, GQA_KV_HEADS * hd), BF16),
            jax.ShapeDtypeStruct((t, GQA_KV_HEADS * 2 * hd), BF16),
        ],
        compiler_params=_cparams(("parallel",)),
        name="gqa_qkv",
    )(p, cos, sin, qg.reshape(1, hd), kg.reshape(1, hd))


def _attn_kernel(q_ref, kc_ref, vc_ref, *rest, group, dk, dv, with_latent):
    if with_latent:
        kl_ref, vl_ref, o_ref = rest
    else:
        (o_ref,) = rest
    tq = q_ref.shape[0]
    chain_rows = min(ATT_CHAIN_ROWS, group * tq)
    per_head = tq // chain_rows if chain_rows < tq else 0
    for c in range(group * tq // chain_rows):
        if per_head:
            g, part = divmod(c, per_head)
            rows = slice(part * chain_rows, (part + 1) * chain_rows)
            q = q_ref[rows, g * dk:(g + 1) * dk]
            dst = [(rows, g, slice(0, chain_rows))]
        else:
            heads = range(c * chain_rows // tq, (c + 1) * chain_rows // tq)
            q = jnp.concatenate([q_ref[:, g * dk:(g + 1) * dk] for g in heads], axis=0)
            dst = [(slice(0, tq), g, slice(n * tq, (n + 1) * tq)) for n, g in enumerate(heads)]
        sc = _nt_dot(q, kc_ref[...])
        m = jnp.max(sc, axis=-1, keepdims=True)
        if with_latent:
            sl = _nt_dot(q, kl_ref[...])
            m = jnp.maximum(m, jnp.max(sl, axis=-1, keepdims=True))
        o = jnp.dot(jnp.exp((sc - m).astype(BF16)), vc_ref[...], preferred_element_type=F32)
        if with_latent:
            o = o + jnp.dot(jnp.exp((sl - m).astype(BF16)), vl_ref[...], preferred_element_type=F32)
        o = (o[:, :dv] / o[:, dv:dv + 1]).astype(BF16)
        for rows, g, src in dst:
            o_ref[rows, g * dv:(g + 1) * dv] = o[src, :]


def _attention(q, k, v, *, batch, seq, ctx_len, kv_heads, group, dk, dv, need_ctx):
    t_lat = batch * seq
    tq = min(ATT_Q_ROWS // group, seq)
    nq = seq // tq
    ctx_blk0 = t_lat // ctx_len
    common = dict(group=group, dk=dk, dv=dv)
    lat = pl.pallas_call(
        functools.partial(_attn_kernel, with_latent=True, **common),
        grid=(batch, kv_heads, nq),
        in_specs=[
            pl.BlockSpec((tq, group * dk), lambda b, h, j: (b * nq + j, h)),
            pl.BlockSpec((ctx_len, dk), lambda b, h, j: (ctx_blk0 + b, h)),
            pl.BlockSpec((ctx_len, 2 * dv), lambda b, h, j: (ctx_blk0 + b, h)),
            pl.BlockSpec((seq, dk), lambda b, h, j: (b, h)),
            pl.BlockSpec((seq, 2 * dv), lambda b, h, j: (b, h)),
        ],
        out_specs=pl.BlockSpec((tq, group * dv), lambda b, h, j: (b * nq + j, h)),
        out_shape=jax.ShapeDtypeStruct((t_lat, kv_heads * group * dv), BF16),
        compiler_params=_cparams(("parallel", "parallel", "arbitrary")),
        name="attn_latent",
    )(q, k, v, k, v)
    if not need_ctx:
        return lat, None
    ctx = pl.pallas_call(
        functools.partial(_attn_kernel, with_latent=False, **common),
        grid=(batch, kv_heads),
        in_specs=[
            pl.BlockSpec((ctx_len, group * dk), lambda b, h: (ctx_blk0 + b, h)),
            pl.BlockSpec((ctx_len, dk), lambda b, h: (ctx_blk0 + b, h)),
            pl.BlockSpec((ctx_len, 2 * dv), lambda b, h: (ctx_blk0 + b, h)),
        ],
        out_specs=pl.BlockSpec((ctx_len, group * dv), lambda b, h: (b, h)),
        out_shape=jax.ShapeDtypeStruct((batch * ctx_len, kv_heads * group * dv), BF16),
        compiler_params=_cparams(("parallel", "parallel")),
        name="attn_ctx",
    )(q, k, v)
    return lat, ctx


def _mixout_kernel(*refs, with_conv, with_ctx, t_lat, seq, ctx_len, a_width):
    a_ref, refs = refs[0], refs[1:]
    actx_ref = None
    if with_ctx:
        actx_ref, refs = refs[0], refs[1:]
    if with_conv:
        (bg_ref, cg_ref, hz_ref, cgp_ref, hzp_ref, cgn_ref, hzn_ref, cw_ref, cb_ref,
         w_ref, x_ref, gate_ref, o_ref, cs_ref) = refs
    else:
        w_ref, x_ref, gate_ref, o_ref = refs
    i = pl.program_id(0)

    if with_conv:
        @pl.when(pl.program_id(1) == 0)
        def _():
            bm = cg_ref.shape[0]
            u = cg_ref[...] * hz_ref[...]
            u_before = cgp_ref[V7X_SUBLANES - 1:V7X_SUBLANES, :] * hzp_ref[V7X_SUBLANES - 1:V7X_SUBLANES, :]
            u_after = cgn_ref[0:1, :] * hzn_ref[0:1, :]
            local = lax.broadcasted_iota(jnp.int32, (bm, 1), 0)
            row = local + i * bm
            in_lat = row < t_lat
            seg_pos = jnp.where(in_lat, jnp.bitwise_and(row, seq - 1), jnp.bitwise_and(row - t_lat, ctx_len - 1))
            seg_len = jnp.where(in_lat, seq, ctx_len)
            up = jnp.where(local == 0, u_before, pltpu.roll(u, 1, 0))
            up = jnp.where(seg_pos == 0, 0.0, up)
            un = jnp.where(local == bm - 1, u_after, pltpu.roll(u, bm - 1, 0))
            un = jnp.where(seg_pos == seg_len - 1, 0.0, un)
            y = up * cw_ref[0:1, :] + u * cw_ref[1:2, :] + un * cw_ref[2:3, :] + cb_ref[...]
            cs_ref[...] = (bg_ref[...] * y).astype(BF16)

    def finish(a):
        acc = jnp.dot(a, w_ref[0:a_width, :], preferred_element_type=F32)
        if with_conv:
            acc = acc + jnp.dot(cs_ref[...], w_ref[a_width:, :], preferred_element_type=F32)
        o_ref[...] = x_ref[...] + gate_ref[0] * acc

    if with_ctx:
        lat_blocks = t_lat // a_ref.shape[0]
        pl.when(i < lat_blocks)(lambda: finish(a_ref[...]))
        pl.when(i >= lat_blocks)(lambda: finish(actx_ref[...]))
    else:
        finish(a_ref[...])


def _mixout(a, a_ctx, w, x, gate, group_of, conv=None, *, t_lat, seq, ctx_len):
    d = x.shape[1]
    t = x.shape[0] if a_ctx is not None else t_lat
    bm, bn = ROW_BLOCK, COL_BLOCK
    a_width = a.shape[1]
    lat_blocks = t_lat // bm
    kw = dict(t_lat=t_lat, seq=seq, ctx_len=ctx_len, a_width=a_width, with_ctx=a_ctx is not None)
    in_specs = [pl.BlockSpec((bm, a_width), lambda i, j: (jnp.minimum(i, lat_blocks - 1), 0))]
    args = [a]
    if a_ctx is not None:
        in_specs.append(pl.BlockSpec((bm, a_width), lambda i, j: (jnp.maximum(i - lat_blocks, 0), 0)))
        args.append(a_ctx)
    scratch = []
    if conv is not None:
        p, z_col0, cw, cb = conv
        cwid = cw.shape[1]
        assert z_col0 % cwid == 0
        zb = z_col0 // cwid
        sub = V7X_SUBLANES
        last_halo = p.shape[0] // sub - 1
        prev_idx = lambda i: jnp.maximum(i * (bm // sub) - 1, 0)
        next_idx = lambda i: jnp.minimum((i + 1) * (bm // sub), last_halo)
        in_specs += [
            pl.BlockSpec((bm, cwid), lambda i, j: (i, zb)),
            pl.BlockSpec((bm, cwid), lambda i, j: (i, zb + 1)),
            pl.BlockSpec((bm, cwid), lambda i, j: (i, zb + 2)),
            pl.BlockSpec((sub, cwid), lambda i, j: (prev_idx(i), zb + 1)),
            pl.BlockSpec((sub, cwid), lambda i, j: (prev_idx(i), zb + 2)),
            pl.BlockSpec((sub, cwid), lambda i, j: (next_idx(i), zb + 1)),
            pl.BlockSpec((sub, cwid), lambda i, j: (next_idx(i), zb + 2)),
            pl.BlockSpec(cw.shape, lambda i, j: (0, 0)),
            pl.BlockSpec((1, cwid), lambda i, j: (0, 0)),
        ]
        args += [p] * 7 + [cw, cb.reshape(1, cwid)]
        scratch = [pltpu.VMEM((bm, cwid), BF16)]
    in_specs += [
        pl.BlockSpec((w.shape[0], bn), lambda i, j: (0, j)),
        pl.BlockSpec((bm, bn), lambda i, j: (i, j)),
        pl.BlockSpec((1, 1, bn), lambda i, j: (group_of(i), 0, j)),
    ]
    args += [w, x, gate]
    return pl.pallas_call(
        functools.partial(_mixout_kernel, with_conv=conv is not None, **kw),
        grid=(t // bm, d // bn),
        in_specs=in_specs,
        out_specs=pl.BlockSpec((bm, bn), lambda i, j: (i, j)),
        out_shape=jax.ShapeDtypeStruct((t, d), F32),
        scratch_shapes=scratch,
        compiler_params=_cparams(("parallel", "arbitrary")),
        name="mixout_conv" if conv is not None else "mixout",
    )(*args)


_CAND_PAIRS = [(a, b) for a in range(PEER_TOPK) for b in range(PEER_TOPK // (a + 1))]
_CAND_ROWS = -(-len(_CAND_PAIRS) // V7X_SUBLANES) * V7X_SUBLANES
BF16_ROWS = 2 * V7X_SUBLANES


def _oddeven_sort_pairs(n):
    pairs = []

    def merge(lo, hi, r):
        step = r * 2
        if step < hi - lo:
            merge(lo, hi, step)
            merge(lo + r, hi, step)
            pairs.extend((i, i + r) for i in range(lo + r, hi - r, step))
        else:
            pairs.append((lo, lo + r))

    def sort(lo, hi):
        if hi - lo >= 1:
            mid = lo + (hi - lo) // 2
            sort(lo, mid)
            sort(mid + 1, hi)
            merge(lo, hi, 1)

    sort(0, n - 1)
    return pairs


_SORT_PAIRS = _oddeven_sort_pairs(PEER_TOPK)


def _top_sorted(s):
    sub = V7X_SUBLANES
    k = PEER_TOPK
    assert s.shape[0] == sub * k
    v = [s[r * sub:(r + 1) * sub, :] for r in range(k)]

    def exchange(i, j):
        v[i], v[j] = jnp.maximum(v[i], v[j]), jnp.minimum(v[i], v[j])

    for i, j in _SORT_PAIRS:
        exchange(i, j)
    shift = sub // 2
    while shift:
        v = [jnp.maximum(v[r], pltpu.roll(v[k - 1 - r], shift, 0)) for r in range(k)]
        dist = k // 2
        while dist:
            for i in range(k):
                if not i & dist:
                    exchange(i, i + dist)
            dist //= 2
        shift //= 2
    return v


def _bf16_pair_words(v):
    bits = pltpu.bitcast(v.astype(BF16).astype(F32), jnp.uint32)
    return bits | (bits >> 16)


def _route_kernel(q_ref, sk_ref, r1_ref, p1_ref, np_ref, cand_ref):
    nk = N_KEYS
    k_top = PEER_TOPK
    lanes = V7X_LANES
    q = pltpu.bitcast(q_ref[...], BF16)
    tb = q.shape[0]
    cand_ref[len(_CAND_PAIRS):, :] = jnp.full((_CAND_ROWS - len(_CAND_PAIRS), lanes), NEG_INF, F32)
    for c in range(tb // lanes):
        cs = slice(c * lanes, (c + 1) * lanes)
        s0, s1 = (_nt_dot(sk_ref[0, half], q[cs, half * nk:(half + 1) * nk])
                  for half in range(2))

        tops0 = [v[0:1, :] for v in _top_sorted(s0)]
        tops1 = [v[0:1, :] for v in _top_sorted(s1)]
        rank = jnp.full((nk, lanes), float(k_top), F32)
        for b in reversed(range(k_top)):
            rank = jnp.where(s1 >= tops1[b], float(b), rank)

        for r, (a, b) in enumerate(_CAND_PAIRS):
            cand_ref[r:r + 1, :] = tops0[a] + tops1[b]
        cand = cand_ref[...]
        top = tops0[0] + tops1[0]
        z = jnp.zeros((1, lanes), F32)
        for k in range(k_top):
            tau = jnp.max(cand, axis=0, keepdims=True)
            z = z + jnp.exp(tau - top)
            cand = jnp.where(cand == tau, NEG_INF, cand)

        n0 = jnp.zeros((nk, lanes), F32)
        for b in range(k_top):
            n0 = n0 + jnp.where(s0 + tops1[b] >= tau, 1.0, 0.0)
        n0 = _bf16_pair_words(n0)
        p0 = _bf16_pair_words(jnp.exp(s0 - tops0[0]))
        for g in range(nk // PEER_ROWS):
            np_ref[0, g, 0:PEER_ROWS, cs] = n0[g * PEER_ROWS:(g + 1) * PEER_ROWS, :]
            np_ref[0, g, PEER_ROWS:, cs] = p0[g * PEER_ROWS:(g + 1) * PEER_ROWS, :]
        rank_b = rank.astype(BF16)
        p1_b = (jnp.exp(s1 - tops1[0]) / z).astype(BF16)
        for g in range(nk // BF16_ROWS):
            r1_ref[0, g, :, cs] = pltpu.bitcast(rank_b[g * BF16_ROWS:(g + 1) * BF16_ROWS, :], jnp.int32)
            p1_ref[0, g, :, cs] = pltpu.bitcast(p1_b[g * BF16_ROWS:(g + 1) * BF16_ROWS, :], jnp.int32)


def _route(qp, sub_keys):
    t = 2 * qp.shape[0]
    tb = PEER_TB
    nk = N_KEYS
    heads = PEER_HEADS
    packed = jax.ShapeDtypeStruct((heads, nk // BF16_ROWS, V7X_SUBLANES, t), jnp.int32)
    packed_spec = pl.BlockSpec((1, nk // BF16_ROWS, V7X_SUBLANES, tb), lambda i, h: (h, 0, 0, i))
    plain = jax.ShapeDtypeStruct((heads, nk // PEER_ROWS, 2 * PEER_ROWS, t), jnp.uint32)
    plain_spec = pl.BlockSpec((1, nk // PEER_ROWS, 2 * PEER_ROWS, tb), lambda i, h: (h, 0, 0, i))
    return pl.pallas_call(
        _route_kernel,
        grid=(t // tb, heads),
        in_specs=[
            pl.BlockSpec((tb // 2, 2 * nk), lambda i, h: (i, h)),
            pl.BlockSpec((1, 2, nk, sub_keys.shape[-1]), lambda i, h: (h, 0, 0, 0)),
        ],
        out_specs=[packed_spec, packed_spec, plain_spec],
        out_shape=[packed, packed, plain],
        scratch_shapes=[pltpu.VMEM((_CAND_ROWS, V7X_LANES), F32)],
        compiler_params=_cparams(("parallel", "arbitrary")),
        name="peer_route",
    )(qp, sub_keys)


def _peer_kernel(h_ref, u_ref, vt_ref, r1_ref, p1_ref, np_ref, x_ref, gate_ref,
                 o_ref, acc_ref, at0_ref, at1_ref, hs_ref, *, n_blocks):
    j = pl.program_id(1)
    nk = N_KEYS
    lanes = V7X_LANES
    eb, tb = at0_ref.shape
    at_refs = (at0_ref, at1_ref)
    jm = jnp.maximum(j - 1, 0)

    def key_row(hd, row, cs):
        words = jnp.broadcast_to(np_ref[hd, jm, row:row + 1, cs], (V7X_SUBLANES, lanes))
        return pltpu.bitcast(words, BF16)[None]

    def gate_rows(il, prev):
        for c in range(tb // lanes):
            cs = slice(c * lanes, (c + 1) * lanes)
            w = None
            for hd in range(PEER_HEADS):
                r1 = pltpu.bitcast(r1_ref[hd, :, :, cs], BF16)
                p1 = pltpu.bitcast(p1_ref[hd, :, :, cs], BF16)
                term = jnp.where(r1 < key_row(hd, il, cs), p1 * key_row(hd, PEER_ROWS + il, cs),
                                 jnp.zeros((), BF16))
                w = term if w is None else w + term
            a = at_refs[prev][il * nk:(il + 1) * nk, cs]
            gelu = (0.5 * a * (1.0 + lax.erf(a * SQRT_HALF))).astype(BF16)
            for g in range(nk // BF16_ROWS):
                r0 = il * nk + g * BF16_ROWS
                hs_ref[r0:r0 + BF16_ROWS, cs] = w[g] * gelu[g * BF16_ROWS:(g + 1) * BF16_ROWS, :]

    def first_matmul(cur):
        at_refs[cur][...] = _nt_dot(pltpu.bitcast(u_ref[0], BF16), pltpu.bitcast(h_ref[...], BF16))

    def mix(prev):
        for il in range(PEER_ROWS):
            gate_rows(il, prev)
        acc_ref[...] += jnp.dot(pltpu.bitcast(vt_ref[0, 0], BF16), hs_ref[...], preferred_element_type=F32)

    @pl.when(j == 0)
    def _():
        acc_ref[...] = jnp.zeros_like(acc_ref)
        first_matmul(0)

    for parity in range(2):
        @pl.when(jnp.logical_and(jnp.logical_and(j > 0, j < n_blocks), j % 2 == parity))
        def _(parity=parity):
            first_matmul(parity)
            mix(1 - parity)

    @pl.when(j == n_blocks)
    def _():
        mix((n_blocks - 1) % 2)
        o_ref[...] = x_ref[...] + gate_ref[0] * acc_ref[...].T


def _peer(h, u, vt, layer, r1, p1, n0p0, x, gate, group_of):
    t, d = x.shape
    tb, eb = PEER_TB, PEER_EB
    n_blocks = vt.shape[1]
    nk = N_KEYS
    heads = PEER_HEADS
    packed_spec = pl.BlockSpec((heads, nk // BF16_ROWS, V7X_SUBLANES, tb), lambda i, j: (0, 0, 0, i))
    plain_spec = pl.BlockSpec((heads, nk // PEER_ROWS, 2 * PEER_ROWS, tb), lambda i, j: (0, 0, 0, i))
    once = dict(pipeline_mode=pl.Buffered(1))
    return pl.pallas_call(
        functools.partial(_peer_kernel, n_blocks=n_blocks),
        grid=(t // tb, n_blocks + 1),
        in_specs=[
            pl.BlockSpec((tb // 2, d), lambda i, j: (i, 0), **once),
            pl.BlockSpec((1, eb // 2, d), lambda i, j: (layer, jnp.minimum(j, n_blocks - 1), 0)),
            pl.BlockSpec((1, 1, d // 2, eb), lambda i, j: (layer, jnp.maximum(j - 1, 0), 0, 0)),
            pl.BlockSpec(packed_spec.block_shape, packed_spec.index_map, **once),
            pl.BlockSpec(packed_spec.block_shape, packed_spec.index_map, **once),
            pl.BlockSpec(plain_spec.block_shape, plain_spec.index_map, **once),
            pl.BlockSpec((tb, d), lambda i, j: (i, 0), **once),
            pl.BlockSpec((1, 1, d), lambda i, j: (group_of(i), 0, 0)),
        ],
        out_specs=pl.BlockSpec((tb, d), lambda i, j: (i, 0)),
        out_shape=jax.ShapeDtypeStruct((t, d), F32),
        scratch_shapes=[
            pltpu.VMEM((d, tb), F32),
            pltpu.VMEM((eb, tb), F32),
            pltpu.VMEM((eb, tb), F32),
            pltpu.VMEM((eb, tb), BF16),
        ],
        compiler_params=_cparams(("parallel", "arbitrary")),
        name="peer_mix",
    )(h, u, vt, r1, p1, n0p0, x, gate)


def _pack_table_kernel(x_ref, o_ref, *, transpose):
    x = x_ref[0]
    if transpose:
        o_ref[0, 0] = pltpu.bitcast(x.T.astype(BF16), jnp.uint32)
    else:
        o_ref[0] = pltpu.bitcast(x.astype(BF16), jnp.uint32)


def _pack_table(tab, transpose):
    layers, ne, d = tab.shape
    eb = PEER_EB
    if transpose:
        out_shape = jax.ShapeDtypeStruct((layers, ne // eb, d // 2, eb), jnp.uint32)
        out_spec = pl.BlockSpec((1, 1, d // 2, eb), lambda l, j: (l, j, 0, 0))
    else:
        out_shape = jax.ShapeDtypeStruct((layers, ne // 2, d), jnp.uint32)
        out_spec = pl.BlockSpec((1, eb // 2, d), lambda l, j: (l, j, 0))
    return pl.pallas_call(
        functools.partial(_pack_table_kernel, transpose=transpose),
        grid=(layers, ne // eb),
        in_specs=[pl.BlockSpec((1, eb, d), lambda l, j: (l, j, 0))],
        out_specs=out_spec,
        out_shape=out_shape,
        compiler_params=_cparams(("parallel", "parallel")),
        name="pack_table_t" if transpose else "pack_table",
    )(tab)


def _rope_tables(seq, rope_dim, batch, ctx_rows):
    rows = seq // GRID_W
    row = jnp.repeat(jnp.arange(rows, dtype=F32), GRID_W)
    col = jnp.tile(jnp.arange(GRID_W, dtype=F32), rows)
    quarter = rope_dim // 4
    freqs = ROPE_THETA ** (-jnp.arange(quarter, dtype=F32) / quarter)
    ang = jnp.concatenate([row[:, None] * freqs, col[:, None] * freqs], axis=-1)
    cos, sin = jnp.cos(ang), jnp.sin(ang)
    pad = V7X_LANES - rope_dim
    cos_l = jnp.pad(jnp.concatenate([cos, cos], axis=-1), ((0, 0), (0, pad)))
    sin_l = jnp.pad(jnp.concatenate([-sin, sin], axis=-1), ((0, 0), (0, pad)))
    cos_c = jnp.pad(jnp.ones((ctx_rows, rope_dim), F32), ((0, 0), (0, pad)))
    sin_c = jnp.zeros((ctx_rows, V7X_LANES), F32)
    return (jnp.concatenate([jnp.tile(cos_l, (batch, 1)), cos_c], axis=0),
            jnp.concatenate([jnp.tile(sin_l, (batch, 1)), sin_c], axis=0))


def _swap_halves(a):
    half = a.shape[-1] // 2
    return jnp.concatenate([a[..., half:], a[..., :half]], axis=-1)


def _pad_lanes(a):
    return jnp.pad(a, [(0, 0)] * (a.ndim - 1) + [(0, V7X_LANES - a.shape[-1])])


def _mla_weights(w_in, w_uq, q_g, k_g, q_lora, kv_lora):
    d = w_in.shape[0]
    c_kr = q_lora + kv_lora
    w_kr = w_in[:, c_kr:c_kr + MLA_ROPE]
    w_in_ext = jnp.concatenate([
        w_in[:, :c_kr], _pad_lanes(w_kr), _pad_lanes(_swap_halves(w_kr)), w_in[:, c_kr + MLA_ROPE:]], axis=1)
    wq = w_uq.reshape(q_lora, MLA_HEADS, MLA_NOPE + MLA_ROPE)
    wq_rope = wq[:, :, MLA_NOPE:]
    wq_ext = jnp.concatenate([wq[:, :, :MLA_NOPE], _pad_lanes(wq_rope), _pad_lanes(_swap_halves(wq_rope))], axis=-1)
    wq_ext = wq_ext.reshape(q_lora, MLA_HEADS * 3 * V7X_LANES)

    def gains(g):
        g_r = g[MLA_NOPE:]
        return [g[:MLA_NOPE].reshape(1, -1), _pad_lanes(g_r).reshape(1, -1), _pad_lanes(_swap_halves(g_r)).reshape(1, -1)]

    return w_in_ext.astype(BF16), wq_ext.astype(BF16), gains(q_g) + gains(k_g)


def kernel(x, c, ctx, c_ctx, ada_w, ada_b, norm1_g, norm2_g, a_w_in, a_q_lora_g, a_kv_lora_g, a_w_uq, a_w_ukv, a_q_g, a_k_g, b_conv_w, b_conv_b, e_w_o, c_w_qkv, c_q_g, c_k_g, c_w_o, p_w_q, p_sub_keys, p_u, p_v):
    batch, seq, d = x.shape
    ctx_len = ctx.shape[1]
    depth = ada_w.shape[0]
    q_lora = a_q_lora_g.shape[1]
    kv_lora = a_kv_lora_g.shape[1]
    conv_width = b_conv_w.shape[2]
    t_lat = batch * seq
    t_ctx = batch * ctx_len
    t = t_lat + t_ctx
    bm = ROW_BLOCK
    assert seq % bm == 0 and t_ctx % bm == 0 and t % PEER_TB == 0 and seq % min(ATT_Q_ROWS, seq) == 0
    assert seq & (seq - 1) == 0 and ctx_len & (ctx_len - 1) == 0 and seq % GRID_W == 0
    assert batch + 1 <= V7X_SUBLANES

    blocks_per_batch = seq // bm
    group_of = lambda i: jnp.minimum(i // blocks_per_batch, batch)

    cvec = jnp.zeros((V7X_SUBLANES, d), F32).at[:batch].set(c).at[batch].set(c_ctx)
    mods = _ada(cvec, ada_w, ada_b)
    mods = mods.reshape(depth, V7X_SUBLANES, N_MOD, 1, d).transpose(0, 2, 1, 3, 4)

    cos_a, sin_a = _rope_tables(seq, MLA_ROPE, batch, t_ctx)
    cos_c, sin_c = _rope_tables(seq, GQA_HD, batch, t_ctx)

    u_all = _pack_table(p_u, transpose=False)
    vt_all = _pack_table(p_v, transpose=True)
    xs = jnp.concatenate([x.reshape(t_lat, d), ctx.reshape(t_ctx, d)], axis=0)
    for l in range(depth):
        sh1, sc1, g1, sh2, sc2, g2 = (mods[l, k] for k in range(N_MOD))
        i = l // 2
        need_ctx = l < depth - 1
        if l % 2 == 0:
            w_in_ext, wq_ext, gains = _mla_weights(a_w_in[i], a_w_uq[i], a_q_g[i], a_k_g[i], q_lora, kv_lora)
            p = _normproj(xs, norm1_g[l], sc1, sh1, w_in_ext, group_of, emit_h=False)
            q, k, v = _mla_qkv(p, cos_a, sin_a, wq_ext, a_w_ukv[i].astype(BF16),
                               [a_q_lora_g[i].reshape(1, -1), a_kv_lora_g[i].reshape(1, -1)] + gains,
                               q_lora, kv_lora)
            att, att_ctx = _attention(q, k, v, batch=batch, seq=seq, ctx_len=ctx_len, kv_heads=MLA_HEADS,
                                      group=1, dk=2 * V7X_LANES, dv=MLA_V, need_ctx=need_ctx)
            z_col0 = q_lora + kv_lora + 2 * V7X_LANES
            xs = _mixout(att, att_ctx, e_w_o[i].astype(BF16), xs, g1, group_of,
                         conv=(p, z_col0, b_conv_w[i], b_conv_b[i]), t_lat=t_lat, seq=seq, ctx_len=ctx_len)
        else:
            p = _normproj(xs, norm1_g[l], sc1, sh1, c_w_qkv[i].astype(BF16), group_of, emit_h=False)
            q, k, v = _gqa_qkv(p, cos_c, sin_c, c_q_g[i], c_k_g[i])
            att, att_ctx = _attention(q, k, v, batch=batch, seq=seq, ctx_len=ctx_len, kv_heads=GQA_KV_HEADS,
                                      group=GQA_HEADS // GQA_KV_HEADS, dk=GQA_HD, dv=GQA_HD, need_ctx=need_ctx)
            xs = _mixout(att, att_ctx, c_w_o[i].astype(BF16), xs, g1, group_of,
                         t_lat=t_lat, seq=seq, ctx_len=ctx_len)
        qp, h2 = _normproj(xs, norm2_g[l], sc2, sh2, p_w_q[l].astype(BF16), group_of, emit_h=True)
        r1, p1, n0p0 = _route(qp, p_sub_keys[l].astype(BF16))
        xs = _peer(h2, u_all, vt_all, l, r1, p1, n0p0, xs, g2, group_of)
    return xs.reshape(batch, seq, d)
```

```python
import functools
import math

import jax
import jax.numpy as jnp
from jax import lax
from jax.experimental import pallas as pl
from jax.experimental.pallas import tpu as pltpu

F32 = jnp.float32
BF16 = jnp.bfloat16

EPS = 1e-6
ROPE_THETA = 10000.0
GRID_W = 64
MLA_HEADS = 8
MLA_NOPE = 128
MLA_ROPE = 64
MLA_V = 128
GQA_HEADS = 16
GQA_KV_HEADS = 4
GQA_HD = 128
PEER_HEADS = 8
N_KEYS = 128
PEER_TOPK = 16
N_MOD = 6

V7X_LANES = 128
V7X_SUBLANES = 8
V7X_VMEM_LIMIT_BYTES = 56 * 1024 * 1024

ROW_BLOCK = 512
COL_BLOCKS = (2048, 1024)
ATT_Q_ROWS = 2048
ATT_CHAIN_ROWS = 256
PEER_TB = 512
PEER_EB = 1024
PEER_ROWS = PEER_EB // N_KEYS
SQRT_HALF = 0.7071067811865476
NEG_INF = float("-inf")


def _cparams(sem):
    return pltpu.CompilerParams(dimension_semantics=sem, vmem_limit_bytes=V7X_VMEM_LIMIT_BYTES)


def _nt_dot(a, b):
    return lax.dot_general(a, b, (((1,), (1,)), ((), ())), preferred_element_type=F32)


def _ada_kernel(c_ref, w_ref, b_ref, o_ref):
    c = c_ref[...]
    s = c / (1.0 + jnp.exp(-c))
    o_ref[0] = jnp.dot(s.astype(BF16), w_ref[0].astype(BF16), preferred_element_type=F32) + b_ref[0]


def _ada(cvec, ada_w, ada_b):
    depth, d, n = ada_w.shape
    nb = 1024
    rows = cvec.shape[0]
    return pl.pallas_call(
        _ada_kernel,
        grid=(depth, n // nb),
        in_specs=[
            pl.BlockSpec((rows, d), lambda l, j: (0, 0)),
            pl.BlockSpec((1, d, nb), lambda l, j: (l, 0, j)),
            pl.BlockSpec((1, 1, nb), lambda l, j: (l, 0, j)),
        ],
        out_specs=pl.BlockSpec((1, rows, nb), lambda l, j: (l, 0, j)),
        out_shape=jax.ShapeDtypeStruct((depth, rows, n), F32),
        compiler_params=_cparams(("parallel", "parallel")),
        name="ada_mod",
    )(cvec, ada_w, ada_b.reshape(depth, 1, n))


def _normproj_kernel(x_ref, g_ref, sc_ref, sh_ref, w_ref, o_ref, *rest, emit_h):
    hs_ref = rest[-1]

    @pl.when(pl.program_id(1) == 0)
    def _():
        x = x_ref[...]
        y = x * lax.rsqrt(jnp.mean(x * x, axis=-1, keepdims=True) + EPS) * g_ref[...]
        h = (y * (1.0 + sc_ref[0]) + sh_ref[0]).astype(BF16)
        hs_ref[...] = h
        if emit_h:
            rest[0][...] = pltpu.bitcast(h, jnp.uint32)

    out = jnp.dot(hs_ref[...], w_ref[...], preferred_element_type=F32)
    o_ref[...] = pltpu.bitcast(out.astype(BF16), jnp.uint32) if emit_h else out


def _normproj(x, g, sc, sh, w, group_of, emit_h):
    t, d = x.shape
    n = w.shape[1]
    bm, bn = ROW_BLOCK, next(c for c in COL_BLOCKS if n % c == 0)
    out_shape = [jax.ShapeDtypeStruct((t, n), F32)]
    out_specs = [pl.BlockSpec((bm, bn), lambda i, j: (i, j))]
    if emit_h:
        out_shape = [jax.ShapeDtypeStruct((t // 2, n), jnp.uint32)]
        out_specs = [pl.BlockSpec((bm // 2, bn), lambda i, j: (i, j))]
        out_shape.append(jax.ShapeDtypeStruct((t // 2, d), jnp.uint32))
        out_specs.append(pl.BlockSpec((bm // 2, d), lambda i, j: (i, 0)))
    res = pl.pallas_call(
        functools.partial(_normproj_kernel, emit_h=emit_h),
        grid=(t // bm, n // bn),
        in_specs=[
            pl.BlockSpec((bm, d), lambda i, j: (i, 0)),
            pl.BlockSpec((1, d), lambda i, j: (0, 0)),
            pl.BlockSpec((1, 1, d), lambda i, j: (group_of(i), 0, 0)),
            pl.BlockSpec((1, 1, d), lambda i, j: (group_of(i), 0, 0)),
            pl.BlockSpec((d, bn), lambda i, j: (0, j)),
        ],
        out_specs=out_specs,
        out_shape=out_shape,
        scratch_shapes=[pltpu.VMEM((bm, d), BF16)],
        compiler_params=_cparams(("parallel", "arbitrary")),
        name="normproj_h" if emit_h else "normproj",
    )(x, g.reshape(1, d), sc, sh, w)
    return res if emit_h else res[0]


def _mla_qkv_kernel(p_ref, cos_ref, sin_ref, wuq_ref, wukv_ref, qlg_ref, kvlg_ref,
                    qgn_ref, qgr_ref, qgs_ref, kgn_ref, kgr_ref, kgs_ref,
                    q_ref, k_ref, v_ref, *, q_lora, kv_lora, scale):
    qk_dim = MLA_NOPE + MLA_ROPE
    lanes = V7X_LANES
    cq = p_ref[:, 0:q_lora]
    ckv = p_ref[:, q_lora:q_lora + kv_lora]
    kr = p_ref[:, q_lora + kv_lora:q_lora + kv_lora + lanes]
    krs = p_ref[:, q_lora + kv_lora + lanes:q_lora + kv_lora + 2 * lanes]
    cos = cos_ref[...]
    sin = sin_ref[...]

    cqn = cq * lax.rsqrt(jnp.mean(cq * cq, axis=-1, keepdims=True) + EPS) * qlg_ref[...]
    qraw = jnp.dot(cqn.astype(BF16), wuq_ref[...], preferred_element_type=F32)
    ckvn = ckv * lax.rsqrt(jnp.mean(ckv * ckv, axis=-1, keepdims=True) + EPS) * kvlg_ref[...]
    kvraw = jnp.dot(ckvn.astype(BF16), wukv_ref[...], preferred_element_type=F32)

    kr_rot = kr * kgr_ref[...] * cos + krs * kgs_ref[...] * sin
    kr_ssq = jnp.sum(kr * kr, axis=-1, keepdims=True)
    for h in range(MLA_HEADS):
        nope = qraw[:, h * 3 * lanes:h * 3 * lanes + lanes]
        rope = qraw[:, h * 3 * lanes + lanes:h * 3 * lanes + 2 * lanes]
        rope_sw = qraw[:, h * 3 * lanes + 2 * lanes:h * 3 * lanes + 3 * lanes]
        ssq = jnp.sum(nope * nope, axis=-1, keepdims=True) + jnp.sum(rope * rope, axis=-1, keepdims=True)
        r = lax.rsqrt(ssq * (1.0 / qk_dim) + EPS) * scale
        q_ref[:, h * 2 * lanes:h * 2 * lanes + lanes] = (nope * r * qgn_ref[...]).astype(BF16)
        q_ref[:, h * 2 * lanes + lanes:(h + 1) * 2 * lanes] = (
            (rope * qgr_ref[...] * cos + rope_sw * qgs_ref[...] * sin) * r).astype(BF16)

        k_nope = kvraw[:, h * 2 * lanes:h * 2 * lanes + lanes]
        v = kvraw[:, h * 2 * lanes + lanes:(h + 1) * 2 * lanes]
        kssq = jnp.sum(k_nope * k_nope, axis=-1, keepdims=True) + kr_ssq
        rk = lax.rsqrt(kssq * (1.0 / qk_dim) + EPS)
        k_ref[:, h * 2 * lanes:h * 2 * lanes + lanes] = (k_nope * rk * kgn_ref[...]).astype(BF16)
        k_ref[:, h * 2 * lanes + lanes:(h + 1) * 2 * lanes] = (kr_rot * rk).astype(BF16)
        v_ref[:, h * 2 * lanes:h * 2 * lanes + lanes] = v.astype(BF16)
        v_ref[:, h * 2 * lanes + lanes:(h + 1) * 2 * lanes] = jnp.ones((v.shape[0], lanes), BF16)


def _mla_qkv(p, cos, sin, wuq, wukv, gains, q_lora, kv_lora):
    t = p.shape[0]
    bm = ROW_BLOCK
    lanes = V7X_LANES
    head_cols = q_lora + kv_lora + 2 * lanes
    full = lambda a: pl.BlockSpec(a.shape, lambda i: (0,) * a.ndim)
    return pl.pallas_call(
        functools.partial(_mla_qkv_kernel, q_lora=q_lora, kv_lora=kv_lora,
                          scale=float(MLA_NOPE + MLA_ROPE) ** -0.5),
        grid=(t // bm,),
        in_specs=[
            pl.BlockSpec((bm, head_cols), lambda i: (i, 0)),
            pl.BlockSpec((bm, lanes), lambda i: (i, 0)),
            pl.BlockSpec((bm, lanes), lambda i: (i, 0)),
            full(wuq), full(wukv)] + [full(g) for g in gains],
        out_specs=[
            pl.BlockSpec((bm, MLA_HEADS * 2 * lanes), lambda i: (i, 0)),
            pl.BlockSpec((bm, MLA_HEADS * 2 * lanes), lambda i: (i, 0)),
            pl.BlockSpec((bm, MLA_HEADS * 2 * lanes), lambda i: (i, 0)),
        ],
        out_shape=[
            jax.ShapeDtypeStruct((t, MLA_HEADS * 2 * lanes), BF16),
            jax.ShapeDtypeStruct((t, MLA_HEADS * 2 * lanes), BF16),
            jax.ShapeDtypeStruct((t, MLA_HEADS * 2 * lanes), BF16),
        ],
        compiler_params=_cparams(("parallel",)),
        name="mla_qkv",
    )(p, cos, sin, wuq, wukv, *gains)


def _gqa_qkv_kernel(p_ref, cos_ref, sin_ref, qg_ref, kg_ref, q_ref, k_ref, v_ref, *, scale):
    hd = GQA_HD
    cos = cos_ref[...]
    sin = sin_ref[...]

    def head(x, g):
        y = x * lax.rsqrt(jnp.mean(x * x, axis=-1, keepdims=True) + EPS) * g
        return y * cos + pltpu.roll(y, hd // 2, 1) * sin

    for h in range(GQA_HEADS):
        q_ref[:, h * hd:(h + 1) * hd] = (head(p_ref[:, h * hd:(h + 1) * hd], qg_ref[...]) * scale).astype(BF16)
    k0 = GQA_HEADS * hd
    v0 = k0 + GQA_KV_HEADS * hd
    for h in range(GQA_KV_HEADS):
        k_ref[:, h * hd:(h + 1) * hd] = head(p_ref[:, k0 + h * hd:k0 + (h + 1) * hd], kg_ref[...]).astype(BF16)
    for h in range(GQA_KV_HEADS):
        v_ref[:, h * 2 * hd:h * 2 * hd + hd] = p_ref[:, v0 + h * hd:v0 + (h + 1) * hd].astype(BF16)
        v_ref[:, h * 2 * hd + hd:(h + 1) * 2 * hd] = jnp.ones((v_ref.shape[0], hd), BF16)


def _gqa_qkv(p, cos, sin, qg, kg):
    t, n = p.shape
    bm = ROW_BLOCK
    hd = GQA_HD
    return pl.pallas_call(
        functools.partial(_gqa_qkv_kernel, scale=float(hd) ** -0.5),
        grid=(t // bm,),
        in_specs=[
            pl.BlockSpec((bm, n), lambda i: (i, 0)),
            pl.BlockSpec((bm, hd), lambda i: (i, 0)),
            pl.BlockSpec((bm, hd), lambda i: (i, 0)),
            pl.BlockSpec((1, hd), lambda i: (0, 0)),
            pl.BlockSpec((1, hd), lambda i: (0, 0)),
        ],
        out_specs=[
            pl.BlockSpec((bm, GQA_HEADS * hd), lambda i: (i, 0)),
            pl.BlockSpec((bm, GQA_KV_HEADS * hd), lambda i: (i, 0)),
            pl.BlockSpec((bm, GQA_KV_HEADS * 2 * hd), lambda i: (i, 0)),
        ],
        out_shape=[
            jax.ShapeDtypeStruct((t, GQA_HEADS * hd), BF16),
            jax.ShapeDtypeStruct((t, GQA_KV_HEADS * hd), BF16),
            jax.ShapeDtypeStruct((t, GQA_KV_HEADS * 2 * hd), BF16),
        ],
        compiler_params=_cparams(("parallel",)),
        name="gqa_qkv",
    )(p, cos, sin, qg.reshape(1, hd), kg.reshape(1, hd))


def _attn_kernel(q_ref, kc_ref, vc_ref, *rest, group, dk, dv, with_latent):
    if with_latent:
        kl_ref, vl_ref, o_ref = rest
    else:
        (o_ref,) = rest
    tq = q_ref.shape[0]
    chain_rows = min(ATT_CHAIN_ROWS, group * tq)
    per_head = tq // chain_rows if chain_rows < tq else 0
    for c in range(group * tq // chain_rows):
        if per_head:
            g, part = divmod(c, per_head)
            rows = slice(part * chain_rows, (part + 1) * chain_rows)
            q = q_ref[rows, g * dk:(g + 1) * dk]
            dst = [(rows, g, slice(0, chain_rows))]
        else:
            heads = range(c * chain_rows // tq, (c + 1) * chain_rows // tq)
            q = jnp.concatenate([q_ref[:, g * dk:(g + 1) * dk] for g in heads], axis=0)
            dst = [(slice(0, tq), g, slice(n * tq, (n + 1) * tq)) for n, g in enumerate(heads)]
        sc = _nt_dot(q, kc_ref[...])
        m = jnp.max(sc, axis=-1, keepdims=True)
        if with_latent:
            sl = _nt_dot(q, kl_ref[...])
            m = jnp.maximum(m, jnp.max(sl, axis=-1, keepdims=True))
        o = jnp.dot(jnp.exp((sc - m).astype(BF16)), vc_ref[...], preferred_element_type=F32)
        if with_latent:
            o = o + jnp.dot(jnp.exp((sl - m).astype(BF16)), vl_ref[...], preferred_element_type=F32)
        o = (o[:, :dv] / o[:, dv:dv + 1]).astype(BF16)
        for rows, g, src in dst:
            o_ref[rows, g * dv:(g + 1) * dv] = o[src, :]


def _attention(q, k, v, *, batch, seq, ctx_len, kv_heads, group, dk, dv, need_ctx):
    t_lat = batch * seq
    tq = min(ATT_Q_ROWS // group, seq)
    nq = seq // tq
    ctx_blk0 = t_lat // ctx_len
    common = dict(group=group, dk=dk, dv=dv)
    lat = pl.pallas_call(
        functools.partial(_attn_kernel, with_latent=True, **common),
        grid=(batch, kv_heads, nq),
        in_specs=[
            pl.BlockSpec((tq, group * dk), lambda b, h, j: (b * nq + j, h)),
            pl.BlockSpec((ctx_len, dk), lambda b, h, j: (ctx_blk0 + b, h)),
            pl.BlockSpec((ctx_len, 2 * dv), lambda b, h, j: (ctx_blk0 + b, h)),
            pl.BlockSpec((seq, dk), lambda b, h, j: (b, h)),
            pl.BlockSpec((seq, 2 * dv), lambda b, h, j: (b, h)),
        ],
        out_specs=pl.BlockSpec((tq, group * dv), lambda b, h, j: (b * nq + j, h)),
        out_shape=jax.ShapeDtypeStruct((t_lat, kv_heads * group * dv), BF16),
        compiler_params=_cparams(("parallel", "parallel", "arbitrary")),
        name="attn_latent",
    )(q, k, v, k, v)
    if not need_ctx:
        return lat, None
    ctx = pl.pallas_call(
        functools.partial(_attn_kernel, with_latent=False, **common),
        grid=(batch, kv_heads),
        in_specs=[
            pl.BlockSpec((ctx_len, group * dk), lambda b, h: (ctx_blk0 + b, h)),
            pl.BlockSpec((ctx_len, dk), lambda b, h: (ctx_blk0 + b, h)),
            pl.BlockSpec((ctx_len, 2 * dv), lambda b, h: (ctx_blk0 + b, h)),
        ],
        out_specs=pl.BlockSpec((ctx_len, group * dv), lambda b, h: (b, h)),
        out_shape=jax.ShapeDtypeStruct((batch * ctx_len, kv_heads * group * dv), BF16),
        compiler_params=_cparams(("parallel", "parallel")),
        name="attn_ctx",
    )(q, k, v)
    return lat, ctx


def _mixout_kernel(*refs, with_conv, with_ctx, t_lat, seq, ctx_len, a_width):
    a_ref, refs = refs[0], refs[1:]
    actx_ref = None
    if with_ctx:
        actx_ref, refs = refs[0], refs[1:]
    if with_conv:
        (bg_ref, cg_ref, hz_ref, cgp_ref, hzp_ref, cgn_ref, hzn_ref, cw_ref, cb_ref,
         w_ref, x_ref, gate_ref, o_ref, cs_ref) = refs
    else:
        w_ref, x_ref, gate_ref, o_ref = refs
    i = pl.program_id(0)

    if with_conv:
        @pl.when(pl.program_id(1) == 0)
        def _():
            bm = cg_ref.shape[0]
            u = cg_ref[...] * hz_ref[...]
            u_before = cgp_ref[V7X_SUBLANES - 1:V7X_SUBLANES, :] * hzp_ref[V7X_SUBLANES - 1:V7X_SUBLANES, :]
            u_after = cgn_ref[0:1, :] * hzn_ref[0:1, :]
            local = lax.broadcasted_iota(jnp.int32, (bm, 1), 0)
            row = local + i * bm
            in_lat = row < t_lat
            seg_pos = jnp.where(in_lat, jnp.bitwise_and(row, seq - 1), jnp.bitwise_and(row - t_lat, ctx_len - 1))
            seg_len = jnp.where(in_lat, seq, ctx_len)
            up = jnp.where(local == 0, u_before, pltpu.roll(u, 1, 0))
            up = jnp.where(seg_pos == 0, 0.0, up)
            un = jnp.where(local == bm - 1, u_after, pltpu.roll(u, bm - 1, 0))
            un = jnp.where(seg_pos == seg_len - 1, 0.0, un)
            y = up * cw_ref[0:1, :] + u * cw_ref[1:2, :] + un * cw_ref[2:3, :] + cb_ref[...]
            cs_ref[...] = (bg_ref[...] * y).astype(BF16)

    def finish(a):
        acc = jnp.dot(a, w_ref[0:a_width, :], preferred_element_type=F32)
        if with_conv:
            acc = acc + jnp.dot(cs_ref[...], w_ref[a_width:, :], preferred_element_type=F32)
        o_ref[...] = x_ref[...] + gate_ref[0] * acc

    if with_ctx:
        lat_blocks = t_lat // a_ref.shape[0]
        pl.when(i < lat_blocks)(lambda: finish(a_ref[...]))
        pl.when(i >= lat_blocks)(lambda: finish(actx_ref[...]))
    else:
        finish(a_ref[...])


def _mixout(a, a_ctx, w, x, gate, group_of, conv=None, *, t_lat, seq, ctx_len):
    d = x.shape[1]
    t = x.shape[0] if a_ctx is not None else t_lat
    bm, bn = ROW_BLOCK, next(c for c in COL_BLOCKS if d % c == 0)
    a_width = a.shape[1]
    lat_blocks = t_lat // bm
    kw = dict(t_lat=t_lat, seq=seq, ctx_len=ctx_len, a_width=a_width, with_ctx=a_ctx is not None)
    in_specs = [pl.BlockSpec((bm, a_width), lambda i, j: (jnp.minimum(i, lat_blocks - 1), 0))]
    args = [a]
    if a_ctx is not None:
        in_specs.append(pl.BlockSpec((bm, a_width), lambda i, j: (jnp.maximum(i - lat_blocks, 0), 0)))
        args.append(a_ctx)
    scratch = []
    if conv is not None:
        p, z_col0, cw, cb = conv
        cwid = cw.shape[1]
        assert z_col0 % cwid == 0
        zb = z_col0 // cwid
        sub = V7X_SUBLANES
        last_halo = p.shape[0] // sub - 1
        prev_idx = lambda i: jnp.maximum(i * (bm // sub) - 1, 0)
        next_idx = lambda i: jnp.minimum((i + 1) * (bm // sub), last_halo)
        in_specs += [
            pl.BlockSpec((bm, cwid), lambda i, j: (i, zb)),
            pl.BlockSpec((bm, cwid), lambda i, j: (i, zb + 1)),
            pl.BlockSpec((bm, cwid), lambda i, j: (i, zb + 2)),
            pl.BlockSpec((sub, cwid), lambda i, j: (prev_idx(i), zb + 1)),
            pl.BlockSpec((sub, cwid), lambda i, j: (prev_idx(i), zb + 2)),
            pl.BlockSpec((sub, cwid), lambda i, j: (next_idx(i), zb + 1)),
            pl.BlockSpec((sub, cwid), lambda i, j: (next_idx(i), zb + 2)),
            pl.BlockSpec(cw.shape, lambda i, j: (0, 0)),
            pl.BlockSpec((1, cwid), lambda i, j: (0, 0)),
        ]
        args += [p] * 7 + [cw, cb.reshape(1, cwid)]
        scratch = [pltpu.VMEM((bm, cwid), BF16)]
    in_specs += [
        pl.BlockSpec((w.shape[0], bn), lambda i, j: (0, j)),
        pl.BlockSpec((bm, bn), lambda i, j: (i, j)),
        pl.BlockSpec((1, 1, bn), lambda i, j: (group_of(i), 0, j)),
    ]
    args += [w, x, gate]
    return pl.pallas_call(
        functools.partial(_mixout_kernel, with_conv=conv is not None, **kw),
        grid=(t // bm, d // bn),
        in_specs=in_specs,
        out_specs=pl.BlockSpec((bm, bn), lambda i, j: (i, j)),
        out_shape=jax.ShapeDtypeStruct((t, d), F32),
        scratch_shapes=scratch,
        compiler_params=_cparams(("parallel", "arbitrary")),
        name="mixout_conv" if conv is not None else "mixout",
    )(*args)


_CAND_PAIRS = [(a, b) for a in range(PEER_TOPK) for b in range(PEER_TOPK // (a + 1))]
_CAND_ROWS = -(-len(_CAND_PAIRS) // V7X_SUBLANES) * V7X_SUBLANES
BF16_ROWS = 2 * V7X_SUBLANES


def _oddeven_sort_pairs(n):
    pairs = []

    def merge(lo, hi, r):
        step = r * 2
        if step < hi - lo:
            merge(lo, hi, step)
            merge(lo + r, hi, step)
            pairs.extend((i, i + r) for i in range(lo + r, hi - r, step))
        else:
            pairs.append((lo, lo + r))

    def sort(lo, hi):
        if hi - lo >= 1:
            mid = lo + (hi - lo) // 2
            sort(lo, mid)
            sort(mid + 1, hi)
            merge(lo, hi, 1)

    sort(0, n - 1)
    return pairs


_SORT_PAIRS = _oddeven_sort_pairs(PEER_TOPK)


def _top_sorted(s):
    sub = V7X_SUBLANES
    k = PEER_TOPK
    assert s.shape[0] == sub * k
    v = [s[r * sub:(r + 1) * sub, :] for r in range(k)]

    def exchange(i, j):
        v[i], v[j] = jnp.maximum(v[i], v[j]), jnp.minimum(v[i], v[j])

    for i, j in _SORT_PAIRS:
        exchange(i, j)
    shift = sub // 2
    while shift:
        v = [jnp.maximum(v[r], pltpu.roll(v[k - 1 - r], shift, 0)) for r in range(k)]
        dist = k // 2
        while dist:
            for i in range(k):
                if not i & dist:
                    exchange(i, i + dist)
            dist //= 2
        shift //= 2
    return v


def _bf16_pair_words(v):
    bits = pltpu.bitcast(v.astype(BF16).astype(F32), jnp.uint32)
    return bits | (bits >> 16)


def _route_kernel(q_ref, sk_ref, r1_ref, p1_ref, np_ref, cand_ref):
    nk = N_KEYS
    k_top = PEER_TOPK
    lanes = V7X_LANES
    q = pltpu.bitcast(q_ref[...], BF16)
    tb = q.shape[0]
    cand_ref[len(_CAND_PAIRS):, :] = jnp.full((_CAND_ROWS - len(_CAND_PAIRS), lanes), NEG_INF, F32)
    for c in range(tb // lanes):
        cs = slice(c * lanes, (c + 1) * lanes)
        s0, s1 = (_nt_dot(sk_ref[0, half], q[cs, half * nk:(half + 1) * nk])
                  for half in range(2))

        tops0 = [v[0:1, :] for v in _top_sorted(s0)]
        tops1 = [v[0:1, :] for v in _top_sorted(s1)]
        rank = jnp.full((nk, lanes), float(k_top), F32)
        for b in reversed(range(k_top)):
            rank = jnp.where(s1 >= tops1[b], float(b), rank)

        for r, (a, b) in enumerate(_CAND_PAIRS):
            cand_ref[r:r + 1, :] = tops0[a] + tops1[b]
        cand = cand_ref[...]
        top = tops0[0] + tops1[0]
        z = jnp.zeros((1, lanes), F32)
        for k in range(k_top):
            tau = jnp.max(cand, axis=0, keepdims=True)
            z = z + jnp.exp(tau - top)
            cand = jnp.where(cand == tau, NEG_INF, cand)

        n0 = jnp.zeros((nk, lanes), F32)
        for b in range(k_top):
            n0 = jnp.where(s0 + tops1[b] >= tau, float(b + 1), n0)
        n0 = _bf16_pair_words(n0)
        p0 = _bf16_pair_words(jnp.exp(s0 - tops0[0]))
        for g in range(nk // PEER_ROWS):
            np_ref[0, g, 0:PEER_ROWS, cs] = n0[g * PEER_ROWS:(g + 1) * PEER_ROWS, :]
            np_ref[0, g, PEER_ROWS:, cs] = p0[g * PEER_ROWS:(g + 1) * PEER_ROWS, :]
        rank_b = rank.astype(BF16)
        p1_b = (jnp.exp(s1 - tops1[0]) / z).astype(BF16)
        for g in range(nk // BF16_ROWS):
            r1_ref[0, g, :, cs] = pltpu.bitcast(rank_b[g * BF16_ROWS:(g + 1) * BF16_ROWS, :], jnp.int32)
            p1_ref[0, g, :, cs] = pltpu.bitcast(p1_b[g * BF16_ROWS:(g + 1) * BF16_ROWS, :], jnp.int32)


def _route(qp, sub_keys):
    t = 2 * qp.shape[0]
    tb = PEER_TB
    nk = N_KEYS
    heads = PEER_HEADS
    packed = jax.ShapeDtypeStruct((heads, nk // BF16_ROWS, V7X_SUBLANES, t), jnp.int32)
    packed_spec = pl.BlockSpec((1, nk // BF16_ROWS, V7X_SUBLANES, tb), lambda i, h: (h, 0, 0, i))
    plain = jax.ShapeDtypeStruct((heads, nk // PEER_ROWS, 2 * PEER_ROWS, t), jnp.uint32)
    plain_spec = pl.BlockSpec((1, nk // PEER_ROWS, 2 * PEER_ROWS, tb), lambda i, h: (h, 0, 0, i))
    return pl.pallas_call(
        _route_kernel,
        grid=(t // tb, heads),
        in_specs=[
            pl.BlockSpec((tb // 2, 2 * nk), lambda i, h: (i, h)),
            pl.BlockSpec((1, 2, nk, sub_keys.shape[-1]), lambda i, h: (h, 0, 0, 0)),
        ],
        out_specs=[packed_spec, packed_spec, plain_spec],
        out_shape=[packed, packed, plain],
        scratch_shapes=[pltpu.VMEM((_CAND_ROWS, V7X_LANES), F32)],
        compiler_params=_cparams(("parallel", "arbitrary")),
        name="peer_route",
    )(qp, sub_keys)


def _peer_kernel(h_ref, u_ref, vt_ref, r1_ref, p1_ref, np_ref, x_ref, gate_ref,
                 o_ref, acc_ref, at0_ref, at1_ref, hs_ref, *, n_blocks):
    j = pl.program_id(1)
    nk = N_KEYS
    lanes = V7X_LANES
    eb, tb = at0_ref.shape
    at_refs = (at0_ref, at1_ref)
    jm = jnp.maximum(j - 1, 0)

    def key_row(hd, row, cs):
        words = jnp.broadcast_to(np_ref[hd, jm, row:row + 1, cs], (V7X_SUBLANES, lanes))
        return pltpu.bitcast(words, BF16)[None]

    def gate_rows(il, prev):
        for c in range(tb // lanes):
            cs = slice(c * lanes, (c + 1) * lanes)
            w = None
            for hd in range(PEER_HEADS):
                r1 = pltpu.bitcast(r1_ref[hd, :, :, cs], BF16)
                p1 = pltpu.bitcast(p1_ref[hd, :, :, cs], BF16)
                term = jnp.where(r1 < key_row(hd, il, cs), p1 * key_row(hd, PEER_ROWS + il, cs),
                                 jnp.zeros((), BF16))
                w = term if w is None else w + term
            a = at_refs[prev][il * nk:(il + 1) * nk, cs]
            gelu = (0.5 * a * (1.0 + lax.erf(a * SQRT_HALF))).astype(BF16)
            for g in range(nk // BF16_ROWS):
                r0 = il * nk + g * BF16_ROWS
                hs_ref[r0:r0 + BF16_ROWS, cs] = w[g] * gelu[g * BF16_ROWS:(g + 1) * BF16_ROWS, :]

    def first_matmul(cur):
        at_refs[cur][...] = _nt_dot(pltpu.bitcast(u_ref[0], BF16), pltpu.bitcast(h_ref[...], BF16))

    def mix(prev):
        for il in range(PEER_ROWS):
            gate_rows(il, prev)
        acc_ref[...] += jnp.dot(pltpu.bitcast(vt_ref[0, 0], BF16), hs_ref[...], preferred_element_type=F32)

    @pl.when(j == 0)
    def _():
        acc_ref[...] = jnp.zeros_like(acc_ref)
        first_matmul(0)

    for parity in range(2):
        @pl.when(jnp.logical_and(jnp.logical_and(j > 0, j < n_blocks), j % 2 == parity))
        def _(parity=parity):
            first_matmul(parity)
            mix(1 - parity)

    @pl.when(j == n_blocks)
    def _():
        mix((n_blocks - 1) % 2)
        o_ref[...] = x_ref[...] + gate_ref[0] * acc_ref[...].T


def _peer(h, u, vt, layer, r1, p1, n0p0, x, gate, group_of):
    t, d = x.shape
    tb, eb = PEER_TB, PEER_EB
    n_blocks = vt.shape[1]
    nk = N_KEYS
    heads = PEER_HEADS
    packed_spec = pl.BlockSpec((heads, nk // BF16_ROWS, V7X_SUBLANES, tb), lambda i, j: (0, 0, 0, i))
    plain_spec = pl.BlockSpec((heads, nk // PEER_ROWS, 2 * PEER_ROWS, tb), lambda i, j: (0, 0, 0, i))
    once = dict(pipeline_mode=pl.Buffered(1))
    return pl.pallas_call(
        functools.partial(_peer_kernel, n_blocks=n_blocks),
        grid=(t // tb, n_blocks + 1),
        in_specs=[
            pl.BlockSpec((tb // 2, d), lambda i, j: (i, 0), **once),
            pl.BlockSpec((1, eb // 2, d), lambda i, j: (layer, jnp.minimum(j, n_blocks - 1), 0)),
            pl.BlockSpec((1, 1, d // 2, eb), lambda i, j: (layer, jnp.maximum(j - 1, 0), 0, 0)),
            pl.BlockSpec(packed_spec.block_shape, packed_spec.index_map, **once),
            pl.BlockSpec(packed_spec.block_shape, packed_spec.index_map, **once),
            pl.BlockSpec(plain_spec.block_shape, plain_spec.index_map, **once),
            pl.BlockSpec((tb, d), lambda i, j: (i, 0), **once),
            pl.BlockSpec((1, 1, d), lambda i, j: (group_of(i), 0, 0)),
        ],
        out_specs=pl.BlockSpec((tb, d), lambda i, j: (i, 0)),
        out_shape=jax.ShapeDtypeStruct((t, d), F32),
        scratch_shapes=[
            pltpu.VMEM((d, tb), F32),
            pltpu.VMEM((eb, tb), F32),
            pltpu.VMEM((eb, tb), F32),
            pltpu.VMEM((eb, tb), BF16),
        ],
        compiler_params=_cparams(("parallel", "arbitrary")),
        name="peer_mix",
    )(h, u, vt, r1, p1, n0p0, x, gate)


def _pack_table_kernel(x_ref, o_ref, *, transpose):
    x = x_ref[0]
    if transpose:
        o_ref[0, 0] = pltpu.bitcast(x.T.astype(BF16), jnp.uint32)
    else:
        o_ref[0] = pltpu.bitcast(x.astype(BF16), jnp.uint32)


def _pack_table(tab, transpose):
    layers, ne, d = tab.shape
    eb = PEER_EB
    if transpose:
        out_shape = jax.ShapeDtypeStruct((layers, ne // eb, d // 2, eb), jnp.uint32)
        out_spec = pl.BlockSpec((1, 1, d // 2, eb), lambda l, j: (l, j, 0, 0))
    else:
        out_shape = jax.ShapeDtypeStruct((layers, ne // 2, d), jnp.uint32)
        out_spec = pl.BlockSpec((1, eb // 2, d), lambda l, j: (l, j, 0))
    return pl.pallas_call(
        functools.partial(_pack_table_kernel, transpose=transpose),
        grid=(layers, ne // eb),
        in_specs=[pl.BlockSpec((1, eb, d), lambda l, j: (l, j, 0))],
        out_specs=out_spec,
        out_shape=out_shape,
        compiler_params=_cparams(("parallel", "parallel")),
        name="pack_table_t" if transpose else "pack_table",
    )(tab)


def _rope_tables(seq, rope_dim, batch, ctx_rows):
    rows = seq // GRID_W
    row = jnp.repeat(jnp.arange(rows, dtype=F32), GRID_W)
    col = jnp.tile(jnp.arange(GRID_W, dtype=F32), rows)
    quarter = rope_dim // 4
    freqs = ROPE_THETA ** (-jnp.arange(quarter, dtype=F32) / quarter)
    ang = jnp.concatenate([row[:, None] * freqs, col[:, None] * freqs], axis=-1)
    cos, sin = jnp.cos(ang), jnp.sin(ang)
    pad = V7X_LANES - rope_dim
    cos_l = jnp.pad(jnp.concatenate([cos, cos], axis=-1), ((0, 0), (0, pad)))
    sin_l = jnp.pad(jnp.concatenate([-sin, sin], axis=-1), ((0, 0), (0, pad)))
    cos_c = jnp.pad(jnp.ones((ctx_rows, rope_dim), F32), ((0, 0), (0, pad)))
    sin_c = jnp.zeros((ctx_rows, V7X_LANES), F32)
    return (jnp.concatenate([jnp.tile(cos_l, (batch, 1)), cos_c], axis=0),
            jnp.concatenate([jnp.tile(sin_l, (batch, 1)), sin_c], axis=0))


def _swap_halves(a):
    half = a.shape[-1] // 2
    return jnp.concatenate([a[..., half:], a[..., :half]], axis=-1)


def _pad_lanes(a):
    return jnp.pad(a, [(0, 0)] * (a.ndim - 1) + [(0, V7X_LANES - a.shape[-1])])


def _mla_weights(w_in, w_uq, q_g, k_g, q_lora, kv_lora):
    d = w_in.shape[0]
    c_kr = q_lora + kv_lora
    w_kr = w_in[:, c_kr:c_kr + MLA_ROPE]
    w_in_ext = jnp.concatenate([
        w_in[:, :c_kr], _pad_lanes(w_kr), _pad_lanes(_swap_halves(w_kr)), w_in[:, c_kr + MLA_ROPE:]], axis=1)
    wq = w_uq.reshape(q_lora, MLA_HEADS, MLA_NOPE + MLA_ROPE)
    wq_rope = wq[:, :, MLA_NOPE:]
    wq_ext = jnp.concatenate([wq[:, :, :MLA_NOPE], _pad_lanes(wq_rope), _pad_lanes(_swap_halves(wq_rope))], axis=-1)
    wq_ext = wq_ext.reshape(q_lora, MLA_HEADS * 3 * V7X_LANES)

    def gains(g):
        g_r = g[MLA_NOPE:]
        return [g[:MLA_NOPE].reshape(1, -1), _pad_lanes(g_r).reshape(1, -1), _pad_lanes(_swap_halves(g_r)).reshape(1, -1)]

    return w_in_ext.astype(BF16), wq_ext.astype(BF16), gains(q_g) + gains(k_g)


def kernel(x, c, ctx, c_ctx, ada_w, ada_b, norm1_g, norm2_g, a_w_in, a_q_lora_g, a_kv_lora_g, a_w_uq, a_w_ukv, a_q_g, a_k_g, b_conv_w, b_conv_b, e_w_o, c_w_qkv, c_q_g, c_k_g, c_w_o, p_w_q, p_sub_keys, p_u, p_v):
    batch, seq, d = x.shape
    ctx_len = ctx.shape[1]
    depth = ada_w.shape[0]
    q_lora = a_q_lora_g.shape[1]
    kv_lora = a_kv_lora_g.shape[1]
    conv_width = b_conv_w.shape[2]
    t_lat = batch * seq
    t_ctx = batch * ctx_len
    t = t_lat + t_ctx
    bm = ROW_BLOCK
    assert seq % bm == 0 and t_ctx % bm == 0 and t % PEER_TB == 0 and seq % min(ATT_Q_ROWS, seq) == 0
    assert seq & (seq - 1) == 0 and ctx_len & (ctx_len - 1) == 0 and seq % GRID_W == 0
    assert batch + 1 <= V7X_SUBLANES

    blocks_per_batch = seq // bm
    group_of = lambda i: jnp.minimum(i // blocks_per_batch, batch)

    cvec = jnp.zeros((V7X_SUBLANES, d), F32).at[:batch].set(c).at[batch].set(c_ctx)
    mods = _ada(cvec, ada_w, ada_b)
    mods = mods.reshape(depth, V7X_SUBLANES, N_MOD, 1, d).transpose(0, 2, 1, 3, 4)

    cos_a, sin_a = _rope_tables(seq, MLA_ROPE, batch, t_ctx)
    cos_c, sin_c = _rope_tables(seq, GQA_HD, batch, t_ctx)

    u_all = _pack_table(p_u, transpose=False)
    vt_all = _pack_table(p_v, transpose=True)
    xs = jnp.concatenate([x.reshape(t_lat, d), ctx.reshape(t_ctx, d)], axis=0)
    for l in range(depth):
        sh1, sc1, g1, sh2, sc2, g2 = (mods[l, k] for k in range(N_MOD))
        i = l // 2
        need_ctx = l < depth - 1
        if l % 2 == 0:
            w_in_ext, wq_ext, gains = _mla_weights(a_w_in[i], a_w_uq[i], a_q_g[i], a_k_g[i], q_lora, kv_lora)
            p = _normproj(xs, norm1_g[l], sc1, sh1, w_in_ext, group_of, emit_h=False)
            q, k, v = _mla_qkv(p, cos_a, sin_a, wq_ext, a_w_ukv[i].astype(BF16),
                               [a_q_lora_g[i].reshape(1, -1), a_kv_lora_g[i].reshape(1, -1)] + gains,
                               q_lora, kv_lora)
            att, att_ctx = _attention(q, k, v, batch=batch, seq=seq, ctx_len=ctx_len, kv_heads=MLA_HEADS,
                                      group=1, dk=2 * V7X_LANES, dv=MLA_V, need_ctx=need_ctx)
            z_col0 = q_lora + kv_lora + 2 * V7X_LANES
            xs = _mixout(att, att_ctx, e_w_o[i].astype(BF16), xs, g1, group_of,
                         conv=(p, z_col0, b_conv_w[i], b_conv_b[i]), t_lat=t_lat, seq=seq, ctx_len=ctx_len)
        else:
            p = _normproj(xs, norm1_g[l], sc1, sh1, c_w_qkv[i].astype(BF16), group_of, emit_h=False)
            q, k, v = _gqa_qkv(p, cos_c, sin_c, c_q_g[i], c_k_g[i])
            att, att_ctx = _attention(q, k, v, batch=batch, seq=seq, ctx_len=ctx_len, kv_heads=GQA_KV_HEADS,
                                      group=GQA_HEADS // GQA_KV_HEADS, dk=GQA_HD, dv=GQA_HD, need_ctx=need_ctx)
            xs = _mixout(att, att_ctx, c_w_o[i].astype(BF16), xs, g1, group_of,
                         t_lat=t_lat, seq=seq, ctx_len=ctx_len)
        qp, h2 = _normproj(xs, norm2_g[l], sc2, sh2, p_w_q[l].astype(BF16), group_of, emit_h=True)
        r1, p1, n0p0 = _route(qp, p_sub_keys[l].astype(BF16))
        xs = _peer(h2, u_all, vt_all, l, r1, p1, n0p0, xs, g2, group_of)
    return xs.reshape(batch, seq, d)
```

```python
import functools
import math

import jax
import jax.numpy as jnp
from jax import lax
from jax.experimental import pallas as pl
from jax.experimental.pallas import tpu as pltpu

F32 = jnp.float32
BF16 = jnp.bfloat16

EPS = 1e-6
ROPE_THETA = 10000.0
GRID_W = 64
MLA_HEADS = 8
MLA_NOPE = 128
MLA_ROPE = 64
MLA_V = 128
GQA_HEADS = 16
GQA_KV_HEADS = 4
GQA_HD = 128
PEER_HEADS = 8
N_KEYS = 128
PEER_TOPK = 16
N_MOD = 6

V7X_LANES = 128
V7X_SUBLANES = 8
V7X_VMEM_LIMIT_BYTES = 56 * 1024 * 1024

ROW_BLOCK = 512
MAX_COL_BLOCK = 4096
ATT_Q_ROWS = 2048
ATT_CHAIN_ROWS = 256
PEER_TB = 512
PEER_EB = 1024
PEER_ROWS = PEER_EB // N_KEYS
SQRT_HALF = 0.7071067811865476
NEG_INF = float("-inf")


def _cparams(sem):
    return pltpu.CompilerParams(dimension_semantics=sem, vmem_limit_bytes=V7X_VMEM_LIMIT_BYTES)


def _col_block(n):
    bn = n
    while bn > MAX_COL_BLOCK:
        assert bn % 2 == 0
        bn //= 2
    return bn


def _weight_spec(k, bn, n):
    if bn == n:
        return pl.BlockSpec((k, bn), lambda i, j: (0, 0), pipeline_mode=pl.Buffered(1))
    return pl.BlockSpec((k, bn), lambda i, j: (0, j))


def _nt_dot(a, b):
    return lax.dot_general(a, b, (((1,), (1,)), ((), ())), preferred_element_type=F32)


def _ada_kernel(c_ref, w_ref, b_ref, o_ref):
    c = c_ref[...]
    s = c / (1.0 + jnp.exp(-c))
    o_ref[0] = jnp.dot(s.astype(BF16), w_ref[0].astype(BF16), preferred_element_type=F32) + b_ref[0]


def _ada(cvec, ada_w, ada_b):
    depth, d, n = ada_w.shape
    nb = 2048
    rows = cvec.shape[0]
    return pl.pallas_call(
        _ada_kernel,
        grid=(depth, n // nb),
        in_specs=[
            pl.BlockSpec((rows, d), lambda l, j: (0, 0)),
            pl.BlockSpec((1, d, nb), lambda l, j: (l, 0, j)),
            pl.BlockSpec((1, 1, nb), lambda l, j: (l, 0, j)),
        ],
        out_specs=pl.BlockSpec((1, rows, nb), lambda l, j: (l, 0, j)),
        out_shape=jax.ShapeDtypeStruct((depth, rows, n), F32),
        compiler_params=_cparams(("parallel", "parallel")),
        name="ada_mod",
    )(cvec, ada_w, ada_b.reshape(depth, 1, n))


def _normproj_kernel(x_ref, g_ref, sc_ref, sh_ref, w_ref, o_ref, *rest, emit_h):
    hs_ref = rest[-1]

    @pl.when(pl.program_id(1) == 0)
    def _():
        x = x_ref[...]
        y = x * lax.rsqrt(jnp.mean(x * x, axis=-1, keepdims=True) + EPS) * g_ref[...]
        h = (y * (1.0 + sc_ref[0]) + sh_ref[0]).astype(BF16)
        hs_ref[...] = h
        if emit_h:
            rest[0][...] = pltpu.bitcast(h, jnp.uint32)

    out = jnp.dot(hs_ref[...], w_ref[...], preferred_element_type=F32)
    o_ref[...] = pltpu.bitcast(out.astype(BF16), jnp.uint32) if emit_h else out


def _normproj(x, g, sc, sh, w, group_of, emit_h):
    t, d = x.shape
    n = w.shape[1]
    bm, bn = ROW_BLOCK, _col_block(n)
    out_shape = [jax.ShapeDtypeStruct((t, n), F32)]
    out_specs = [pl.BlockSpec((bm, bn), lambda i, j: (i, j))]
    if emit_h:
        out_shape = [jax.ShapeDtypeStruct((t // 2, n), jnp.uint32)]
        out_specs = [pl.BlockSpec((bm // 2, bn), lambda i, j: (i, j))]
        out_shape.append(jax.ShapeDtypeStruct((t // 2, d), jnp.uint32))
        out_specs.append(pl.BlockSpec((bm // 2, d), lambda i, j: (i, 0)))
    res = pl.pallas_call(
        functools.partial(_normproj_kernel, emit_h=emit_h),
        grid=(t // bm, n // bn),
        in_specs=[
            pl.BlockSpec((bm, d), lambda i, j: (i, 0)),
            pl.BlockSpec((1, d), lambda i, j: (0, 0)),
            pl.BlockSpec((1, 1, d), lambda i, j: (group_of(i), 0, 0)),
            pl.BlockSpec((1, 1, d), lambda i, j: (group_of(i), 0, 0)),
            _weight_spec(d, bn, n),
        ],
        out_specs=out_specs,
        out_shape=out_shape,
        scratch_shapes=[pltpu.VMEM((bm, d), BF16)],
        compiler_params=_cparams(("parallel", "arbitrary")),
        name="normproj_h" if emit_h else "normproj",
    )(x, g.reshape(1, d), sc, sh, w)
    return res if emit_h else res[0]


def _mla_qkv_kernel(p_ref, cos_ref, sin_ref, wuq_ref, wukv_ref, qlg_ref, kvlg_ref,
                    qgn_ref, qgr_ref, qgs_ref, kgn_ref, kgr_ref, kgs_ref,
                    q_ref, k_ref, v_ref, *, q_lora, kv_lora, scale):
    qk_dim = MLA_NOPE + MLA_ROPE
    lanes = V7X_LANES
    cq = p_ref[:, 0:q_lora]
    ckv = p_ref[:, q_lora:q_lora + kv_lora]
    kr = p_ref[:, q_lora + kv_lora:q_lora + kv_lora + lanes]
    krs = p_ref[:, q_lora + kv_lora + lanes:q_lora + kv_lora + 2 * lanes]
    cos = cos_ref[...]
    sin = sin_ref[...]

    cqn = cq * lax.rsqrt(jnp.mean(cq * cq, axis=-1, keepdims=True) + EPS) * qlg_ref[...]
    qraw = jnp.dot(cqn.astype(BF16), wuq_ref[...], preferred_element_type=F32)
    ckvn = ckv * lax.rsqrt(jnp.mean(ckv * ckv, axis=-1, keepdims=True) + EPS) * kvlg_ref[...]
    kvraw = jnp.dot(ckvn.astype(BF16), wukv_ref[...], preferred_element_type=F32)

    kr_rot = kr * kgr_ref[...] * cos + krs * kgs_ref[...] * sin
    kr_ssq = jnp.sum(kr * kr, axis=-1, keepdims=True)
    for h in range(MLA_HEADS):
        nope = qraw[:, h * 3 * lanes:h * 3 * lanes + lanes]
        rope = qraw[:, h * 3 * lanes + lanes:h * 3 * lanes + 2 * lanes]
        rope_sw = qraw[:, h * 3 * lanes + 2 * lanes:h * 3 * lanes + 3 * lanes]
        ssq = jnp.sum(nope * nope, axis=-1, keepdims=True) + jnp.sum(rope * rope, axis=-1, keepdims=True)
        r = lax.rsqrt(ssq * (1.0 / qk_dim) + EPS) * scale
        q_ref[:, h * 2 * lanes:h * 2 * lanes + lanes] = (nope * r * qgn_ref[...]).astype(BF16)
        q_ref[:, h * 2 * lanes + lanes:(h + 1) * 2 * lanes] = (
            (rope * qgr_ref[...] * cos + rope_sw * qgs_ref[...] * sin) * r).astype(BF16)

        k_nope = kvraw[:, h * 2 * lanes:h * 2 * lanes + lanes]
        v = kvraw[:, h * 2 * lanes + lanes:(h + 1) * 2 * lanes]
        kssq = jnp.sum(k_nope * k_nope, axis=-1, keepdims=True) + kr_ssq
        rk = lax.rsqrt(kssq * (1.0 / qk_dim) + EPS)
        k_ref[:, h * 2 * lanes:h * 2 * lanes + lanes] = (k_nope * rk * kgn_ref[...]).astype(BF16)
        k_ref[:, h * 2 * lanes + lanes:(h + 1) * 2 * lanes] = (kr_rot * rk).astype(BF16)
        v_ref[:, h * 2 * lanes:h * 2 * lanes + lanes] = v.astype(BF16)
        v_ref[:, h * 2 * lanes + lanes:(h + 1) * 2 * lanes] = jnp.ones((v.shape[0], lanes), BF16)


def _mla_qkv(p, cos, sin, wuq, wukv, gains, q_lora, kv_lora):
    t = p.shape[0]
    bm = ROW_BLOCK
    lanes = V7X_LANES
    head_cols = q_lora + kv_lora + 2 * lanes
    full = lambda a: pl.BlockSpec(a.shape, lambda i: (0,) * a.ndim)
    return pl.pallas_call(
        functools.partial(_mla_qkv_kernel, q_lora=q_lora, kv_lora=kv_lora,
                          scale=float(MLA_NOPE + MLA_ROPE) ** -0.5),
        grid=(t // bm,),
        in_specs=[
            pl.BlockSpec((bm, head_cols), lambda i: (i, 0)),
            pl.BlockSpec((bm, lanes), lambda i: (i, 0)),
            pl.BlockSpec((bm, lanes), lambda i: (i, 0)),
            full(wuq), full(wukv)] + [full(g) for g in gains],
        out_specs=[
            pl.BlockSpec((bm, MLA_HEADS * 2 * lanes), lambda i: (i, 0)),
            pl.BlockSpec((bm, MLA_HEADS * 2 * lanes), lambda i: (i, 0)),
            pl.BlockSpec((bm, MLA_HEADS * 2 * lanes), lambda i: (i, 0)),
        ],
        out_shape=[
            jax.ShapeDtypeStruct((t, MLA_HEADS * 2 * lanes), BF16),
            jax.ShapeDtypeStruct((t, MLA_HEADS * 2 * lanes), BF16),
            jax.ShapeDtypeStruct((t, MLA_HEADS * 2 * lanes), BF16),
        ],
        compiler_params=_cparams(("parallel",)),
        name="mla_qkv",
    )(p, cos, sin, wuq, wukv, *gains)


def _gqa_qkv_kernel(p_ref, cos_ref, sin_ref, qg_ref, kg_ref, q_ref, k_ref, v_ref, *, scale):
    hd = GQA_HD
    cos = cos_ref[...]
    sin = sin_ref[...]

    def head(x, g):
        y = x * lax.rsqrt(jnp.mean(x * x, axis=-1, keepdims=True) + EPS) * g
        return y * cos + pltpu.roll(y, hd // 2, 1) * sin

    for h in range(GQA_HEADS):
        q_ref[:, h * hd:(h + 1) * hd] = (head(p_ref[:, h * hd:(h + 1) * hd], qg_ref[...]) * scale).astype(BF16)
    k0 = GQA_HEADS * hd
    v0 = k0 + GQA_KV_HEADS * hd
    for h in range(GQA_KV_HEADS):
        k_ref[:, h * hd:(h + 1) * hd] = head(p_ref[:, k0 + h * hd:k0 + (h + 1) * hd], kg_ref[...]).astype(BF16)
    for h in range(GQA_KV_HEADS):
        v_ref[:, h * 2 * hd:h * 2 * hd + hd] = p_ref[:, v0 + h * hd:v0 + (h + 1) * hd].astype(BF16)
        v_ref[:, h * 2 * hd + hd:(h + 1) * 2 * hd] = jnp.ones((v_ref.shape[0], hd), BF16)


def _gqa_qkv(p, cos, sin, qg, kg):
    t, n = p.shape
    bm = ROW_BLOCK
    hd = GQA_HD
    return pl.pallas_call(
        functools.partial(_gqa_qkv_kernel, scale=float(hd) ** -0.5),
        grid=(t // bm,),
        in_specs=[
            pl.BlockSpec((bm, n), lambda i: (i, 0)),
            pl.BlockSpec((bm, hd), lambda i: (i, 0)),
            pl.BlockSpec((bm, hd), lambda i: (i, 0)),
            pl.BlockSpec((1, hd), lambda i: (0, 0)),
            pl.BlockSpec((1, hd), lambda i: (0, 0)),
        ],
        out_specs=[
            pl.BlockSpec((bm, GQA_HEADS * hd), lambda i: (i, 0)),
            pl.BlockSpec((bm, GQA_KV_HEADS * hd), lambda i: (i, 0)),
            pl.BlockSpec((bm, GQA_KV_HEADS * 2 * hd), lambda i: (i, 0)),
        ],
        out_shape=[
            jax.ShapeDtypeStruct((t, GQA_HEADS * hd), BF16),
            jax.ShapeDtypeStruct((t, GQA_KV_HEADS * hd), BF16),
            jax.ShapeDtypeStruct((t, GQA_KV_HEADS * 2 * hd), BF16),
        ],
        compiler_params=_cparams(("parallel",)),
        name="gqa_qkv",
    )(p, cos, sin, qg.reshape(1, hd), kg.reshape(1, hd))


def _attn_kernel(q_ref, kc_ref, vc_ref, *rest, group, dk, dv, with_latent):
    if with_latent:
        kl_ref, vl_ref, o_ref = rest
    else:
        (o_ref,) = rest
    tq = q_ref.shape[0]
    chain_rows = min(ATT_CHAIN_ROWS, group * tq)
    per_head = tq // chain_rows if chain_rows < tq else 0
    for c in range(group * tq // chain_rows):
        if per_head:
            g, part = divmod(c, per_head)
            rows = slice(part * chain_rows, (part + 1) * chain_rows)
            q = q_ref[rows, g * dk:(g + 1) * dk]
            dst = [(rows, g, slice(0, chain_rows))]
        else:
            heads = range(c * chain_rows // tq, (c + 1) * chain_rows // tq)
            q = jnp.concatenate([q_ref[:, g * dk:(g + 1) * dk] for g in heads], axis=0)
            dst = [(slice(0, tq), g, slice(n * tq, (n + 1) * tq)) for n, g in enumerate(heads)]
        sc = _nt_dot(q, kc_ref[...])
        m = jnp.max(sc, axis=-1, keepdims=True)
        if with_latent:
            sl = _nt_dot(q, kl_ref[...])
            m = jnp.maximum(m, jnp.max(sl, axis=-1, keepdims=True))
        o = jnp.dot(jnp.exp((sc - m).astype(BF16)), vc_ref[...], preferred_element_type=F32)
        if with_latent:
            o = o + jnp.dot(jnp.exp((sl - m).astype(BF16)), vl_ref[...], preferred_element_type=F32)
        o = (o[:, :dv] / o[:, dv:dv + 1]).astype(BF16)
        for rows, g, src in dst:
            o_ref[rows, g * dv:(g + 1) * dv] = o[src, :]


def _attention(q, k, v, *, batch, seq, ctx_len, kv_heads, group, dk, dv, need_ctx):
    t_lat = batch * seq
    tq = min(ATT_Q_ROWS // group, seq)
    nq = seq // tq
    ctx_blk0 = t_lat // ctx_len
    common = dict(group=group, dk=dk, dv=dv)
    lat = pl.pallas_call(
        functools.partial(_attn_kernel, with_latent=True, **common),
        grid=(batch, kv_heads, nq),
        in_specs=[
            pl.BlockSpec((tq, group * dk), lambda b, h, j: (b * nq + j, h)),
            pl.BlockSpec((ctx_len, dk), lambda b, h, j: (ctx_blk0 + b, h)),
            pl.BlockSpec((ctx_len, 2 * dv), lambda b, h, j: (ctx_blk0 + b, h)),
            pl.BlockSpec((seq, dk), lambda b, h, j: (b, h)),
            pl.BlockSpec((seq, 2 * dv), lambda b, h, j: (b, h)),
        ],
        out_specs=pl.BlockSpec((tq, group * dv), lambda b, h, j: (b * nq + j, h)),
        out_shape=jax.ShapeDtypeStruct((t_lat, kv_heads * group * dv), BF16),
        compiler_params=_cparams(("parallel", "parallel", "arbitrary")),
        name="attn_latent",
    )(q, k, v, k, v)
    if not need_ctx:
        return lat, None
    ctx = pl.pallas_call(
        functools.partial(_attn_kernel, with_latent=False, **common),
        grid=(batch, kv_heads),
        in_specs=[
            pl.BlockSpec((ctx_len, group * dk), lambda b, h: (ctx_blk0 + b, h)),
            pl.BlockSpec((ctx_len, dk), lambda b, h: (ctx_blk0 + b, h)),
            pl.BlockSpec((ctx_len, 2 * dv), lambda b, h: (ctx_blk0 + b, h)),
        ],
        out_specs=pl.BlockSpec((ctx_len, group * dv), lambda b, h: (b, h)),
        out_shape=jax.ShapeDtypeStruct((batch * ctx_len, kv_heads * group * dv), BF16),
        compiler_params=_cparams(("parallel", "parallel")),
        name="attn_ctx",
    )(q, k, v)
    return lat, ctx


def _mixout_kernel(*refs, with_conv, with_ctx, t_lat, seq, ctx_len, a_width):
    a_ref, refs = refs[0], refs[1:]
    actx_ref = None
    if with_ctx:
        actx_ref, refs = refs[0], refs[1:]
    if with_conv:
        (bg_ref, cg_ref, hz_ref, cgp_ref, hzp_ref, cgn_ref, hzn_ref, cw_ref, cb_ref,
         w_ref, x_ref, gate_ref, o_ref, cs_ref) = refs
    else:
        w_ref, x_ref, gate_ref, o_ref = refs
    i = pl.program_id(0)

    if with_conv:
        @pl.when(pl.program_id(1) == 0)
        def _():
            bm = cg_ref.shape[0]
            u = cg_ref[...] * hz_ref[...]
            u_before = cgp_ref[V7X_SUBLANES - 1:V7X_SUBLANES, :] * hzp_ref[V7X_SUBLANES - 1:V7X_SUBLANES, :]
            u_after = cgn_ref[0:1, :] * hzn_ref[0:1, :]
            local = lax.broadcasted_iota(jnp.int32, (bm, 1), 0)
            row = local + i * bm
            in_lat = row < t_lat
            seg_pos = jnp.where(in_lat, jnp.bitwise_and(row, seq - 1), jnp.bitwise_and(row - t_lat, ctx_len - 1))
            seg_len = jnp.where(in_lat, seq, ctx_len)
            up = jnp.where(local == 0, u_before, pltpu.roll(u, 1, 0))
            up = jnp.where(seg_pos == 0, 0.0, up)
            un = jnp.where(local == bm - 1, u_after, pltpu.roll(u, bm - 1, 0))
            un = jnp.where(seg_pos == seg_len - 1, 0.0, un)
            y = up * cw_ref[0:1, :] + u * cw_ref[1:2, :] + un * cw_ref[2:3, :] + cb_ref[...]
            cs_ref[...] = (bg_ref[...] * y).astype(BF16)

    def finish(a):
        acc = jnp.dot(a, w_ref[0:a_width, :], preferred_element_type=F32)
        if with_conv:
            acc = acc + jnp.dot(cs_ref[...], w_ref[a_width:, :], preferred_element_type=F32)
        o_ref[...] = x_ref[...] + gate_ref[0] * acc

    if with_ctx:
        lat_blocks = t_lat // a_ref.shape[0]
        pl.when(i < lat_blocks)(lambda: finish(a_ref[...]))
        pl.when(i >= lat_blocks)(lambda: finish(actx_ref[...]))
    else:
        finish(a_ref[...])


def _mixout(a, a_ctx, w, x, gate, group_of, conv=None, *, t_lat, seq, ctx_len):
    d = x.shape[1]
    t = x.shape[0] if a_ctx is not None else t_lat
    bm, bn = ROW_BLOCK, _col_block(d)
    a_width = a.shape[1]
    lat_blocks = t_lat // bm
    kw = dict(t_lat=t_lat, seq=seq, ctx_len=ctx_len, a_width=a_width, with_ctx=a_ctx is not None)
    in_specs = [pl.BlockSpec((bm, a_width), lambda i, j: (jnp.minimum(i, lat_blocks - 1), 0))]
    args = [a]
    if a_ctx is not None:
        in_specs.append(pl.BlockSpec((bm, a_width), lambda i, j: (jnp.maximum(i - lat_blocks, 0), 0)))
        args.append(a_ctx)
    scratch = []
    if conv is not None:
        p, z_col0, cw, cb = conv
        cwid = cw.shape[1]
        assert z_col0 % cwid == 0
        zb = z_col0 // cwid
        sub = V7X_SUBLANES
        last_halo = p.shape[0] // sub - 1
        prev_idx = lambda i: jnp.maximum(i * (bm // sub) - 1, 0)
        next_idx = lambda i: jnp.minimum((i + 1) * (bm // sub), last_halo)
        in_specs += [
            pl.BlockSpec((bm, cwid), lambda i, j: (i, zb)),
            pl.BlockSpec((bm, cwid), lambda i, j: (i, zb + 1)),
            pl.BlockSpec((bm, cwid), lambda i, j: (i, zb + 2)),
            pl.BlockSpec((sub, cwid), lambda i, j: (prev_idx(i), zb + 1)),
            pl.BlockSpec((sub, cwid), lambda i, j: (prev_idx(i), zb + 2)),
            pl.BlockSpec((sub, cwid), lambda i, j: (next_idx(i), zb + 1)),
            pl.BlockSpec((sub, cwid), lambda i, j: (next_idx(i), zb + 2)),
            pl.BlockSpec(cw.shape, lambda i, j: (0, 0)),
            pl.BlockSpec((1, cwid), lambda i, j: (0, 0)),
        ]
        args += [p] * 7 + [cw, cb.reshape(1, cwid)]
        scratch = [pltpu.VMEM((bm, cwid), BF16)]
    in_specs += [
        _weight_spec(w.shape[0], bn, d),
        pl.BlockSpec((bm, bn), lambda i, j: (i, j)),
        pl.BlockSpec((1, 1, bn), lambda i, j: (group_of(i), 0, j)),
    ]
    args += [w, x, gate]
    return pl.pallas_call(
        functools.partial(_mixout_kernel, with_conv=conv is not None, **kw),
        grid=(t // bm, d // bn),
        in_specs=in_specs,
        out_specs=pl.BlockSpec((bm, bn), lambda i, j: (i, j)),
        out_shape=jax.ShapeDtypeStruct((t, d), F32),
        scratch_shapes=scratch,
        compiler_params=_cparams(("parallel", "arbitrary")),
        name="mixout_conv" if conv is not None else "mixout",
    )(*args)


_CAND_PAIRS = [(a, b) for a in range(PEER_TOPK) for b in range(PEER_TOPK // (a + 1))]
_CAND_ROWS = -(-len(_CAND_PAIRS) // V7X_SUBLANES) * V7X_SUBLANES
BF16_ROWS = 2 * V7X_SUBLANES


def _oddeven_sort_pairs(n):
    pairs = []

    def merge(lo, hi, r):
        step = r * 2
        if step < hi - lo:
            merge(lo, hi, step)
            merge(lo + r, hi, step)
            pairs.extend((i, i + r) for i in range(lo + r, hi - r, step))
        else:
            pairs.append((lo, lo + r))

    def sort(lo, hi):
        if hi - lo >= 1:
            mid = lo + (hi - lo) // 2
            sort(lo, mid)
            sort(mid + 1, hi)
            merge(lo, hi, 1)

    sort(0, n - 1)
    return pairs


_SORT_PAIRS = _oddeven_sort_pairs(PEER_TOPK)


def _top_sorted(s):
    sub = V7X_SUBLANES
    k = PEER_TOPK
    assert s.shape[0] == sub * k
    v = [s[r * sub:(r + 1) * sub, :] for r in range(k)]

    def exchange(i, j):
        v[i], v[j] = jnp.maximum(v[i], v[j]), jnp.minimum(v[i], v[j])

    for i, j in _SORT_PAIRS:
        exchange(i, j)
    shift = sub // 2
    while shift:
        v = [jnp.maximum(v[r], pltpu.roll(v[k - 1 - r], shift, 0)) for r in range(k)]
        dist = k // 2
        while dist:
            for i in range(k):
                if not i & dist:
                    exchange(i, i + dist)
            dist //= 2
        shift //= 2
    return v


def _bf16_pair_words(v):
    bits = pltpu.bitcast(v.astype(BF16).astype(F32), jnp.uint32)
    return bits | (bits >> 16)


def _route_kernel(q_ref, sk_ref, r1_ref, p1_ref, np_ref, cand_ref):
    nk = N_KEYS
    k_top = PEER_TOPK
    lanes = V7X_LANES
    q = pltpu.bitcast(q_ref[...], BF16)
    tb = q.shape[0]
    cand_ref[len(_CAND_PAIRS):, :] = jnp.full((_CAND_ROWS - len(_CAND_PAIRS), lanes), NEG_INF, F32)
    for c in range(tb // lanes):
        cs = slice(c * lanes, (c + 1) * lanes)
        s0, s1 = (_nt_dot(sk_ref[0, half], q[cs, half * nk:(half + 1) * nk])
                  for half in range(2))

        tops0 = [v[0:1, :] for v in _top_sorted(s0)]
        tops1 = [v[0:1, :] for v in _top_sorted(s1)]
        rank = jnp.full((nk, lanes), float(k_top), F32)
        for b in reversed(range(k_top)):
            rank = jnp.where(s1 >= tops1[b], float(b), rank)

        for r, (a, b) in enumerate(_CAND_PAIRS):
            cand_ref[r:r + 1, :] = tops0[a] + tops1[b]
        cand = cand_ref[...]
        top = tops0[0] + tops1[0]
        z = jnp.zeros((1, lanes), F32)
        for k in range(k_top):
            tau = jnp.max(cand, axis=0, keepdims=True)
            z = z + jnp.exp(tau - top)
            cand = jnp.where(cand == tau, NEG_INF, cand)

        n0 = jnp.zeros((nk, lanes), F32)
        for b in range(k_top):
            n0 = jnp.where(s0 + tops1[b] >= tau, float(b + 1), n0)
        n0 = _bf16_pair_words(n0)
        p0 = _bf16_pair_words(jnp.exp(s0 - tops0[0]))
        for g in range(nk // PEER_ROWS):
            np_ref[0, g, 0:PEER_ROWS, cs] = n0[g * PEER_ROWS:(g + 1) * PEER_ROWS, :]
            np_ref[0, g, PEER_ROWS:, cs] = p0[g * PEER_ROWS:(g + 1) * PEER_ROWS, :]
        rank_b = rank.astype(BF16)
        p1_b = (jnp.exp(s1 - tops1[0]) / z).astype(BF16)
        for g in range(nk // BF16_ROWS):
            r1_ref[0, g, :, cs] = pltpu.bitcast(rank_b[g * BF16_ROWS:(g + 1) * BF16_ROWS, :], jnp.int32)
            p1_ref[0, g, :, cs] = pltpu.bitcast(p1_b[g * BF16_ROWS:(g + 1) * BF16_ROWS, :], jnp.int32)


def _route(qp, sub_keys):
    t = 2 * qp.shape[0]
    tb = PEER_TB
    nk = N_KEYS
    heads = PEER_HEADS
    packed = jax.ShapeDtypeStruct((heads, nk // BF16_ROWS, V7X_SUBLANES, t), jnp.int32)
    packed_spec = pl.BlockSpec((1, nk // BF16_ROWS, V7X_SUBLANES, tb), lambda i, h: (h, 0, 0, i))
    plain = jax.ShapeDtypeStruct((heads, nk // PEER_ROWS, 2 * PEER_ROWS, t), jnp.uint32)
    plain_spec = pl.BlockSpec((1, nk // PEER_ROWS, 2 * PEER_ROWS, tb), lambda i, h: (h, 0, 0, i))
    return pl.pallas_call(
        _route_kernel,
        grid=(t // tb, heads),
        in_specs=[
            pl.BlockSpec((tb // 2, 2 * nk), lambda i, h: (i, h)),
            pl.BlockSpec((1, 2, nk, sub_keys.shape[-1]), lambda i, h: (h, 0, 0, 0)),
        ],
        out_specs=[packed_spec, packed_spec, plain_spec],
        out_shape=[packed, packed, plain],
        scratch_shapes=[pltpu.VMEM((_CAND_ROWS, V7X_LANES), F32)],
        compiler_params=_cparams(("parallel", "arbitrary")),
        name="peer_route",
    )(qp, sub_keys)


def _peer_kernel(h_ref, u_ref, vt_ref, r1_ref, p1_ref, np_ref, x_ref, gate_ref,
                 o_ref, acc_ref, at0_ref, at1_ref, hs_ref, *, n_blocks):
    j = pl.program_id(1)
    nk = N_KEYS
    lanes = V7X_LANES
    eb, tb = at0_ref.shape
    at_refs = (at0_ref, at1_ref)
    jm = jnp.maximum(j - 1, 0)

    def key_row(hd, row, cs):
        words = jnp.broadcast_to(np_ref[hd, jm, row:row + 1, cs], (V7X_SUBLANES, lanes))
        return pltpu.bitcast(words, BF16)[None]

    def gate_rows(il, prev):
        for c in range(tb // lanes):
            cs = slice(c * lanes, (c + 1) * lanes)
            w = None
            for hd in range(PEER_HEADS):
                r1 = pltpu.bitcast(r1_ref[hd, :, :, cs], BF16)
                p1 = pltpu.bitcast(p1_ref[hd, :, :, cs], BF16)
                term = jnp.where(r1 < key_row(hd, il, cs), p1 * key_row(hd, PEER_ROWS + il, cs),
                                 jnp.zeros((), BF16))
                w = term if w is None else w + term
            a = at_refs[prev][il * nk:(il + 1) * nk, cs]
            gelu = (0.5 * a * (1.0 + lax.erf(a * SQRT_HALF))).astype(BF16)
            for g in range(nk // BF16_ROWS):
                r0 = il * nk + g * BF16_ROWS
                hs_ref[r0:r0 + BF16_ROWS, cs] = w[g] * gelu[g * BF16_ROWS:(g + 1) * BF16_ROWS, :]

    def first_matmul(cur):
        at_refs[cur][...] = _nt_dot(pltpu.bitcast(u_ref[0], BF16), pltpu.bitcast(h_ref[...], BF16))

    def mix(prev):
        for il in range(PEER_ROWS):
            gate_rows(il, prev)
        acc_ref[...] += jnp.dot(pltpu.bitcast(vt_ref[0, 0], BF16), hs_ref[...], preferred_element_type=F32)

    @pl.when(j == 0)
    def _():
        acc_ref[...] = jnp.zeros_like(acc_ref)
        first_matmul(0)

    for parity in range(2):
        @pl.when(jnp.logical_and(jnp.logical_and(j > 0, j < n_blocks), j % 2 == parity))
        def _(parity=parity):
            first_matmul(parity)
            mix(1 - parity)

    @pl.when(j == n_blocks)
    def _():
        mix((n_blocks - 1) % 2)
        o_ref[...] = x_ref[...] + gate_ref[0] * acc_ref[...].T


def _peer(h, u, vt, layer, r1, p1, n0p0, x, gate, group_of):
    t, d = x.shape
    tb, eb = PEER_TB, PEER_EB
    n_blocks = vt.shape[1]
    nk = N_KEYS
    heads = PEER_HEADS
    packed_spec = pl.BlockSpec((heads, nk // BF16_ROWS, V7X_SUBLANES, tb), lambda i, j: (0, 0, 0, i))
    plain_spec = pl.BlockSpec((heads, nk // PEER_ROWS, 2 * PEER_ROWS, tb), lambda i, j: (0, 0, 0, i))
    once = dict(pipeline_mode=pl.Buffered(1))
    return pl.pallas_call(
        functools.partial(_peer_kernel, n_blocks=n_blocks),
        grid=(t // tb, n_blocks + 1),
        in_specs=[
            pl.BlockSpec((tb // 2, d), lambda i, j: (i, 0), **once),
            pl.BlockSpec((1, eb // 2, d), lambda i, j: (layer, jnp.minimum(j, n_blocks - 1), 0)),
            pl.BlockSpec((1, 1, d // 2, eb), lambda i, j: (layer, jnp.maximum(j - 1, 0), 0, 0)),
            pl.BlockSpec(packed_spec.block_shape, packed_spec.index_map, **once),
            pl.BlockSpec(packed_spec.block_shape, packed_spec.index_map, **once),
            pl.BlockSpec(plain_spec.block_shape, plain_spec.index_map, **once),
            pl.BlockSpec((tb, d), lambda i, j: (i, 0), **once),
            pl.BlockSpec((1, 1, d), lambda i, j: (group_of(i), 0, 0)),
        ],
        out_specs=pl.BlockSpec((tb, d), lambda i, j: (i, 0)),
        out_shape=jax.ShapeDtypeStruct((t, d), F32),
        scratch_shapes=[
            pltpu.VMEM((d, tb), F32),
            pltpu.VMEM((eb, tb), F32),
            pltpu.VMEM((eb, tb), F32),
            pltpu.VMEM((eb, tb), BF16),
        ],
        compiler_params=_cparams(("parallel", "arbitrary")),
        name="peer_mix",
    )(h, u, vt, r1, p1, n0p0, x, gate)


def _pack_table_kernel(x_ref, o_ref, *, transpose):
    x = x_ref[0]
    if transpose:
        o_ref[0, 0] = pltpu.bitcast(x.T.astype(BF16), jnp.uint32)
    else:
        o_ref[0] = pltpu.bitcast(x.astype(BF16), jnp.uint32)


def _pack_table(tab, transpose):
    layers, ne, d = tab.shape
    eb = PEER_EB
    if transpose:
        out_shape = jax.ShapeDtypeStruct((layers, ne // eb, d // 2, eb), jnp.uint32)
        out_spec = pl.BlockSpec((1, 1, d // 2, eb), lambda l, j: (l, j, 0, 0))
    else:
        out_shape = jax.ShapeDtypeStruct((layers, ne // 2, d), jnp.uint32)
        out_spec = pl.BlockSpec((1, eb // 2, d), lambda l, j: (l, j, 0))
    return pl.pallas_call(
        functools.partial(_pack_table_kernel, transpose=transpose),
        grid=(layers, ne // eb),
        in_specs=[pl.BlockSpec((1, eb, d), lambda l, j: (l, j, 0))],
        out_specs=out_spec,
        out_shape=out_shape,
        compiler_params=_cparams(("parallel", "parallel")),
        name="pack_table_t" if transpose else "pack_table",
    )(tab)


def _rope_tables(seq, rope_dim, batch, ctx_rows):
    rows = seq // GRID_W
    row = jnp.repeat(jnp.arange(rows, dtype=F32), GRID_W)
    col = jnp.tile(jnp.arange(GRID_W, dtype=F32), rows)
    quarter = rope_dim // 4
    freqs = ROPE_THETA ** (-jnp.arange(quarter, dtype=F32) / quarter)
    ang = jnp.concatenate([row[:, None] * freqs, col[:, None] * freqs], axis=-1)
    cos, sin = jnp.cos(ang), jnp.sin(ang)
    pad = V7X_LANES - rope_dim
    cos_l = jnp.pad(jnp.concatenate([cos, cos], axis=-1), ((0, 0), (0, pad)))
    sin_l = jnp.pad(jnp.concatenate([-sin, sin], axis=-1), ((0, 0), (0, pad)))
    cos_c = jnp.pad(jnp.ones((ctx_rows, rope_dim), F32), ((0, 0), (0, pad)))
    sin_c = jnp.zeros((ctx_rows, V7X_LANES), F32)
    return (jnp.concatenate([jnp.tile(cos_l, (batch, 1)), cos_c], axis=0),
            jnp.concatenate([jnp.tile(sin_l, (batch, 1)), sin_c], axis=0))


def _swap_halves(a):
    half = a.shape[-1] // 2
    return jnp.concatenate([a[..., half:], a[..., :half]], axis=-1)


def _pad_lanes(a):
    return jnp.pad(a, [(0, 0)] * (a.ndim - 1) + [(0, V7X_LANES - a.shape[-1])])


def _mla_weights(w_in, w_uq, q_g, k_g, q_lora, kv_lora):
    d = w_in.shape[0]
    c_kr = q_lora + kv_lora
    w_kr = w_in[:, c_kr:c_kr + MLA_ROPE]
    w_in_ext = jnp.concatenate([
        w_in[:, :c_kr], _pad_lanes(w_kr), _pad_lanes(_swap_halves(w_kr)), w_in[:, c_kr + MLA_ROPE:]], axis=1)
    wq = w_uq.reshape(q_lora, MLA_HEADS, MLA_NOPE + MLA_ROPE)
    wq_rope = wq[:, :, MLA_NOPE:]
    wq_ext = jnp.concatenate([wq[:, :, :MLA_NOPE], _pad_lanes(wq_rope), _pad_lanes(_swap_halves(wq_rope))], axis=-1)
    wq_ext = wq_ext.reshape(q_lora, MLA_HEADS * 3 * V7X_LANES)

    def gains(g):
        g_r = g[MLA_NOPE:]
        return [g[:MLA_NOPE].reshape(1, -1), _pad_lanes(g_r).reshape(1, -1), _pad_lanes(_swap_halves(g_r)).reshape(1, -1)]

    return w_in_ext.astype(BF16), wq_ext.astype(BF16), gains(q_g) + gains(k_g)


def kernel(x, c, ctx, c_ctx, ada_w, ada_b, norm1_g, norm2_g, a_w_in, a_q_lora_g, a_kv_lora_g, a_w_uq, a_w_ukv, a_q_g, a_k_g, b_conv_w, b_conv_b, e_w_o, c_w_qkv, c_q_g, c_k_g, c_w_o, p_w_q, p_sub_keys, p_u, p_v):
    batch, seq, d = x.shape
    ctx_len = ctx.shape[1]
    depth = ada_w.shape[0]
    q_lora = a_q_lora_g.shape[1]
    kv_lora = a_kv_lora_g.shape[1]
    conv_width = b_conv_w.shape[2]
    t_lat = batch * seq
    t_ctx = batch * ctx_len
    t = t_lat + t_ctx
    bm = ROW_BLOCK
    assert seq % bm == 0 and t_ctx % bm == 0 and t % PEER_TB == 0 and seq % min(ATT_Q_ROWS, seq) == 0
    assert seq & (seq - 1) == 0 and ctx_len & (ctx_len - 1) == 0 and seq % GRID_W == 0
    assert batch + 1 <= V7X_SUBLANES

    blocks_per_batch = seq // bm
    group_of = lambda i: jnp.minimum(i // blocks_per_batch, batch)

    cvec = jnp.zeros((V7X_SUBLANES, d), F32).at[:batch].set(c).at[batch].set(c_ctx)
    mods = _ada(cvec, ada_w, ada_b)
    mods = mods.reshape(depth, V7X_SUBLANES, N_MOD, 1, d).transpose(0, 2, 1, 3, 4)

    cos_a, sin_a = _rope_tables(seq, MLA_ROPE, batch, t_ctx)
    cos_c, sin_c = _rope_tables(seq, GQA_HD, batch, t_ctx)

    u_all = _pack_table(p_u, transpose=False)
    vt_all = _pack_table(p_v, transpose=True)
    xs = jnp.concatenate([x.reshape(t_lat, d), ctx.reshape(t_ctx, d)], axis=0)
    for l in range(depth):
        sh1, sc1, g1, sh2, sc2, g2 = (mods[l, k] for k in range(N_MOD))
        i = l // 2
        need_ctx = l < depth - 1
        if l % 2 == 0:
            w_in_ext, wq_ext, gains = _mla_weights(a_w_in[i], a_w_uq[i], a_q_g[i], a_k_g[i], q_lora, kv_lora)
            p = _normproj(xs, norm1_g[l], sc1, sh1, w_in_ext, group_of, emit_h=False)
            q, k, v = _mla_qkv(p, cos_a, sin_a, wq_ext, a_w_ukv[i].astype(BF16),
                               [a_q_lora_g[i].reshape(1, -1), a_kv_lora_g[i].reshape(1, -1)] + gains,
                               q_lora, kv_lora)
            att, att_ctx = _attention(q, k, v, batch=batch, seq=seq, ctx_len=ctx_len, kv_heads=MLA_HEADS,
                                      group=1, dk=2 * V7X_LANES, dv=MLA_V, need_ctx=need_ctx)
            z_col0 = q_lora + kv_lora + 2 * V7X_LANES
            xs = _mixout(att, att_ctx, e_w_o[i].astype(BF16), xs, g1, group_of,
                         conv=(p, z_col0, b_conv_w[i], b_conv_b[i]), t_lat=t_lat, seq=seq, ctx_len=ctx_len)
        else:
            p = _normproj(xs, norm1_g[l], sc1, sh1, c_w_qkv[i].astype(BF16), group_of, emit_h=False)
            q, k, v = _gqa_qkv(p, cos_c, sin_c, c_q_g[i], c_k_g[i])
            att, att_ctx = _attention(q, k, v, batch=batch, seq=seq, ctx_len=ctx_len, kv_heads=GQA_KV_HEADS,
                                      group=GQA_HEADS // GQA_KV_HEADS, dk=GQA_HD, dv=GQA_HD, need_ctx=need_ctx)
            xs = _mixout(att, att_ctx, c_w_o[i].astype(BF16), xs, g1, group_of,
                         t_lat=t_lat, seq=seq, ctx_len=ctx_len)
        qp, h2 = _normproj(xs, norm2_g[l], sc2, sh2, p_w_q[l].astype(BF16), group_of, emit_h=True)
        r1, p1, n0p0 = _route(qp, p_sub_keys[l].astype(BF16))
        xs = _peer(h2, u_all, vt_all, l, r1, p1, n0p0, xs, g2, group_of)
    return xs.reshape(batch, seq, d)
```

```python
import functools
import math

import jax
import jax.numpy as jnp
from jax import lax
from jax.experimental import pallas as pl
from jax.experimental.pallas import tpu as pltpu

F32 = jnp.float32
BF16 = jnp.bfloat16

EPS = 1e-6
ROPE_THETA = 10000.0
GRID_W = 64
MLA_HEADS = 8
MLA_NOPE = 128
MLA_ROPE = 64
MLA_V = 128
GQA_HEADS = 16
GQA_KV_HEADS = 4
GQA_HD = 128
PEER_HEADS = 8
N_KEYS = 128
PEER_TOPK = 16
N_MOD = 6

V7X_LANES = 128
V7X_SUBLANES = 8
V7X_VMEM_LIMIT_BYTES = 56 * 1024 * 1024

ROW_BLOCK = 512
MAX_COL_BLOCK = 4096
ATT_Q_ROWS = 2048
ATT_CHAIN_ROWS = 256
PEER_TB = 512
PEER_EB = 1024
PEER_ROWS = PEER_EB // N_KEYS
SQRT_HALF = 0.7071067811865476
NEG_INF = float("-inf")


def _cparams(sem):
    return pltpu.CompilerParams(dimension_semantics=sem, vmem_limit_bytes=V7X_VMEM_LIMIT_BYTES)


def _col_block(n):
    bn = n
    while bn > MAX_COL_BLOCK:
        assert bn % 2 == 0
        bn //= 2
    return bn


def _weight_spec(k, bn, n):
    if bn == n:
        return pl.BlockSpec((k, bn), lambda i, j: (0, 0), pipeline_mode=pl.Buffered(1))
    return pl.BlockSpec((k, bn), lambda i, j: (0, j))


def _nt_dot(a, b):
    return lax.dot_general(a, b, (((1,), (1,)), ((), ())), preferred_element_type=F32)


def _ada_kernel(c_ref, w_ref, b_ref, o_ref):
    c = c_ref[...]
    s = c / (1.0 + jnp.exp(-c))
    o_ref[0] = jnp.dot(s.astype(BF16), w_ref[0].astype(BF16), preferred_element_type=F32) + b_ref[0]


def _ada(cvec, ada_w, ada_b):
    depth, d, n = ada_w.shape
    nb = 1024
    rows = cvec.shape[0]
    return pl.pallas_call(
        _ada_kernel,
        grid=(depth, n // nb),
        in_specs=[
            pl.BlockSpec((rows, d), lambda l, j: (0, 0)),
            pl.BlockSpec((1, d, nb), lambda l, j: (l, 0, j)),
            pl.BlockSpec((1, 1, nb), lambda l, j: (l, 0, j)),
        ],
        out_specs=pl.BlockSpec((1, rows, nb), lambda l, j: (l, 0, j)),
        out_shape=jax.ShapeDtypeStruct((depth, rows, n), F32),
        compiler_params=_cparams(("parallel", "parallel")),
        name="ada_mod",
    )(cvec, ada_w, ada_b.reshape(depth, 1, n))


def _normproj_kernel(x_ref, g_ref, sc_ref, sh_ref, w_ref, o_ref, *rest, emit_h):
    hs_ref = rest[-1]

    @pl.when(pl.program_id(1) == 0)
    def _():
        x = x_ref[...]
        y = x * lax.rsqrt(jnp.mean(x * x, axis=-1, keepdims=True) + EPS) * g_ref[...]
        h = (y * (1.0 + sc_ref[0]) + sh_ref[0]).astype(BF16)
        hs_ref[...] = h
        if emit_h:
            rest[0][...] = pltpu.bitcast(h, jnp.uint32)

    out = jnp.dot(hs_ref[...], w_ref[...], preferred_element_type=F32)
    o_ref[...] = pltpu.bitcast(out.astype(BF16), jnp.uint32) if emit_h else out


def _normproj(x, g, sc, sh, w, group_of, emit_h):
    t, d = x.shape
    n = w.shape[1]
    bm, bn = ROW_BLOCK, _col_block(n)
    out_shape = [jax.ShapeDtypeStruct((t, n), F32)]
    out_specs = [pl.BlockSpec((bm, bn), lambda i, j: (i, j))]
    if emit_h:
        out_shape = [jax.ShapeDtypeStruct((t // 2, n), jnp.uint32)]
        out_specs = [pl.BlockSpec((bm // 2, bn), lambda i, j: (i, j))]
        out_shape.append(jax.ShapeDtypeStruct((t // 2, d), jnp.uint32))
        out_specs.append(pl.BlockSpec((bm // 2, d), lambda i, j: (i, 0)))
    res = pl.pallas_call(
        functools.partial(_normproj_kernel, emit_h=emit_h),
        grid=(t // bm, n // bn),
        in_specs=[
            pl.BlockSpec((bm, d), lambda i, j: (i, 0)),
            pl.BlockSpec((1, d), lambda i, j: (0, 0)),
            pl.BlockSpec((1, 1, d), lambda i, j: (group_of(i), 0, 0)),
            pl.BlockSpec((1, 1, d), lambda i, j: (group_of(i), 0, 0)),
            _weight_spec(d, bn, n),
        ],
        out_specs=out_specs,
        out_shape=out_shape,
        scratch_shapes=[pltpu.VMEM((bm, d), BF16)],
        compiler_params=_cparams(("parallel", "arbitrary")),
        name="normproj_h" if emit_h else "normproj",
    )(x, g.reshape(1, d), sc, sh, w)
    return res if emit_h else res[0]


def _mla_qkv_kernel(p_ref, cos_ref, sin_ref, wuq_ref, wukv_ref, qlg_ref, kvlg_ref,
                    qgn_ref, qgr_ref, qgs_ref, kgn_ref, kgr_ref, kgs_ref,
                    q_ref, k_ref, v_ref, *, q_lora, kv_lora, scale):
    qk_dim = MLA_NOPE + MLA_ROPE
    lanes = V7X_LANES
    cq = p_ref[:, 0:q_lora]
    ckv = p_ref[:, q_lora:q_lora + kv_lora]
    kr = p_ref[:, q_lora + kv_lora:q_lora + kv_lora + lanes]
    krs = p_ref[:, q_lora + kv_lora + lanes:q_lora + kv_lora + 2 * lanes]
    cos = cos_ref[...]
    sin = sin_ref[...]

    cqn = cq * lax.rsqrt(jnp.mean(cq * cq, axis=-1, keepdims=True) + EPS) * qlg_ref[...]
    qraw = jnp.dot(cqn.astype(BF16), wuq_ref[...], preferred_element_type=F32)
    ckvn = ckv * lax.rsqrt(jnp.mean(ckv * ckv, axis=-1, keepdims=True) + EPS) * kvlg_ref[...]
    kvraw = jnp.dot(ckvn.astype(BF16), wukv_ref[...], preferred_element_type=F32)

    kr_rot = kr * kgr_ref[...] * cos + krs * kgs_ref[...] * sin
    kr_ssq = jnp.sum(kr * kr, axis=-1, keepdims=True)
    for h in range(MLA_HEADS):
        nope = qraw[:, h * 3 * lanes:h * 3 * lanes + lanes]
        rope = qraw[:, h * 3 * lanes + lanes:h * 3 * lanes + 2 * lanes]
        rope_sw = qraw[:, h * 3 * lanes + 2 * lanes:h * 3 * lanes + 3 * lanes]
        ssq = jnp.sum(nope * nope, axis=-1, keepdims=True) + jnp.sum(rope * rope, axis=-1, keepdims=True)
        r = lax.rsqrt(ssq * (1.0 / qk_dim) + EPS) * scale
        q_ref[:, h * 2 * lanes:h * 2 * lanes + lanes] = (nope * r * qgn_ref[...]).astype(BF16)
        q_ref[:, h * 2 * lanes + lanes:(h + 1) * 2 * lanes] = (
            (rope * qgr_ref[...] * cos + rope_sw * qgs_ref[...] * sin) * r).astype(BF16)

        k_nope = kvraw[:, h * 2 * lanes:h * 2 * lanes + lanes]
        v = kvraw[:, h * 2 * lanes + lanes:(h + 1) * 2 * lanes]
        kssq = jnp.sum(k_nope * k_nope, axis=-1, keepdims=True) + kr_ssq
        rk = lax.rsqrt(kssq * (1.0 / qk_dim) + EPS)
        k_ref[:, h * 2 * lanes:h * 2 * lanes + lanes] = (k_nope * rk * kgn_ref[...]).astype(BF16)
        k_ref[:, h * 2 * lanes + lanes:(h + 1) * 2 * lanes] = (kr_rot * rk).astype(BF16)
        v_ref[:, h * 2 * lanes:h * 2 * lanes + lanes] = v.astype(BF16)
        v_ref[:, h * 2 * lanes + lanes:(h + 1) * 2 * lanes] = jnp.ones((v.shape[0], lanes), BF16)


def _mla_qkv(p, cos, sin, wuq, wukv, gains, q_lora, kv_lora):
    t = p.shape[0]
    bm = ROW_BLOCK
    lanes = V7X_LANES
    head_cols = q_lora + kv_lora + 2 * lanes
    full = lambda a: pl.BlockSpec(a.shape, lambda i: (0,) * a.ndim)
    return pl.pallas_call(
        functools.partial(_mla_qkv_kernel, q_lora=q_lora, kv_lora=kv_lora,
                          scale=float(MLA_NOPE + MLA_ROPE) ** -0.5),
        grid=(t // bm,),
        in_specs=[
            pl.BlockSpec((bm, head_cols), lambda i: (i, 0)),
            pl.BlockSpec((bm, lanes), lambda i: (i, 0)),
            pl.BlockSpec((bm, lanes), lambda i: (i, 0)),
            full(wuq), full(wukv)] + [full(g) for g in gains],
        out_specs=[
            pl.BlockSpec((bm, MLA_HEADS * 2 * lanes), lambda i: (i, 0)),
            pl.BlockSpec((bm, MLA_HEADS * 2 * lanes), lambda i: (i, 0)),
            pl.BlockSpec((bm, MLA_HEADS * 2 * lanes), lambda i: (i, 0)),
        ],
        out_shape=[
            jax.ShapeDtypeStruct((t, MLA_HEADS * 2 * lanes), BF16),
            jax.ShapeDtypeStruct((t, MLA_HEADS * 2 * lanes), BF16),
            jax.ShapeDtypeStruct((t, MLA_HEADS * 2 * lanes), BF16),
        ],
        compiler_params=_cparams(("parallel",)),
        name="mla_qkv",
    )(p, cos, sin, wuq, wukv, *gains)


def _gqa_qkv_kernel(p_ref, cos_ref, sin_ref, qg_ref, kg_ref, q_ref, k_ref, v_ref, *, scale):
    hd = GQA_HD
    cos = cos_ref[...]
    sin = sin_ref[...]

    def head(x, g):
        y = x * lax.rsqrt(jnp.mean(x * x, axis=-1, keepdims=True) + EPS) * g
        return y * cos + pltpu.roll(y, hd // 2, 1) * sin

    for h in range(GQA_HEADS):
        q_ref[:, h * hd:(h + 1) * hd] = (head(p_ref[:, h * hd:(h + 1) * hd], qg_ref[...]) * scale).astype(BF16)
    k0 = GQA_HEADS * hd
    v0 = k0 + GQA_KV_HEADS * hd
    for h in range(GQA_KV_HEADS):
        k_ref[:, h * hd:(h + 1) * hd] = head(p_ref[:, k0 + h * hd:k0 + (h + 1) * hd], kg_ref[...]).astype(BF16)
    for h in range(GQA_KV_HEADS):
        v_ref[:, h * 2 * hd:h * 2 * hd + hd] = p_ref[:, v0 + h * hd:v0 + (h + 1) * hd].astype(BF16)
        v_ref[:, h * 2 * hd + hd:(h + 1) * 2 * hd] = jnp.ones((v_ref.shape[0], hd), BF16)


def _gqa_qkv(p, cos, sin, qg, kg):
    t, n = p.shape
    bm = ROW_BLOCK
    hd = GQA_HD
    return pl.pallas_call(
        functools.partial(_gqa_qkv_kernel, scale=float(hd) ** -0.5),
        grid=(t // bm,),
        in_specs=[
            pl.BlockSpec((bm, n), lambda i: (i, 0)),
            pl.BlockSpec((bm, hd), lambda i: (i, 0)),
            pl.BlockSpec((bm, hd), lambda i: (i, 0)),
            pl.BlockSpec((1, hd), lambda i: (0, 0)),
            pl.BlockSpec((1, hd), lambda i: (0, 0)),
        ],
        out_specs=[
            pl.BlockSpec((bm, GQA_HEADS * hd), lambda i: (i, 0)),
            pl.BlockSpec((bm, GQA_KV_HEADS * hd), lambda i: (i, 0)),
            pl.BlockSpec((bm, GQA_KV_HEADS * 2 * hd), lambda i: (i, 0)),
        ],
        out_shape=[
            jax.ShapeDtypeStruct((t, GQA_HEADS * hd), BF16),
            jax.ShapeDtypeStruct((t, GQA_KV_HEADS * hd), BF16),
            jax.ShapeDtypeStruct((t, GQA_KV_HEADS * 2 * hd), BF16),
        ],
        compiler_params=_cparams(("parallel",)),
        name="gqa_qkv",
    )(p, cos, sin, qg.reshape(1, hd), kg.reshape(1, hd))


def _attn_kernel(q_ref, kc_ref, vc_ref, *rest, group, dk, dv, with_latent):
    if with_latent:
        kl_ref, vl_ref, o_ref = rest
    else:
        (o_ref,) = rest
    tq = q_ref.shape[0]
    chain_rows = min(ATT_CHAIN_ROWS, group * tq)
    per_head = tq // chain_rows if chain_rows < tq else 0
    for c in range(group * tq // chain_rows):
        if per_head:
            g, part = divmod(c, per_head)
            rows = slice(part * chain_rows, (part + 1) * chain_rows)
            q = q_ref[rows, g * dk:(g + 1) * dk]
            dst = [(rows, g, slice(0, chain_rows))]
        else:
            heads = range(c * chain_rows // tq, (c + 1) * chain_rows // tq)
            q = jnp.concatenate([q_ref[:, g * dk:(g + 1) * dk] for g in heads], axis=0)
            dst = [(slice(0, tq), g, slice(n * tq, (n + 1) * tq)) for n, g in enumerate(heads)]
        sc = _nt_dot(q, kc_ref[...])
        m = jnp.max(sc, axis=-1, keepdims=True)
        if with_latent:
            sl = _nt_dot(q, kl_ref[...])
            m = jnp.maximum(m, jnp.max(sl, axis=-1, keepdims=True))
        o = jnp.dot(jnp.exp((sc - m).astype(BF16)), vc_ref[...], preferred_element_type=F32)
        if with_latent:
            o = o + jnp.dot(jnp.exp((sl - m).astype(BF16)), vl_ref[...], preferred_element_type=F32)
        o = (o[:, :dv] / o[:, dv:dv + 1]).astype(BF16)
        for rows, g, src in dst:
            o_ref[rows, g * dv:(g + 1) * dv] = o[src, :]


def _attention(q, k, v, *, batch, seq, ctx_len, kv_heads, group, dk, dv, need_ctx):
    t_lat = batch * seq
    tq = min(ATT_Q_ROWS // group, seq)
    nq = seq // tq
    ctx_blk0 = t_lat // ctx_len
    common = dict(group=group, dk=dk, dv=dv)
    lat = pl.pallas_call(
        functools.partial(_attn_kernel, with_latent=True, **common),
        grid=(batch, kv_heads, nq),
        in_specs=[
            pl.BlockSpec((tq, group * dk), lambda b, h, j: (b * nq + j, h)),
            pl.BlockSpec((ctx_len, dk), lambda b, h, j: (ctx_blk0 + b, h)),
            pl.BlockSpec((ctx_len, 2 * dv), lambda b, h, j: (ctx_blk0 + b, h)),
            pl.BlockSpec((seq, dk), lambda b, h, j: (b, h)),
            pl.BlockSpec((seq, 2 * dv), lambda b, h, j: (b, h)),
        ],
        out_specs=pl.BlockSpec((tq, group * dv), lambda b, h, j: (b * nq + j, h)),
        out_shape=jax.ShapeDtypeStruct((t_lat, kv_heads * group * dv), BF16),
        compiler_params=_cparams(("parallel", "parallel", "arbitrary")),
        name="attn_latent",
    )(q, k, v, k, v)
    if not need_ctx:
        return lat, None
    ctx = pl.pallas_call(
        functools.partial(_attn_kernel, with_latent=False, **common),
        grid=(batch, kv_heads),
        in_specs=[
            pl.BlockSpec((ctx_len, group * dk), lambda b, h: (ctx_blk0 + b, h)),
            pl.BlockSpec((ctx_len, dk), lambda b, h: (ctx_blk0 + b, h)),
            pl.BlockSpec((ctx_len, 2 * dv), lambda b, h: (ctx_blk0 + b, h)),
        ],
        out_specs=pl.BlockSpec((ctx_len, group * dv), lambda b, h: (b, h)),
        out_shape=jax.ShapeDtypeStruct((batch * ctx_len, kv_heads * group * dv), BF16),
        compiler_params=_cparams(("parallel", "parallel")),
        name="attn_ctx",
    )(q, k, v)
    return lat, ctx


def _mixout_kernel(*refs, with_conv, with_ctx, t_lat, seq, ctx_len, a_width):
    a_ref, refs = refs[0], refs[1:]
    actx_ref = None
    if with_ctx:
        actx_ref, refs = refs[0], refs[1:]
    if with_conv:
        (bg_ref, cg_ref, hz_ref, cgp_ref, hzp_ref, cgn_ref, hzn_ref, cw_ref, cb_ref,
         w_ref, x_ref, gate_ref, o_ref, cs_ref) = refs
    else:
        w_ref, x_ref, gate_ref, o_ref = refs
    i = pl.program_id(0)

    if with_conv:
        @pl.when(pl.program_id(1) == 0)
        def _():
            bm = cg_ref.shape[0]
            u = cg_ref[...] * hz_ref[...]
            u_before = cgp_ref[V7X_SUBLANES - 1:V7X_SUBLANES, :] * hzp_ref[V7X_SUBLANES - 1:V7X_SUBLANES, :]
            u_after = cgn_ref[0:1, :] * hzn_ref[0:1, :]
            local = lax.broadcasted_iota(jnp.int32, (bm, 1), 0)
            row = local + i * bm
            in_lat = row < t_lat
            seg_pos = jnp.where(in_lat, jnp.bitwise_and(row, seq - 1), jnp.bitwise_and(row - t_lat, ctx_len - 1))
            seg_len = jnp.where(in_lat, seq, ctx_len)
            up = jnp.where(local == 0, u_before, pltpu.roll(u, 1, 0))
            up = jnp.where(seg_pos == 0, 0.0, up)
            un = jnp.where(local == bm - 1, u_after, pltpu.roll(u, bm - 1, 0))
            un = jnp.where(seg_pos == seg_len - 1, 0.0, un)
            y = up * cw_ref[0:1, :] + u * cw_ref[1:2, :] + un * cw_ref[2:3, :] + cb_ref[...]
            cs_ref[...] = (bg_ref[...] * y).astype(BF16)

    def finish(a):
        acc = jnp.dot(a, w_ref[0:a_width, :], preferred_element_type=F32)
        if with_conv:
            acc = acc + jnp.dot(cs_ref[...], w_ref[a_width:, :], preferred_element_type=F32)
        o_ref[...] = x_ref[...] + gate_ref[0] * acc

    if with_ctx:
        lat_blocks = t_lat // a_ref.shape[0]
        pl.when(i < lat_blocks)(lambda: finish(a_ref[...]))
        pl.when(i >= lat_blocks)(lambda: finish(actx_ref[...]))
    else:
        finish(a_ref[...])


def _mixout(a, a_ctx, w, x, gate, group_of, conv=None, *, t_lat, seq, ctx_len):
    d = x.shape[1]
    t = x.shape[0] if a_ctx is not None else t_lat
    bm, bn = ROW_BLOCK, _col_block(d)
    a_width = a.shape[1]
    lat_blocks = t_lat // bm
    kw = dict(t_lat=t_lat, seq=seq, ctx_len=ctx_len, a_width=a_width, with_ctx=a_ctx is not None)
    in_specs = [pl.BlockSpec((bm, a_width), lambda i, j: (jnp.minimum(i, lat_blocks - 1), 0))]
    args = [a]
    if a_ctx is not None:
        in_specs.append(pl.BlockSpec((bm, a_width), lambda i, j: (jnp.maximum(i - lat_blocks, 0), 0)))
        args.append(a_ctx)
    scratch = []
    if conv is not None:
        p, z_col0, cw, cb = conv
        cwid = cw.shape[1]
        assert z_col0 % cwid == 0
        zb = z_col0 // cwid
        sub = V7X_SUBLANES
        last_halo = p.shape[0] // sub - 1
        prev_idx = lambda i: jnp.maximum(i * (bm // sub) - 1, 0)
        next_idx = lambda i: jnp.minimum((i + 1) * (bm // sub), last_halo)
        in_specs += [
            pl.BlockSpec((bm, cwid), lambda i, j: (i, zb)),
            pl.BlockSpec((bm, cwid), lambda i, j: (i, zb + 1)),
            pl.BlockSpec((bm, cwid), lambda i, j: (i, zb + 2)),
            pl.BlockSpec((sub, cwid), lambda i, j: (prev_idx(i), zb + 1)),
            pl.BlockSpec((sub, cwid), lambda i, j: (prev_idx(i), zb + 2)),
            pl.BlockSpec((sub, cwid), lambda i, j: (next_idx(i), zb + 1)),
            pl.BlockSpec((sub, cwid), lambda i, j: (next_idx(i), zb + 2)),
            pl.BlockSpec(cw.shape, lambda i, j: (0, 0)),
            pl.BlockSpec((1, cwid), lambda i, j: (0, 0)),
        ]
        args += [p] * 7 + [cw, cb.reshape(1, cwid)]
        scratch = [pltpu.VMEM((bm, cwid), BF16)]
    in_specs += [
        _weight_spec(w.shape[0], bn, d),
        pl.BlockSpec((bm, bn), lambda i, j: (i, j)),
        pl.BlockSpec((1, 1, bn), lambda i, j: (group_of(i), 0, j)),
    ]
    args += [w, x, gate]
    return pl.pallas_call(
        functools.partial(_mixout_kernel, with_conv=conv is not None, **kw),
        grid=(t // bm, d // bn),
        in_specs=in_specs,
        out_specs=pl.BlockSpec((bm, bn), lambda i, j: (i, j)),
        out_shape=jax.ShapeDtypeStruct((t, d), F32),
        scratch_shapes=scratch,
        compiler_params=_cparams(("parallel", "arbitrary")),
        name="mixout_conv" if conv is not None else "mixout",
    )(*args)


_CAND_PAIRS = [(a, b) for a in range(PEER_TOPK) for b in range(PEER_TOPK // (a + 1))]
_CAND_ROWS = -(-len(_CAND_PAIRS) // V7X_SUBLANES) * V7X_SUBLANES
BF16_ROWS = 2 * V7X_SUBLANES


def _oddeven_sort_pairs(n):
    pairs = []

    def merge(lo, hi, r):
        step = r * 2
        if step < hi - lo:
            merge(lo, hi, step)
            merge(lo + r, hi, step)
            pairs.extend((i, i + r) for i in range(lo + r, hi - r, step))
        else:
            pairs.append((lo, lo + r))

    def sort(lo, hi):
        if hi - lo >= 1:
            mid = lo + (hi - lo) // 2
            sort(lo, mid)
            sort(mid + 1, hi)
            merge(lo, hi, 1)

    sort(0, n - 1)
    return pairs


_SORT_PAIRS = _oddeven_sort_pairs(PEER_TOPK)


def _top_sorted(s):
    sub = V7X_SUBLANES
    k = PEER_TOPK
    assert s.shape[0] == sub * k
    v = [s[r * sub:(r + 1) * sub, :] for r in range(k)]

    def exchange(i, j):
        v[i], v[j] = jnp.maximum(v[i], v[j]), jnp.minimum(v[i], v[j])

    for i, j in _SORT_PAIRS:
        exchange(i, j)
    shift = sub // 2
    while shift:
        v = [jnp.maximum(v[r], pltpu.roll(v[k - 1 - r], shift, 0)) for r in range(k)]
        dist = k // 2
        while dist:
            for i in range(k):
                if not i & dist:
                    exchange(i, i + dist)
            dist //= 2
        shift //= 2
    return v


def _bf16_pair_words(v):
    bits = pltpu.bitcast(v.astype(BF16).astype(F32), jnp.uint32)
    return bits | (bits >> 16)


def _route_kernel(q_ref, sk_ref, r1_ref, p1_ref, np_ref, cand_ref):
    nk = N_KEYS
    k_top = PEER_TOPK
    lanes = V7X_LANES
    q = pltpu.bitcast(q_ref[...], BF16)
    tb = q.shape[0]
    cand_ref[len(_CAND_PAIRS):, :] = jnp.full((_CAND_ROWS - len(_CAND_PAIRS), lanes), NEG_INF, F32)
    for c in range(tb // lanes):
        cs = slice(c * lanes, (c + 1) * lanes)
        s0, s1 = (_nt_dot(sk_ref[0, half], q[cs, half * nk:(half + 1) * nk])
                  for half in range(2))

        tops0 = [v[0:1, :] for v in _top_sorted(s0)]
        tops1 = [v[0:1, :] for v in _top_sorted(s1)]
        rank = jnp.full((nk, lanes), float(k_top), F32)
        for b in reversed(range(k_top)):
            rank = jnp.where(s1 >= tops1[b], float(b), rank)

        for r, (a, b) in enumerate(_CAND_PAIRS):
            cand_ref[r:r + 1, :] = tops0[a] + tops1[b]
        cand = cand_ref[...]
        top = tops0[0] + tops1[0]
        z = jnp.zeros((1, lanes), F32)
        for k in range(k_top):
            tau = jnp.max(cand, axis=0, keepdims=True)
            z = z + jnp.exp(tau - top)
            cand = jnp.where(cand == tau, NEG_INF, cand)

        n0 = jnp.zeros((nk, lanes), F32)
        for b in range(k_top):
            n0 = jnp.where(s0 + tops1[b] >= tau, float(b + 1), n0)
        n0 = _bf16_pair_words(n0)
        p0 = _bf16_pair_words(jnp.exp(s0 - tops0[0]))
        for g in range(nk // PEER_ROWS):
            np_ref[0, g, 0:PEER_ROWS, cs] = n0[g * PEER_ROWS:(g + 1) * PEER_ROWS, :]
            np_ref[0, g, PEER_ROWS:, cs] = p0[g * PEER_ROWS:(g + 1) * PEER_ROWS, :]
        rank_b = rank.astype(BF16)
        p1_b = (jnp.exp(s1 - tops1[0]) / z).astype(BF16)
        for g in range(nk // BF16_ROWS):
            r1_ref[0, g, :, cs] = pltpu.bitcast(rank_b[g * BF16_ROWS:(g + 1) * BF16_ROWS, :], jnp.int32)
            p1_ref[0, g, :, cs] = pltpu.bitcast(p1_b[g * BF16_ROWS:(g + 1) * BF16_ROWS, :], jnp.int32)


def _route(qp, sub_keys):
    t = 2 * qp.shape[0]
    tb = PEER_TB
    nk = N_KEYS
    heads = PEER_HEADS
    packed = jax.ShapeDtypeStruct((heads, nk // BF16_ROWS, V7X_SUBLANES, t), jnp.int32)
    packed_spec = pl.BlockSpec((1, nk // BF16_ROWS, V7X_SUBLANES, tb), lambda i, h: (h, 0, 0, i))
    plain = jax.ShapeDtypeStruct((heads, nk // PEER_ROWS, 2 * PEER_ROWS, t), jnp.uint32)
    plain_spec = pl.BlockSpec((1, nk // PEER_ROWS, 2 * PEER_ROWS, tb), lambda i, h: (h, 0, 0, i))
    return pl.pallas_call(
        _route_kernel,
        grid=(t // tb, heads),
        in_specs=[
            pl.BlockSpec((tb // 2, 2 * nk), lambda i, h: (i, h)),
            pl.BlockSpec((1, 2, nk, sub_keys.shape[-1]), lambda i, h: (h, 0, 0, 0)),
        ],
        out_specs=[packed_spec, packed_spec, plain_spec],
        out_shape=[packed, packed, plain],
        scratch_shapes=[pltpu.VMEM((_CAND_ROWS, V7X_LANES), F32)],
        compiler_params=_cparams(("parallel", "arbitrary")),
        name="peer_route",
    )(qp, sub_keys)


def _peer_kernel(h_ref, ua_ref, ub_ref, vta_ref, vtb_ref, r1_ref, p1_ref, np_ref, x_ref, gate_ref,
                 o_ref, acc_ref, at0_ref, at1_ref, hs_ref, *, n_blocks):
    j = pl.program_id(1)
    nk = N_KEYS
    lanes = V7X_LANES
    eb, tb = at0_ref.shape
    at_refs = (at0_ref, at1_ref)
    jm = jnp.maximum(j - 1, 0)

    def key_row(hd, row, cs):
        words = jnp.broadcast_to(np_ref[hd, jm, row:row + 1, cs], (V7X_SUBLANES, lanes))
        return pltpu.bitcast(words, BF16)[None]

    def gate_rows(il, prev):
        for c in range(tb // lanes):
            cs = slice(c * lanes, (c + 1) * lanes)
            w = None
            for hd in range(PEER_HEADS):
                r1 = pltpu.bitcast(r1_ref[hd, :, :, cs], BF16)
                p1 = pltpu.bitcast(p1_ref[hd, :, :, cs], BF16)
                term = jnp.where(r1 < key_row(hd, il, cs), p1 * key_row(hd, PEER_ROWS + il, cs),
                                 jnp.zeros((), BF16))
                w = term if w is None else w + term
            a = at_refs[prev][il * nk:(il + 1) * nk, cs]
            gelu = (0.5 * a * (1.0 + lax.erf(a * SQRT_HALF))).astype(BF16)
            for g in range(nk // BF16_ROWS):
                r0 = il * nk + g * BF16_ROWS
                hs_ref[r0:r0 + BF16_ROWS, cs] = w[g] * gelu[g * BF16_ROWS:(g + 1) * BF16_ROWS, :]

    def first_matmul(cur):
        h = pltpu.bitcast(h_ref[...], BF16)
        at_refs[cur][:eb // 2, :] = _nt_dot(pltpu.bitcast(ua_ref[0], BF16), h)
        at_refs[cur][eb // 2:, :] = _nt_dot(pltpu.bitcast(ub_ref[0], BF16), h)

    def mix(prev):
        for il in range(PEER_ROWS):
            gate_rows(il, prev)
        half = acc_ref.shape[0] // 2
        acc_ref[:half, :] += jnp.dot(pltpu.bitcast(vta_ref[0, 0], BF16), hs_ref[...], preferred_element_type=F32)
        acc_ref[half:, :] += jnp.dot(pltpu.bitcast(vtb_ref[0, 0], BF16), hs_ref[...], preferred_element_type=F32)

    @pl.when(j == 0)
    def _():
        acc_ref[...] = jnp.zeros_like(acc_ref)
        first_matmul(0)

    for parity in range(2):
        @pl.when(jnp.logical_and(jnp.logical_and(j > 0, j < n_blocks), j % 2 == parity))
        def _(parity=parity):
            first_matmul(parity)
            mix(1 - parity)

    @pl.when(j == n_blocks)
    def _():
        mix((n_blocks - 1) % 2)
        o_ref[...] = x_ref[...] + gate_ref[0] * acc_ref[...].T


def _peer(h, u, vt, layer, r1, p1, n0p0, x, gate, group_of):
    t, d = x.shape
    tb, eb = PEER_TB, PEER_EB
    n_blocks = vt.shape[1]
    nk = N_KEYS
    heads = PEER_HEADS
    packed_spec = pl.BlockSpec((heads, nk // BF16_ROWS, V7X_SUBLANES, tb), lambda i, j: (0, 0, 0, i))
    plain_spec = pl.BlockSpec((heads, nk // PEER_ROWS, 2 * PEER_ROWS, tb), lambda i, j: (0, 0, 0, i))
    once = dict(pipeline_mode=pl.Buffered(1))
    return pl.pallas_call(
        functools.partial(_peer_kernel, n_blocks=n_blocks),
        grid=(t // tb, n_blocks + 1),
        in_specs=[
            pl.BlockSpec((tb // 2, d), lambda i, j: (i, 0), **once),
            pl.BlockSpec((1, eb // 4, d), lambda i, j: (layer, 2 * jnp.minimum(j, n_blocks - 1), 0)),
            pl.BlockSpec((1, eb // 4, d), lambda i, j: (layer, 2 * jnp.minimum(j, n_blocks - 1) + 1, 0)),
            pl.BlockSpec((1, 1, d // 4, eb), lambda i, j: (layer, jnp.maximum(j - 1, 0), 0, 0)),
            pl.BlockSpec((1, 1, d // 4, eb), lambda i, j: (layer, jnp.maximum(j - 1, 0), 1, 0)),
            pl.BlockSpec(packed_spec.block_shape, packed_spec.index_map, **once),
            pl.BlockSpec(packed_spec.block_shape, packed_spec.index_map, **once),
            pl.BlockSpec(plain_spec.block_shape, plain_spec.index_map, **once),
            pl.BlockSpec((tb, d), lambda i, j: (i, 0), **once),
            pl.BlockSpec((1, 1, d), lambda i, j: (group_of(i), 0, 0)),
        ],
        out_specs=pl.BlockSpec((tb, d), lambda i, j: (i, 0)),
        out_shape=jax.ShapeDtypeStruct((t, d), F32),
        scratch_shapes=[
            pltpu.VMEM((d, tb), F32),
            pltpu.VMEM((eb, tb), F32),
            pltpu.VMEM((eb, tb), F32),
            pltpu.VMEM((eb, tb), BF16),
        ],
        compiler_params=_cparams(("parallel", "arbitrary")),
        name="peer_mix",
    )(h, u, u, vt, vt, r1, p1, n0p0, x, gate)


def _pack_table_kernel(x_ref, o_ref, *, transpose):
    x = x_ref[0]
    if transpose:
        o_ref[0, 0] = pltpu.bitcast(x.T.astype(BF16), jnp.uint32)
    else:
        o_ref[0] = pltpu.bitcast(x.astype(BF16), jnp.uint32)


def _pack_table(tab, transpose):
    layers, ne, d = tab.shape
    eb = PEER_EB
    if transpose:
        out_shape = jax.ShapeDtypeStruct((layers, ne // eb, d // 2, eb), jnp.uint32)
        out_spec = pl.BlockSpec((1, 1, d // 2, eb), lambda l, j: (l, j, 0, 0))
    else:
        out_shape = jax.ShapeDtypeStruct((layers, ne // 2, d), jnp.uint32)
        out_spec = pl.BlockSpec((1, eb // 2, d), lambda l, j: (l, j, 0))
    return pl.pallas_call(
        functools.partial(_pack_table_kernel, transpose=transpose),
        grid=(layers, ne // eb),
        in_specs=[pl.BlockSpec((1, eb, d), lambda l, j: (l, j, 0))],
        out_specs=out_spec,
        out_shape=out_shape,
        compiler_params=_cparams(("parallel", "parallel")),
        name="pack_table_t" if transpose else "pack_table",
    )(tab)


def _rope_tables(seq, rope_dim, batch, ctx_rows):
    rows = seq // GRID_W
    row = jnp.repeat(jnp.arange(rows, dtype=F32), GRID_W)
    col = jnp.tile(jnp.arange(GRID_W, dtype=F32), rows)
    quarter = rope_dim // 4
    freqs = ROPE_THETA ** (-jnp.arange(quarter, dtype=F32) / quarter)
    ang = jnp.concatenate([row[:, None] * freqs, col[:, None] * freqs], axis=-1)
    cos, sin = jnp.cos(ang), jnp.sin(ang)
    pad = V7X_LANES - rope_dim
    cos_l = jnp.pad(jnp.concatenate([cos, cos], axis=-1), ((0, 0), (0, pad)))
    sin_l = jnp.pad(jnp.concatenate([-sin, sin], axis=-1), ((0, 0), (0, pad)))
    cos_c = jnp.pad(jnp.ones((ctx_rows, rope_dim), F32), ((0, 0), (0, pad)))
    sin_c = jnp.zeros((ctx_rows, V7X_LANES), F32)
    return (jnp.concatenate([jnp.tile(cos_l, (batch, 1)), cos_c], axis=0),
            jnp.concatenate([jnp.tile(sin_l, (batch, 1)), sin_c], axis=0))


def _swap_halves(a):
    half = a.shape[-1] // 2
    return jnp.concatenate([a[..., half:], a[..., :half]], axis=-1)


def _pad_lanes(a):
    return jnp.pad(a, [(0, 0)] * (a.ndim - 1) + [(0, V7X_LANES - a.shape[-1])])


def _mla_weights(w_in, w_uq, q_g, k_g, q_lora, kv_lora):
    d = w_in.shape[0]
    c_kr = q_lora + kv_lora
    w_kr = w_in[:, c_kr:c_kr + MLA_ROPE]
    w_in_ext = jnp.concatenate([
        w_in[:, :c_kr], _pad_lanes(w_kr), _pad_lanes(_swap_halves(w_kr)), w_in[:, c_kr + MLA_ROPE:]], axis=1)
    wq = w_uq.reshape(q_lora, MLA_HEADS, MLA_NOPE + MLA_ROPE)
    wq_rope = wq[:, :, MLA_NOPE:]
    wq_ext = jnp.concatenate([wq[:, :, :MLA_NOPE], _pad_lanes(wq_rope), _pad_lanes(_swap_halves(wq_rope))], axis=-1)
    wq_ext = wq_ext.reshape(q_lora, MLA_HEADS * 3 * V7X_LANES)

    def gains(g):
        g_r = g[MLA_NOPE:]
        return [g[:MLA_NOPE].reshape(1, -1), _pad_lanes(g_r).reshape(1, -1), _pad_lanes(_swap_halves(g_r)).reshape(1, -1)]

    return w_in_ext.astype(BF16), wq_ext.astype(BF16), gains(q_g) + gains(k_g)


def kernel(x, c, ctx, c_ctx, ada_w, ada_b, norm1_g, norm2_g, a_w_in, a_q_lora_g, a_kv_lora_g, a_w_uq, a_w_ukv, a_q_g, a_k_g, b_conv_w, b_conv_b, e_w_o, c_w_qkv, c_q_g, c_k_g, c_w_o, p_w_q, p_sub_keys, p_u, p_v):
    batch, seq, d = x.shape
    ctx_len = ctx.shape[1]
    depth = ada_w.shape[0]
    q_lora = a_q_lora_g.shape[1]
    kv_lora = a_kv_lora_g.shape[1]
    conv_width = b_conv_w.shape[2]
    t_lat = batch * seq
    t_ctx = batch * ctx_len
    t = t_lat + t_ctx
    bm = ROW_BLOCK
    assert seq % bm == 0 and t_ctx % bm == 0 and t % PEER_TB == 0 and seq % min(ATT_Q_ROWS, seq) == 0
    assert seq & (seq - 1) == 0 and ctx_len & (ctx_len - 1) == 0 and seq % GRID_W == 0
    assert batch + 1 <= V7X_SUBLANES

    blocks_per_batch = seq // bm
    group_of = lambda i: jnp.minimum(i // blocks_per_batch, batch)

    cvec = jnp.zeros((V7X_SUBLANES, d), F32).at[:batch].set(c).at[batch].set(c_ctx)
    mods = _ada(cvec, ada_w, ada_b)
    mods = mods.reshape(depth, V7X_SUBLANES, N_MOD, 1, d).transpose(0, 2, 1, 3, 4)

    cos_a, sin_a = _rope_tables(seq, MLA_ROPE, batch, t_ctx)
    cos_c, sin_c = _rope_tables(seq, GQA_HD, batch, t_ctx)

    u_all = _pack_table(p_u, transpose=False)
    vt_all = _pack_table(p_v, transpose=True)
    xs = jnp.concatenate([x.reshape(t_lat, d), ctx.reshape(t_ctx, d)], axis=0)
    for l in range(depth):
        sh1, sc1, g1, sh2, sc2, g2 = (mods[l, k] for k in range(N_MOD))
        i = l // 2
        need_ctx = l < depth - 1
        if l % 2 == 0:
            w_in_ext, wq_ext, gains = _mla_weights(a_w_in[i], a_w_uq[i], a_q_g[i], a_k_g[i], q_lora, kv_lora)
            p = _normproj(xs, norm1_g[l], sc1, sh1, w_in_ext, group_of, emit_h=False)
            q, k, v = _mla_qkv(p, cos_a, sin_a, wq_ext, a_w_ukv[i].astype(BF16),
                               [a_q_lora_g[i].reshape(1, -1), a_kv_lora_g[i].reshape(1, -1)] + gains,
                               q_lora, kv_lora)
            att, att_ctx = _attention(q, k, v, batch=batch, seq=seq, ctx_len=ctx_len, kv_heads=MLA_HEADS,
                                      group=1, dk=2 * V7X_LANES, dv=MLA_V, need_ctx=need_ctx)
            z_col0 = q_lora + kv_lora + 2 * V7X_LANES
            xs = _mixout(att, att_ctx, e_w_o[i].astype(BF16), xs, g1, group_of,
                         conv=(p, z_col0, b_conv_w[i], b_conv_b[i]), t_lat=t_lat, seq=seq, ctx_len=ctx_len)
        else:
            p = _normproj(xs, norm1_g[l], sc1, sh1, c_w_qkv[i].astype(BF16), group_of, emit_h=False)
            q, k, v = _gqa_qkv(p, cos_c, sin_c, c_q_g[i], c_k_g[i])
            att, att_ctx = _attention(q, k, v, batch=batch, seq=seq, ctx_len=ctx_len, kv_heads=GQA_KV_HEADS,
                                      group=GQA_HEADS // GQA_KV_HEADS, dk=GQA_HD, dv=GQA_HD, need_ctx=need_ctx)
            xs = _mixout(att, att_ctx, c_w_o[i].astype(BF16), xs, g1, group_of,
                         t_lat=t_lat, seq=seq, ctx_len=ctx_len)
        qp, h2 = _normproj(xs, norm2_g[l], sc2, sh2, p_w_q[l].astype(BF16), group_of, emit_h=True)
        r1, p1, n0p0 = _route(qp, p_sub_keys[l].astype(BF16))
        xs = _peer(h2, u_all, vt_all, l, r1, p1, n0p0, xs, g2, group_of)
    return xs.reshape(batch, seq, d)
```

```python
import functools
import math

import jax
import jax.numpy as jnp
from jax import lax
from jax.experimental import pallas as pl
from jax.experimental.pallas import tpu as pltpu

F32 = jnp.float32
BF16 = jnp.bfloat16

EPS = 1e-6
ROPE_THETA = 10000.0
GRID_W = 64
MLA_HEADS = 8
MLA_NOPE = 128
MLA_ROPE = 64
MLA_V = 128
GQA_HEADS = 16
GQA_KV_HEADS = 4
GQA_HD = 128
PEER_HEADS = 8
N_KEYS = 128
PEER_TOPK = 16
N_MOD = 6

V7X_LANES = 128
V7X_SUBLANES = 8
V7X_VMEM_LIMIT_BYTES = 56 * 1024 * 1024

ROW_BLOCK = 512
MAX_COL_BLOCK = 4096
ATT_Q_ROWS = 2048
ATT_CHAIN_ROWS = 256
PEER_TB = 512
PEER_EB = 1024
PEER_ROWS = PEER_EB // N_KEYS
SQRT_HALF = 0.7071067811865476
NEG_INF = float("-inf")


def _cparams(sem):
    return pltpu.CompilerParams(dimension_semantics=sem, vmem_limit_bytes=V7X_VMEM_LIMIT_BYTES)


def _col_block(n):
    bn = n
    while bn > MAX_COL_BLOCK:
        assert bn % 2 == 0
        bn //= 2
    return bn


def _weight_spec(k, bn, n):
    if bn == n:
        return pl.BlockSpec((k, bn), lambda i, j: (0, 0), pipeline_mode=pl.Buffered(1))
    return pl.BlockSpec((k, bn), lambda i, j: (0, j))


def _nt_dot(a, b):
    return lax.dot_general(a, b, (((1,), (1,)), ((), ())), preferred_element_type=F32)


def _ada_kernel(c_ref, w_ref, b_ref, o_ref):
    c = c_ref[...]
    s = c / (1.0 + jnp.exp(-c))
    o_ref[0] = jnp.dot(s.astype(BF16), w_ref[0].astype(BF16), preferred_element_type=F32) + b_ref[0]


def _ada(cvec, ada_w, ada_b):
    depth, d, n = ada_w.shape
    nb = 2048
    rows = cvec.shape[0]
    return pl.pallas_call(
        _ada_kernel,
        grid=(depth, n // nb),
        in_specs=[
            pl.BlockSpec((rows, d), lambda l, j: (0, 0)),
            pl.BlockSpec((1, d, nb), lambda l, j: (l, 0, j)),
            pl.BlockSpec((1, 1, nb), lambda l, j: (l, 0, j)),
        ],
        out_specs=pl.BlockSpec((1, rows, nb), lambda l, j: (l, 0, j)),
        out_shape=jax.ShapeDtypeStruct((depth, rows, n), F32),
        compiler_params=_cparams(("parallel", "parallel")),
        name="ada_mod",
    )(cvec, ada_w, ada_b.reshape(depth, 1, n))


def _normproj_kernel(x_ref, g_ref, sc_ref, sh_ref, w_ref, o_ref, *rest, emit_h):
    hs_ref = rest[-1]

    @pl.when(pl.program_id(1) == 0)
    def _():
        x = x_ref[...]
        y = x * lax.rsqrt(jnp.mean(x * x, axis=-1, keepdims=True) + EPS) * g_ref[...]
        h = (y * (1.0 + sc_ref[0]) + sh_ref[0]).astype(BF16)
        hs_ref[...] = h
        if emit_h:
            rest[0][...] = pltpu.bitcast(h, jnp.uint32)

    out = jnp.dot(hs_ref[...], w_ref[...], preferred_element_type=F32)
    o_ref[...] = pltpu.bitcast(out.astype(BF16), jnp.uint32) if emit_h else out


def _normproj(x, g, sc, sh, w, group_of, emit_h):
    t, d = x.shape
    n = w.shape[1]
    bm, bn = ROW_BLOCK, _col_block(n)
    out_shape = [jax.ShapeDtypeStruct((t, n), F32)]
    out_specs = [pl.BlockSpec((bm, bn), lambda i, j: (i, j))]
    if emit_h:
        out_shape = [jax.ShapeDtypeStruct((t // 2, n), jnp.uint32)]
        out_specs = [pl.BlockSpec((bm // 2, bn), lambda i, j: (i, j))]
        out_shape.append(jax.ShapeDtypeStruct((t // 2, d), jnp.uint32))
        out_specs.append(pl.BlockSpec((bm // 2, d), lambda i, j: (i, 0)))
    res = pl.pallas_call(
        functools.partial(_normproj_kernel, emit_h=emit_h),
        grid=(t // bm, n // bn),
        in_specs=[
            pl.BlockSpec((bm, d), lambda i, j: (i, 0)),
            pl.BlockSpec((1, d), lambda i, j: (0, 0)),
            pl.BlockSpec((1, 1, d), lambda i, j: (group_of(i), 0, 0)),
            pl.BlockSpec((1, 1, d), lambda i, j: (group_of(i), 0, 0)),
            _weight_spec(d, bn, n),
        ],
        out_specs=out_specs,
        out_shape=out_shape,
        scratch_shapes=[pltpu.VMEM((bm, d), BF16)],
        compiler_params=_cparams(("parallel", "arbitrary")),
        name="normproj_h" if emit_h else "normproj",
    )(x, g.reshape(1, d), sc, sh, w)
    return res if emit_h else res[0]


def _mla_qkv_kernel(p_ref, cos_ref, sin_ref, wuq_ref, wukv_ref, qlg_ref, kvlg_ref,
                    qgn_ref, qgr_ref, qgs_ref, kgn_ref, kgr_ref, kgs_ref,
                    q_ref, k_ref, v_ref, *, q_lora, kv_lora, scale):
    qk_dim = MLA_NOPE + MLA_ROPE
    lanes = V7X_LANES
    cq = p_ref[:, 0:q_lora]
    ckv = p_ref[:, q_lora:q_lora + kv_lora]
    kr = p_ref[:, q_lora + kv_lora:q_lora + kv_lora + lanes]
    krs = p_ref[:, q_lora + kv_lora + lanes:q_lora + kv_lora + 2 * lanes]
    cos = cos_ref[...]
    sin = sin_ref[...]

    cqn = cq * lax.rsqrt(jnp.mean(cq * cq, axis=-1, keepdims=True) + EPS) * qlg_ref[...]
    qraw = jnp.dot(cqn.astype(BF16), wuq_ref[...], preferred_element_type=F32)
    ckvn = ckv * lax.rsqrt(jnp.mean(ckv * ckv, axis=-1, keepdims=True) + EPS) * kvlg_ref[...]
    kvraw = jnp.dot(ckvn.astype(BF16), wukv_ref[...], preferred_element_type=F32)

    kr_rot = kr * kgr_ref[...] * cos + krs * kgs_ref[...] * sin
    kr_ssq = jnp.sum(kr * kr, axis=-1, keepdims=True)
    for h in range(MLA_HEADS):
        nope = qraw[:, h * 3 * lanes:h * 3 * lanes + lanes]
        rope = qraw[:, h * 3 * lanes + lanes:h * 3 * lanes + 2 * lanes]
        rope_sw = qraw[:, h * 3 * lanes + 2 * lanes:h * 3 * lanes + 3 * lanes]
        ssq = jnp.sum(nope * nope, axis=-1, keepdims=True) + jnp.sum(rope * rope, axis=-1, keepdims=True)
        r = lax.rsqrt(ssq * (1.0 / qk_dim) + EPS) * scale
        q_ref[:, h * 2 * lanes:h * 2 * lanes + lanes] = (nope * r * qgn_ref[...]).astype(BF16)
        q_ref[:, h * 2 * lanes + lanes:(h + 1) * 2 * lanes] = (
            (rope * qgr_ref[...] * cos + rope_sw * qgs_ref[...] * sin) * r).astype(BF16)

        k_nope = kvraw[:, h * 2 * lanes:h * 2 * lanes + lanes]
        v = kvraw[:, h * 2 * lanes + lanes:(h + 1) * 2 * lanes]
        kssq = jnp.sum(k_nope * k_nope, axis=-1, keepdims=True) + kr_ssq
        rk = lax.rsqrt(kssq * (1.0 / qk_dim) + EPS)
        k_ref[:, h * 2 * lanes:h * 2 * lanes + lanes] = (k_nope * rk * kgn_ref[...]).astype(BF16)
        k_ref[:, h * 2 * lanes + lanes:(h + 1) * 2 * lanes] = (kr_rot * rk).astype(BF16)
        v_ref[:, h * 2 * lanes:h * 2 * lanes + lanes] = v.astype(BF16)
        v_ref[:, h * 2 * lanes + lanes:(h + 1) * 2 * lanes] = jnp.ones((v.shape[0], lanes), BF16)


def _mla_qkv(p, cos, sin, wuq, wukv, gains, q_lora, kv_lora):
    t = p.shape[0]
    bm = ROW_BLOCK
    lanes = V7X_LANES
    head_cols = q_lora + kv_lora + 2 * lanes
    full = lambda a: pl.BlockSpec(a.shape, lambda i: (0,) * a.ndim)
    return pl.pallas_call(
        functools.partial(_mla_qkv_kernel, q_lora=q_lora, kv_lora=kv_lora,
                          scale=float(MLA_NOPE + MLA_ROPE) ** -0.5),
        grid=(t // bm,),
        in_specs=[
            pl.BlockSpec((bm, head_cols), lambda i: (i, 0)),
            pl.BlockSpec((bm, lanes), lambda i: (i, 0)),
            pl.BlockSpec((bm, lanes), lambda i: (i, 0)),
            full(wuq), full(wukv)] + [full(g) for g in gains],
        out_specs=[
            pl.BlockSpec((bm, MLA_HEADS * 2 * lanes), lambda i: (i, 0)),
            pl.BlockSpec((bm, MLA_HEADS * 2 * lanes), lambda i: (i, 0)),
            pl.BlockSpec((bm, MLA_HEADS * 2 * lanes), lambda i: (i, 0)),
        ],
        out_shape=[
            jax.ShapeDtypeStruct((t, MLA_HEADS * 2 * lanes), BF16),
            jax.ShapeDtypeStruct((t, MLA_HEADS * 2 * lanes), BF16),
            jax.ShapeDtypeStruct((t, MLA_HEADS * 2 * lanes), BF16),
        ],
        compiler_params=_cparams(("parallel",)),
        name="mla_qkv",
    )(p, cos, sin, wuq, wukv, *gains)


def _gqa_qkv_kernel(p_ref, cos_ref, sin_ref, qg_ref, kg_ref, q_ref, k_ref, v_ref, *, scale):
    hd = GQA_HD
    cos = cos_ref[...]
    sin = sin_ref[...]

    def head(x, g):
        y = x * lax.rsqrt(jnp.mean(x * x, axis=-1, keepdims=True) + EPS) * g
        return y * cos + pltpu.roll(y, hd // 2, 1) * sin

    for h in range(GQA_HEADS):
        q_ref[:, h * hd:(h + 1) * hd] = (head(p_ref[:, h * hd:(h + 1) * hd], qg_ref[...]) * scale).astype(BF16)
    k0 = GQA_HEADS * hd
    v0 = k0 + GQA_KV_HEADS * hd
    for h in range(GQA_KV_HEADS):
        k_ref[:, h * hd:(h + 1) * hd] = head(p_ref[:, k0 + h * hd:k0 + (h + 1) * hd], kg_ref[...]).astype(BF16)
    for h in range(GQA_KV_HEADS):
        v_ref[:, h * 2 * hd:h * 2 * hd + hd] = p_ref[:, v0 + h * hd:v0 + (h + 1) * hd].astype(BF16)
        v_ref[:, h * 2 * hd + hd:(h + 1) * 2 * hd] = jnp.ones((v_ref.shape[0], hd), BF16)


def _gqa_qkv(p, cos, sin, qg, kg):
    t, n = p.shape
    bm = ROW_BLOCK
    hd = GQA_HD
    return pl.pallas_call(
        functools.partial(_gqa_qkv_kernel, scale=float(hd) ** -0.5),
        grid=(t // bm,),
        in_specs=[
            pl.BlockSpec((bm, n), lambda i: (i, 0)),
            pl.BlockSpec((bm, hd), lambda i: (i, 0)),
            pl.BlockSpec((bm, hd), lambda i: (i, 0)),
            pl.BlockSpec((1, hd), lambda i: (0, 0)),
            pl.BlockSpec((1, hd), lambda i: (0, 0)),
        ],
        out_specs=[
            pl.BlockSpec((bm, GQA_HEADS * hd), lambda i: (i, 0)),
            pl.BlockSpec((bm, GQA_KV_HEADS * hd), lambda i: (i, 0)),
            pl.BlockSpec((bm, GQA_KV_HEADS * 2 * hd), lambda i: (i, 0)),
        ],
        out_shape=[
            jax.ShapeDtypeStruct((t, GQA_HEADS * hd), BF16),
            jax.ShapeDtypeStruct((t, GQA_KV_HEADS * hd), BF16),
            jax.ShapeDtypeStruct((t, GQA_KV_HEADS * 2 * hd), BF16),
        ],
        compiler_params=_cparams(("parallel",)),
        name="gqa_qkv",
    )(p, cos, sin, qg.reshape(1, hd), kg.reshape(1, hd))


def _attn_kernel(q_ref, kc_ref, vc_ref, *rest, group, dk, dv, with_latent):
    if with_latent:
        kl_ref, vl_ref, o_ref = rest
    else:
        (o_ref,) = rest
    tq = q_ref.shape[0]
    chain_rows = min(ATT_CHAIN_ROWS, group * tq)
    per_head = tq // chain_rows if chain_rows < tq else 0
    for c in range(group * tq // chain_rows):
        if per_head:
            g, part = divmod(c, per_head)
            rows = slice(part * chain_rows, (part + 1) * chain_rows)
            q = q_ref[rows, g * dk:(g + 1) * dk]
            dst = [(rows, g, slice(0, chain_rows))]
        else:
            heads = range(c * chain_rows // tq, (c + 1) * chain_rows // tq)
            q = jnp.concatenate([q_ref[:, g * dk:(g + 1) * dk] for g in heads], axis=0)
            dst = [(slice(0, tq), g, slice(n * tq, (n + 1) * tq)) for n, g in enumerate(heads)]
        sc = _nt_dot(q, kc_ref[...])
        m = jnp.max(sc, axis=-1, keepdims=True)
        if with_latent:
            sl = _nt_dot(q, kl_ref[...])
            m = jnp.maximum(m, jnp.max(sl, axis=-1, keepdims=True))
        o = jnp.dot(jnp.exp((sc - m).astype(BF16)), vc_ref[...], preferred_element_type=F32)
        if with_latent:
            o = o + jnp.dot(jnp.exp((sl - m).astype(BF16)), vl_ref[...], preferred_element_type=F32)
        o = (o[:, :dv] / o[:, dv:dv + 1]).astype(BF16)
        for rows, g, src in dst:
            o_ref[rows, g * dv:(g + 1) * dv] = o[src, :]


def _attention(q, k, v, *, batch, seq, ctx_len, kv_heads, group, dk, dv, need_ctx):
    t_lat = batch * seq
    tq = min(ATT_Q_ROWS // group, seq)
    nq = seq // tq
    ctx_blk0 = t_lat // ctx_len
    common = dict(group=group, dk=dk, dv=dv)
    lat = pl.pallas_call(
        functools.partial(_attn_kernel, with_latent=True, **common),
        grid=(batch, kv_heads, nq),
        in_specs=[
            pl.BlockSpec((tq, group * dk), lambda b, h, j: (b * nq + j, h)),
            pl.BlockSpec((ctx_len, dk), lambda b, h, j: (ctx_blk0 + b, h)),
            pl.BlockSpec((ctx_len, 2 * dv), lambda b, h, j: (ctx_blk0 + b, h)),
            pl.BlockSpec((seq, dk), lambda b, h, j: (b, h)),
            pl.BlockSpec((seq, 2 * dv), lambda b, h, j: (b, h)),
        ],
        out_specs=pl.BlockSpec((tq, group * dv), lambda b, h, j: (b * nq + j, h)),
        out_shape=jax.ShapeDtypeStruct((t_lat, kv_heads * group * dv), BF16),
        compiler_params=_cparams(("parallel", "parallel", "arbitrary")),
        name="attn_latent",
    )(q, k, v, k, v)
    if not need_ctx:
        return lat, None
    ctx = pl.pallas_call(
        functools.partial(_attn_kernel, with_latent=False, **common),
        grid=(batch, kv_heads),
        in_specs=[
            pl.BlockSpec((ctx_len, group * dk), lambda b, h: (ctx_blk0 + b, h)),
            pl.BlockSpec((ctx_len, dk), lambda b, h: (ctx_blk0 + b, h)),
            pl.BlockSpec((ctx_len, 2 * dv), lambda b, h: (ctx_blk0 + b, h)),
        ],
        out_specs=pl.BlockSpec((ctx_len, group * dv), lambda b, h: (b, h)),
        out_shape=jax.ShapeDtypeStruct((batch * ctx_len, kv_heads * group * dv), BF16),
        compiler_params=_cparams(("parallel", "parallel")),
        name="attn_ctx",
    )(q, k, v)
    return lat, ctx


def _mixout_kernel(*refs, with_conv, with_ctx, t_lat, seq, ctx_len, a_width):
    a_ref, refs = refs[0], refs[1:]
    actx_ref = None
    if with_ctx:
        actx_ref, refs = refs[0], refs[1:]
    if with_conv:
        (bg_ref, cg_ref, hz_ref, cgp_ref, hzp_ref, cgn_ref, hzn_ref, cw_ref, cb_ref,
         w_ref, x_ref, gate_ref, o_ref, cs_ref) = refs
    else:
        w_ref, x_ref, gate_ref, o_ref = refs
    i = pl.program_id(0)

    if with_conv:
        @pl.when(pl.program_id(1) == 0)
        def _():
            bm = cg_ref.shape[0]
            u = cg_ref[...] * hz_ref[...]
            u_before = cgp_ref[V7X_SUBLANES - 1:V7X_SUBLANES, :] * hzp_ref[V7X_SUBLANES - 1:V7X_SUBLANES, :]
            u_after = cgn_ref[0:1, :] * hzn_ref[0:1, :]
            local = lax.broadcasted_iota(jnp.int32, (bm, 1), 0)
            row = local + i * bm
            in_lat = row < t_lat
            seg_pos = jnp.where(in_lat, jnp.bitwise_and(row, seq - 1), jnp.bitwise_and(row - t_lat, ctx_len - 1))
            seg_len = jnp.where(in_lat, seq, ctx_len)
            up = jnp.where(local == 0, u_before, pltpu.roll(u, 1, 0))
            up = jnp.where(seg_pos == 0, 0.0, up)
            un = jnp.where(local == bm - 1, u_after, pltpu.roll(u, bm - 1, 0))
            un = jnp.where(seg_pos == seg_len - 1, 0.0, un)
            y = up * cw_ref[0:1, :] + u * cw_ref[1:2, :] + un * cw_ref[2:3, :] + cb_ref[...]
            cs_ref[...] = (bg_ref[...] * y).astype(BF16)

    def finish(a):
        acc = jnp.dot(a, w_ref[0:a_width, :], preferred_element_type=F32)
        if with_conv:
            acc = acc + jnp.dot(cs_ref[...], w_ref[a_width:, :], preferred_element_type=F32)
        o_ref[...] = x_ref[...] + gate_ref[0] * acc

    if with_ctx:
        lat_blocks = t_lat // a_ref.shape[0]
        pl.when(i < lat_blocks)(lambda: finish(a_ref[...]))
        pl.when(i >= lat_blocks)(lambda: finish(actx_ref[...]))
    else:
        finish(a_ref[...])


def _mixout(a, a_ctx, w, x, gate, group_of, conv=None, *, t_lat, seq, ctx_len):
    d = x.shape[1]
    t = x.shape[0] if a_ctx is not None else t_lat
    bm, bn = ROW_BLOCK, _col_block(d)
    a_width = a.shape[1]
    lat_blocks = t_lat // bm
    kw = dict(t_lat=t_lat, seq=seq, ctx_len=ctx_len, a_width=a_width, with_ctx=a_ctx is not None)
    in_specs = [pl.BlockSpec((bm, a_width), lambda i, j: (jnp.minimum(i, lat_blocks - 1), 0))]
    args = [a]
    if a_ctx is not None:
        in_specs.append(pl.BlockSpec((bm, a_width), lambda i, j: (jnp.maximum(i - lat_blocks, 0), 0)))
        args.append(a_ctx)
    scratch = []
    if conv is not None:
        p, z_col0, cw, cb = conv
        cwid = cw.shape[1]
        assert z_col0 % cwid == 0
        zb = z_col0 // cwid
        sub = V7X_SUBLANES
        last_halo = p.shape[0] // sub - 1
        prev_idx = lambda i: jnp.maximum(i * (bm // sub) - 1, 0)
        next_idx = lambda i: jnp.minimum((i + 1) * (bm // sub), last_halo)
        in_specs += [
            pl.BlockSpec((bm, cwid), lambda i, j: (i, zb)),
            pl.BlockSpec((bm, cwid), lambda i, j: (i, zb + 1)),
            pl.BlockSpec((bm, cwid), lambda i, j: (i, zb + 2)),
            pl.BlockSpec((sub, cwid), lambda i, j: (prev_idx(i), zb + 1)),
            pl.BlockSpec((sub, cwid), lambda i, j: (prev_idx(i), zb + 2)),
            pl.BlockSpec((sub, cwid), lambda i, j: (next_idx(i), zb + 1)),
            pl.BlockSpec((sub, cwid), lambda i, j: (next_idx(i), zb + 2)),
            pl.BlockSpec(cw.shape, lambda i, j: (0, 0)),
            pl.BlockSpec((1, cwid), lambda i, j: (0, 0)),
        ]
        args += [p] * 7 + [cw, cb.reshape(1, cwid)]
        scratch = [pltpu.VMEM((bm, cwid), BF16)]
    in_specs += [
        _weight_spec(w.shape[0], bn, d),
        pl.BlockSpec((bm, bn), lambda i, j: (i, j)),
        pl.BlockSpec((1, 1, bn), lambda i, j: (group_of(i), 0, j)),
    ]
    args += [w, x, gate]
    return pl.pallas_call(
        functools.partial(_mixout_kernel, with_conv=conv is not None, **kw),
        grid=(t // bm, d // bn),
        in_specs=in_specs,
        out_specs=pl.BlockSpec((bm, bn), lambda i, j: (i, j)),
        out_shape=jax.ShapeDtypeStruct((t, d), F32),
        scratch_shapes=scratch,
        compiler_params=_cparams(("parallel", "arbitrary")),
        name="mixout_conv" if conv is not None else "mixout",
    )(*args)


_CAND_PAIRS = [(a, b) for a in range(PEER_TOPK) for b in range(PEER_TOPK // (a + 1))]
_CAND_ROWS = -(-len(_CAND_PAIRS) // V7X_SUBLANES) * V7X_SUBLANES
BF16_ROWS = 2 * V7X_SUBLANES


def _oddeven_sort_pairs(n):
    pairs = []

    def merge(lo, hi, r):
        step = r * 2
        if step < hi - lo:
            merge(lo, hi, step)
            merge(lo + r, hi, step)
            pairs.extend((i, i + r) for i in range(lo + r, hi - r, step))
        else:
            pairs.append((lo, lo + r))

    def sort(lo, hi):
        if hi - lo >= 1:
            mid = lo + (hi - lo) // 2
            sort(lo, mid)
            sort(mid + 1, hi)
            merge(lo, hi, 1)

    sort(0, n - 1)
    return pairs


_SORT_PAIRS = _oddeven_sort_pairs(PEER_TOPK)


def _top_sorted(s):
    sub = V7X_SUBLANES
    k = PEER_TOPK
    assert s.shape[0] == sub * k
    v = [s[r * sub:(r + 1) * sub, :] for r in range(k)]

    def exchange(i, j):
        v[i], v[j] = jnp.maximum(v[i], v[j]), jnp.minimum(v[i], v[j])

    for i, j in _SORT_PAIRS:
        exchange(i, j)
    shift = sub // 2
    while shift:
        v = [jnp.maximum(v[r], pltpu.roll(v[k - 1 - r], shift, 0)) for r in range(k)]
        dist = k // 2
        while dist:
            for i in range(k):
                if not i & dist:
                    exchange(i, i + dist)
            dist //= 2
        shift //= 2
    return v


def _bf16_pair_words(v):
    bits = pltpu.bitcast(v.astype(BF16).astype(F32), jnp.uint32)
    return bits | (bits >> 16)


def _route_kernel(q_ref, sk_ref, r1_ref, p1_ref, np_ref, cand_ref):
    nk = N_KEYS
    k_top = PEER_TOPK
    lanes = V7X_LANES
    q = pltpu.bitcast(q_ref[...], BF16)
    tb = q.shape[0]
    cand_ref[len(_CAND_PAIRS):, :] = jnp.full((_CAND_ROWS - len(_CAND_PAIRS), lanes), NEG_INF, F32)
    for c in range(tb // lanes):
        cs = slice(c * lanes, (c + 1) * lanes)
        s0, s1 = (_nt_dot(sk_ref[0, half], q[cs, half * nk:(half + 1) * nk])
                  for half in range(2))

        tops0 = [v[0:1, :] for v in _top_sorted(s0)]
        tops1 = [v[0:1, :] for v in _top_sorted(s1)]
        rank = jnp.full((nk, lanes), float(k_top), F32)
        for b in reversed(range(k_top)):
            rank = jnp.where(s1 >= tops1[b], float(b), rank)

        for r, (a, b) in enumerate(_CAND_PAIRS):
            cand_ref[r:r + 1, :] = tops0[a] + tops1[b]
        cand = cand_ref[...]
        top = tops0[0] + tops1[0]
        z = jnp.zeros((1, lanes), F32)
        for k in range(k_top):
            tau = jnp.max(cand, axis=0, keepdims=True)
            z = z + jnp.exp(tau - top)
            cand = jnp.where(cand == tau, NEG_INF, cand)

        n0 = jnp.zeros((nk, lanes), F32)
        for b in range(k_top):
            n0 = jnp.where(s0 + tops1[b] >= tau, float(b + 1), n0)
        n0 = _bf16_pair_words(n0)
        p0 = _bf16_pair_words(jnp.exp(s0 - tops0[0]))
        for g in range(nk // PEER_ROWS):
            np_ref[0, g, 0:PEER_ROWS, cs] = n0[g * PEER_ROWS:(g + 1) * PEER_ROWS, :]
            np_ref[0, g, PEER_ROWS:, cs] = p0[g * PEER_ROWS:(g + 1) * PEER_ROWS, :]
        rank_b = rank.astype(BF16)
        p1_b = (jnp.exp(s1 - tops1[0]) / z).astype(BF16)
        for g in range(nk // BF16_ROWS):
            r1_ref[0, g, :, cs] = pltpu.bitcast(rank_b[g * BF16_ROWS:(g + 1) * BF16_ROWS, :], jnp.int32)
            p1_ref[0, g, :, cs] = pltpu.bitcast(p1_b[g * BF16_ROWS:(g + 1) * BF16_ROWS, :], jnp.int32)


def _route(qp, sub_keys):
    t = 2 * qp.shape[0]
    tb = PEER_TB
    nk = N_KEYS
    heads = PEER_HEADS
    packed = jax.ShapeDtypeStruct((heads, nk // BF16_ROWS, V7X_SUBLANES, t), jnp.int32)
    packed_spec = pl.BlockSpec((1, nk // BF16_ROWS, V7X_SUBLANES, tb), lambda i, h: (h, 0, 0, i))
    plain = jax.ShapeDtypeStruct((heads, nk // PEER_ROWS, 2 * PEER_ROWS, t), jnp.uint32)
    plain_spec = pl.BlockSpec((1, nk // PEER_ROWS, 2 * PEER_ROWS, tb), lambda i, h: (h, 0, 0, i))
    return pl.pallas_call(
        _route_kernel,
        grid=(t // tb, heads),
        in_specs=[
            pl.BlockSpec((tb // 2, 2 * nk), lambda i, h: (i, h)),
            pl.BlockSpec((1, 2, nk, sub_keys.shape[-1]), lambda i, h: (h, 0, 0, 0)),
        ],
        out_specs=[packed_spec, packed_spec, plain_spec],
        out_shape=[packed, packed, plain],
        scratch_shapes=[pltpu.VMEM((_CAND_ROWS, V7X_LANES), F32)],
        compiler_params=_cparams(("parallel", "arbitrary")),
        name="peer_route",
    )(qp, sub_keys)


def _peer_kernel(h_ref, u_ref, vt_ref, r1_ref, p1_ref, np_ref, x_ref, gate_ref,
                 o_ref, acc_ref, at0_ref, at1_ref, hs_ref, *, n_blocks):
    j = pl.program_id(1)
    nk = N_KEYS
    lanes = V7X_LANES
    eb, tb = at0_ref.shape
    at_refs = (at0_ref, at1_ref)

    def key_row(hd, row, cs):
        words = jnp.broadcast_to(np_ref[hd, 0, row:row + 1, cs], (V7X_SUBLANES, lanes))
        return pltpu.bitcast(words, BF16)[None]

    def gate_rows(il, prev):
        for c in range(tb // lanes):
            cs = slice(c * lanes, (c + 1) * lanes)
            w = None
            for hd in range(PEER_HEADS):
                r1 = pltpu.bitcast(r1_ref[hd, :, :, cs], BF16)
                p1 = pltpu.bitcast(p1_ref[hd, :, :, cs], BF16)
                term = jnp.where(r1 < key_row(hd, il, cs), p1 * key_row(hd, PEER_ROWS + il, cs),
                                 jnp.zeros((), BF16))
                w = term if w is None else w + term
            a = at_refs[prev][il * nk:(il + 1) * nk, cs]
            gelu = (0.5 * a * (1.0 + lax.erf(a * SQRT_HALF))).astype(BF16)
            for g in range(nk // BF16_ROWS):
                r0 = il * nk + g * BF16_ROWS
                hs_ref[r0:r0 + BF16_ROWS, cs] = w[g] * gelu[g * BF16_ROWS:(g + 1) * BF16_ROWS, :]

    def first_matmul(cur):
        at_refs[cur][...] = _nt_dot(pltpu.bitcast(u_ref[0], BF16), pltpu.bitcast(h_ref[...], BF16))

    def mix(prev):
        for il in range(PEER_ROWS):
            gate_rows(il, prev)
        acc_ref[...] += jnp.dot(pltpu.bitcast(vt_ref[0, 0], BF16), hs_ref[...], preferred_element_type=F32)

    @pl.when(j == 0)
    def _():
        acc_ref[...] = jnp.zeros_like(acc_ref)
        first_matmul(0)

    for parity in range(2):
        @pl.when(jnp.logical_and(jnp.logical_and(j > 0, j < n_blocks), j % 2 == parity))
        def _(parity=parity):
            first_matmul(parity)
            mix(1 - parity)

    @pl.when(j == n_blocks)
    def _():
        mix((n_blocks - 1) % 2)
        o_ref[...] = x_ref[...] + gate_ref[0] * acc_ref[...].T


def _peer(h, u, vt, layer, r1, p1, n0p0, x, gate, group_of):
    t, d = x.shape
    tb, eb = PEER_TB, PEER_EB
    n_blocks = vt.shape[1]
    nk = N_KEYS
    heads = PEER_HEADS
    packed_spec = pl.BlockSpec((heads, nk // BF16_ROWS, V7X_SUBLANES, tb), lambda i, j: (0, 0, 0, i))
    plain_spec = pl.BlockSpec((heads, 1, 2 * PEER_ROWS, tb), lambda i, j: (0, jnp.clip(j - 1, 0, n_blocks - 1), 0, i))
    last_tok = t // tb - 1
    h_spec = pl.BlockSpec((tb // 2, d), lambda i, j: (jnp.minimum(jnp.where(j == n_blocks, i + 1, i), last_tok), 0))
    x_spec = pl.BlockSpec((tb, d), lambda i, j: (jnp.where(j >= n_blocks - 1, i, jnp.maximum(i - 1, 0)), 0))
    return pl.pallas_call(
        functools.partial(_peer_kernel, n_blocks=n_blocks),
        grid=(t // tb, n_blocks + 1),
        in_specs=[
            h_spec,
            pl.BlockSpec((1, eb // 2, d), lambda i, j: (layer, jnp.minimum(j, n_blocks - 1), 0)),
            pl.BlockSpec((1, 1, d // 2, eb), lambda i, j: (layer, jnp.maximum(j - 1, 0), 0, 0)),
            packed_spec, packed_spec, plain_spec,
            x_spec,
            pl.BlockSpec((1, 1, d), lambda i, j: (group_of(i), 0, 0)),
        ],
        out_specs=pl.BlockSpec((tb, d), lambda i, j: (i, 0)),
        out_shape=jax.ShapeDtypeStruct((t, d), F32),
        scratch_shapes=[
            pltpu.VMEM((d, tb), F32),
            pltpu.VMEM((eb, tb), F32),
            pltpu.VMEM((eb, tb), F32),
            pltpu.VMEM((eb, tb), BF16),
        ],
        compiler_params=_cparams(("parallel", "arbitrary")),
        name="peer_mix",
    )(h, u, vt, r1, p1, n0p0, x, gate)


def _pack_table_kernel(x_ref, o_ref, *, transpose):
    x = x_ref[0]
    if transpose:
        o_ref[0, 0] = pltpu.bitcast(x.T.astype(BF16), jnp.uint32)
    else:
        o_ref[0] = pltpu.bitcast(x.astype(BF16), jnp.uint32)


def _pack_table(tab, transpose):
    layers, ne, d = tab.shape
    eb = PEER_EB
    if transpose:
        out_shape = jax.ShapeDtypeStruct((layers, ne // eb, d // 2, eb), jnp.uint32)
        out_spec = pl.BlockSpec((1, 1, d // 2, eb), lambda l, j: (l, j, 0, 0))
    else:
        out_shape = jax.ShapeDtypeStruct((layers, ne // 2, d), jnp.uint32)
        out_spec = pl.BlockSpec((1, eb // 2, d), lambda l, j: (l, j, 0))
    return pl.pallas_call(
        functools.partial(_pack_table_kernel, transpose=transpose),
        grid=(layers, ne // eb),
        in_specs=[pl.BlockSpec((1, eb, d), lambda l, j: (l, j, 0))],
        out_specs=out_spec,
        out_shape=out_shape,
        compiler_params=_cparams(("parallel", "parallel")),
        name="pack_table_t" if transpose else "pack_table",
    )(tab)


def _rope_tables(seq, rope_dim, batch, ctx_rows):
    rows = seq // GRID_W
    row = jnp.repeat(jnp.arange(rows, dtype=F32), GRID_W)
    col = jnp.tile(jnp.arange(GRID_W, dtype=F32), rows)
    quarter = rope_dim // 4
    freqs = ROPE_THETA ** (-jnp.arange(quarter, dtype=F32) / quarter)
    ang = jnp.concatenate([row[:, None] * freqs, col[:, None] * freqs], axis=-1)
    cos, sin = jnp.cos(ang), jnp.sin(ang)
    pad = V7X_LANES - rope_dim
    cos_l = jnp.pad(jnp.concatenate([cos, cos], axis=-1), ((0, 0), (0, pad)))
    sin_l = jnp.pad(jnp.concatenate([-sin, sin], axis=-1), ((0, 0), (0, pad)))
    cos_c = jnp.pad(jnp.ones((ctx_rows, rope_dim), F32), ((0, 0), (0, pad)))
    sin_c = jnp.zeros((ctx_rows, V7X_LANES), F32)
    return (jnp.concatenate([jnp.tile(cos_l, (batch, 1)), cos_c], axis=0),
            jnp.concatenate([jnp.tile(sin_l, (batch, 1)), sin_c], axis=0))


def _swap_halves(a):
    half = a.shape[-1] // 2
    return jnp.concatenate([a[..., half:], a[..., :half]], axis=-1)


def _pad_lanes(a):
    return jnp.pad(a, [(0, 0)] * (a.ndim - 1) + [(0, V7X_LANES - a.shape[-1])])


def _mla_weights(w_in, w_uq, q_g, k_g, q_lora, kv_lora):
    d = w_in.shape[0]
    c_kr = q_lora + kv_lora
    w_kr = w_in[:, c_kr:c_kr + MLA_ROPE]
    w_in_ext = jnp.concatenate([
        w_in[:, :c_kr], _pad_lanes(w_kr), _pad_lanes(_swap_halves(w_kr)), w_in[:, c_kr + MLA_ROPE:]], axis=1)
    wq = w_uq.reshape(q_lora, MLA_HEADS, MLA_NOPE + MLA_ROPE)
    wq_rope = wq[:, :, MLA_NOPE:]
    wq_ext = jnp.concatenate([wq[:, :, :MLA_NOPE], _pad_lanes(wq_rope), _pad_lanes(_swap_halves(wq_rope))], axis=-1)
    wq_ext = wq_ext.reshape(q_lora, MLA_HEADS * 3 * V7X_LANES)

    def gains(g):
        g_r = g[MLA_NOPE:]
        return [g[:MLA_NOPE].reshape(1, -1), _pad_lanes(g_r).reshape(1, -1), _pad_lanes(_swap_halves(g_r)).reshape(1, -1)]

    return w_in_ext.astype(BF16), wq_ext.astype(BF16), gains(q_g) + gains(k_g)


def kernel(x, c, ctx, c_ctx, ada_w, ada_b, norm1_g, norm2_g, a_w_in, a_q_lora_g, a_kv_lora_g, a_w_uq, a_w_ukv, a_q_g, a_k_g, b_conv_w, b_conv_b, e_w_o, c_w_qkv, c_q_g, c_k_g, c_w_o, p_w_q, p_sub_keys, p_u, p_v):
    batch, seq, d = x.shape
    ctx_len = ctx.shape[1]
    depth = ada_w.shape[0]
    q_lora = a_q_lora_g.shape[1]
    kv_lora = a_kv_lora_g.shape[1]
    conv_width = b_conv_w.shape[2]
    t_lat = batch * seq
    t_ctx = batch * ctx_len
    t = t_lat + t_ctx
    bm = ROW_BLOCK
    assert seq % bm == 0 and t_ctx % bm == 0 and t % PEER_TB == 0 and seq % min(ATT_Q_ROWS, seq) == 0
    assert seq & (seq - 1) == 0 and ctx_len & (ctx_len - 1) == 0 and seq % GRID_W == 0
    assert batch + 1 <= V7X_SUBLANES

    blocks_per_batch = seq // bm
    group_of = lambda i: jnp.minimum(i // blocks_per_batch, batch)

    cvec = jnp.zeros((V7X_SUBLANES, d), F32).at[:batch].set(c).at[batch].set(c_ctx)
    mods = _ada(cvec, ada_w, ada_b)
    mods = mods.reshape(depth, V7X_SUBLANES, N_MOD, 1, d).transpose(0, 2, 1, 3, 4)

    cos_a, sin_a = _rope_tables(seq, MLA_ROPE, batch, t_ctx)
    cos_c, sin_c = _rope_tables(seq, GQA_HD, batch, t_ctx)

    u_all = _pack_table(p_u, transpose=False)
    vt_all = _pack_table(p_v, transpose=True)
    xs = jnp.concatenate([x.reshape(t_lat, d), ctx.reshape(t_ctx, d)], axis=0)
    for l in range(depth):
        sh1, sc1, g1, sh2, sc2, g2 = (mods[l, k] for k in range(N_MOD))
        i = l // 2
        need_ctx = l < depth - 1
        if l % 2 == 0:
            w_in_ext, wq_ext, gains = _mla_weights(a_w_in[i], a_w_uq[i], a_q_g[i], a_k_g[i], q_lora, kv_lora)
            p = _normproj(xs, norm1_g[l], sc1, sh1, w_in_ext, group_of, emit_h=False)
            q, k, v = _mla_qkv(p, cos_a, sin_a, wq_ext, a_w_ukv[i].astype(BF16),
                               [a_q_lora_g[i].reshape(1, -1), a_kv_lora_g[i].reshape(1, -1)] + gains,
                               q_lora, kv_lora)
            att, att_ctx = _attention(q, k, v, batch=batch, seq=seq, ctx_len=ctx_len, kv_heads=MLA_HEADS,
                                      group=1, dk=2 * V7X_LANES, dv=MLA_V, need_ctx=need_ctx)
            z_col0 = q_lora + kv_lora + 2 * V7X_LANES
            xs = _mixout(att, att_ctx, e_w_o[i].astype(BF16), xs, g1, group_of,
                         conv=(p, z_col0, b_conv_w[i], b_conv_b[i]), t_lat=t_lat, seq=seq, ctx_len=ctx_len)
        else:
            p = _normproj(xs, norm1_g[l], sc1, sh1, c_w_qkv[i].astype(BF16), group_of, emit_h=False)
            q, k, v = _gqa_qkv(p, cos_c, sin_c, c_q_g[i], c_k_g[i])
            att, att_ctx = _attention(q, k, v, batch=batch, seq=seq, ctx_len=ctx_len, kv_heads=GQA_KV_HEADS,
                                      group=GQA_HEADS // GQA_KV_HEADS, dk=GQA_HD, dv=GQA_HD, need_ctx=need_ctx)
            xs = _mixout(att, att_ctx, c_w_o[i].astype(BF16), xs, g1, group_of,
                         t_lat=t_lat, seq=seq, ctx_len=ctx_len)
        qp, h2 = _normproj(xs, norm2_g[l], sc2, sh2, p_w_q[l].astype(BF16), group_of, emit_h=True)
        r1, p1, n0p0 = _route(qp, p_sub_keys[l].astype(BF16))
        xs = _peer(h2, u_all, vt_all, l, r1, p1, n0p0, xs, g2, group_of)
    return xs.reshape(batch, seq, d)
```

```python
import functools
import math

import jax
import jax.numpy as jnp
from jax import lax
from jax.experimental import pallas as pl
from jax.experimental.pallas import tpu as pltpu

F32 = jnp.float32
BF16 = jnp.bfloat16

EPS = 1e-6
ROPE_THETA = 10000.0
GRID_W = 64
MLA_HEADS = 8
MLA_NOPE = 128
MLA_ROPE = 64
MLA_V = 128
GQA_HEADS = 16
GQA_KV_HEADS = 4
GQA_HD = 128
PEER_HEADS = 8
N_KEYS = 128
PEER_TOPK = 16
N_MOD = 6

V7X_LANES = 128
V7X_SUBLANES = 8
V7X_VMEM_LIMIT_BYTES = 56 * 1024 * 1024

ROW_BLOCK = 512
MAX_COL_BLOCK = 4096
ATT_Q_ROWS = 2048
ATT_CHAIN_ROWS = 256
PEER_TB = 512
PEER_EB = 1024
PEER_ROWS = PEER_EB // N_KEYS
SQRT_HALF = 0.7071067811865476
NEG_INF = float("-inf")


def _cparams(sem):
    return pltpu.CompilerParams(dimension_semantics=sem, vmem_limit_bytes=V7X_VMEM_LIMIT_BYTES)


def _col_block(n):
    bn = n
    while bn > MAX_COL_BLOCK:
        assert bn % 2 == 0
        bn //= 2
    return bn


def _weight_spec(k, bn, n):
    if bn == n:
        return pl.BlockSpec((k, bn), lambda i, j: (0, 0), pipeline_mode=pl.Buffered(1))
    return pl.BlockSpec((k, bn), lambda i, j: (0, j))


def _nt_dot(a, b):
    return lax.dot_general(a, b, (((1,), (1,)), ((), ())), preferred_element_type=F32)


def _ada_kernel(c_ref, w_ref, b_ref, o_ref):
    c = c_ref[...]
    s = c / (1.0 + jnp.exp(-c))
    o_ref[0] = jnp.dot(s.astype(BF16), w_ref[0].astype(BF16), preferred_element_type=F32) + b_ref[0]


def _ada(cvec, ada_w, ada_b):
    depth, d, n = ada_w.shape
    nb = 1024
    rows = cvec.shape[0]
    return pl.pallas_call(
        _ada_kernel,
        grid=(depth, n // nb),
        in_specs=[
            pl.BlockSpec((rows, d), lambda l, j: (0, 0)),
            pl.BlockSpec((1, d, nb), lambda l, j: (l, 0, j)),
            pl.BlockSpec((1, 1, nb), lambda l, j: (l, 0, j)),
        ],
        out_specs=pl.BlockSpec((1, rows, nb), lambda l, j: (l, 0, j)),
        out_shape=jax.ShapeDtypeStruct((depth, rows, n), F32),
        compiler_params=_cparams(("parallel", "parallel")),
        name="ada_mod",
    )(cvec, ada_w, ada_b.reshape(depth, 1, n))


def _normproj_kernel(x_ref, g_ref, sc_ref, sh_ref, w_ref, o_ref, *rest, emit_h):
    hs_ref = rest[-1]

    @pl.when(pl.program_id(1) == 0)
    def _():
        x = x_ref[...]
        y = x * lax.rsqrt(jnp.mean(x * x, axis=-1, keepdims=True) + EPS) * g_ref[...]
        h = (y * (1.0 + sc_ref[0]) + sh_ref[0]).astype(BF16)
        hs_ref[...] = h
        if emit_h:
            rest[0][...] = pltpu.bitcast(h, jnp.uint32)

    out = jnp.dot(hs_ref[...], w_ref[...], preferred_element_type=F32)
    o_ref[...] = pltpu.bitcast(out.astype(BF16), jnp.uint32) if emit_h else out


def _normproj(x, g, sc, sh, w, group_of, emit_h):
    t, d = x.shape
    n = w.shape[1]
    bm, bn = ROW_BLOCK, _col_block(n)
    out_shape = [jax.ShapeDtypeStruct((t, n), F32)]
    out_specs = [pl.BlockSpec((bm, bn), lambda i, j: (i, j))]
    if emit_h:
        out_shape = [jax.ShapeDtypeStruct((t // 2, n), jnp.uint32)]
        out_specs = [pl.BlockSpec((bm // 2, bn), lambda i, j: (i, j))]
        out_shape.append(jax.ShapeDtypeStruct((t // 2, d), jnp.uint32))
        out_specs.append(pl.BlockSpec((bm // 2, d), lambda i, j: (i, 0)))
    res = pl.pallas_call(
        functools.partial(_normproj_kernel, emit_h=emit_h),
        grid=(t // bm, n // bn),
        in_specs=[
            pl.BlockSpec((bm, d), lambda i, j: (i, 0)),
            pl.BlockSpec((1, d), lambda i, j: (0, 0)),
            pl.BlockSpec((1, 1, d), lambda i, j: (group_of(i), 0, 0)),
            pl.BlockSpec((1, 1, d), lambda i, j: (group_of(i), 0, 0)),
            _weight_spec(d, bn, n),
        ],
        out_specs=out_specs,
        out_shape=out_shape,
        scratch_shapes=[pltpu.VMEM((bm, d), BF16)],
        compiler_params=_cparams(("parallel", "arbitrary")),
        name="normproj_h" if emit_h else "normproj",
    )(x, g.reshape(1, d), sc, sh, w)
    return res if emit_h else res[0]


def _mla_qkv_kernel(p_ref, cos_ref, sin_ref, wuq_ref, wukv_ref, qlg_ref, kvlg_ref,
                    qgn_ref, qgr_ref, qgs_ref, kgn_ref, kgr_ref, kgs_ref,
                    q_ref, k_ref, v_ref, *, q_lora, kv_lora, scale):
    qk_dim = MLA_NOPE + MLA_ROPE
    lanes = V7X_LANES
    cq = p_ref[:, 0:q_lora]
    ckv = p_ref[:, q_lora:q_lora + kv_lora]
    kr = p_ref[:, q_lora + kv_lora:q_lora + kv_lora + lanes]
    krs = p_ref[:, q_lora + kv_lora + lanes:q_lora + kv_lora + 2 * lanes]
    cos = cos_ref[...]
    sin = sin_ref[...]

    cqn = cq * lax.rsqrt(jnp.mean(cq * cq, axis=-1, keepdims=True) + EPS) * qlg_ref[...]
    qraw = jnp.dot(cqn.astype(BF16), wuq_ref[...], preferred_element_type=F32)
    ckvn = ckv * lax.rsqrt(jnp.mean(ckv * ckv, axis=-1, keepdims=True) + EPS) * kvlg_ref[...]
    kvraw = jnp.dot(ckvn.astype(BF16), wukv_ref[...], preferred_element_type=F32)

    kr_rot = kr * kgr_ref[...] * cos + krs * kgs_ref[...] * sin
    kr_ssq = jnp.sum(kr * kr, axis=-1, keepdims=True)
    for h in range(MLA_HEADS):
        nope = qraw[:, h * 3 * lanes:h * 3 * lanes + lanes]
        rope = qraw[:, h * 3 * lanes + lanes:h * 3 * lanes + 2 * lanes]
        rope_sw = qraw[:, h * 3 * lanes + 2 * lanes:h * 3 * lanes + 3 * lanes]
        ssq = jnp.sum(nope * nope, axis=-1, keepdims=True) + jnp.sum(rope * rope, axis=-1, keepdims=True)
        r = lax.rsqrt(ssq * (1.0 / qk_dim) + EPS) * scale
        q_ref[:, h * 2 * lanes:h * 2 * lanes + lanes] = (nope * r * qgn_ref[...]).astype(BF16)
        q_ref[:, h * 2 * lanes + lanes:(h + 1) * 2 * lanes] = (
            (rope * qgr_ref[...] * cos + rope_sw * qgs_ref[...] * sin) * r).astype(BF16)

        k_nope = kvraw[:, h * 2 * lanes:h * 2 * lanes + lanes]
        v = kvraw[:, h * 2 * lanes + lanes:(h + 1) * 2 * lanes]
        kssq = jnp.sum(k_nope * k_nope, axis=-1, keepdims=True) + kr_ssq
        rk = lax.rsqrt(kssq * (1.0 / qk_dim) + EPS)
        k_ref[:, h * 2 * lanes:h * 2 * lanes + lanes] = (k_nope * rk * kgn_ref[...]).astype(BF16)
        k_ref[:, h * 2 * lanes + lanes:(h + 1) * 2 * lanes] = (kr_rot * rk).astype(BF16)
        v_ref[:, h * 2 * lanes:h * 2 * lanes + lanes] = v.astype(BF16)
        v_ref[:, h * 2 * lanes + lanes:(h + 1) * 2 * lanes] = jnp.ones((v.shape[0], lanes), BF16)


def _mla_qkv(p, cos, sin, wuq, wukv, gains, q_lora, kv_lora):
    t = p.shape[0]
    bm = ROW_BLOCK
    lanes = V7X_LANES
    head_cols = q_lora + kv_lora + 2 * lanes
    full = lambda a: pl.BlockSpec(a.shape, lambda i: (0,) * a.ndim)
    return pl.pallas_call(
        functools.partial(_mla_qkv_kernel, q_lora=q_lora, kv_lora=kv_lora,
                          scale=float(MLA_NOPE + MLA_ROPE) ** -0.5),
        grid=(t // bm,),
        in_specs=[
            pl.BlockSpec((bm, head_cols), lambda i: (i, 0)),
            pl.BlockSpec((bm, lanes), lambda i: (i, 0)),
            pl.BlockSpec((bm, lanes), lambda i: (i, 0)),
            full(wuq), full(wukv)] + [full(g) for g in gains],
        out_specs=[
            pl.BlockSpec((bm, MLA_HEADS * 2 * lanes), lambda i: (i, 0)),
            pl.BlockSpec((bm, MLA_HEADS * 2 * lanes), lambda i: (i, 0)),
            pl.BlockSpec((bm, MLA_HEADS * 2 * lanes), lambda i: (i, 0)),
        ],
        out_shape=[
            jax.ShapeDtypeStruct((t, MLA_HEADS * 2 * lanes), BF16),
            jax.ShapeDtypeStruct((t, MLA_HEADS * 2 * lanes), BF16),
            jax.ShapeDtypeStruct((t, MLA_HEADS * 2 * lanes), BF16),
        ],
        compiler_params=_cparams(("parallel",)),
        name="mla_qkv",
    )(p, cos, sin, wuq, wukv, *gains)


def _gqa_qkv_kernel(p_ref, cos_ref, sin_ref, qg_ref, kg_ref, q_ref, k_ref, v_ref, *, scale):
    hd = GQA_HD
    cos = cos_ref[...]
    sin = sin_ref[...]

    def head(x, g):
        y = x * lax.rsqrt(jnp.mean(x * x, axis=-1, keepdims=True) + EPS) * g
        return y * cos + pltpu.roll(y, hd // 2, 1) * sin

    for h in range(GQA_HEADS):
        q_ref[:, h * hd:(h + 1) * hd] = (head(p_ref[:, h * hd:(h + 1) * hd], qg_ref[...]) * scale).astype(BF16)
    k0 = GQA_HEADS * hd
    v0 = k0 + GQA_KV_HEADS * hd
    for h in range(GQA_KV_HEADS):
        k_ref[:, h * hd:(h + 1) * hd] = head(p_ref[:, k0 + h * hd:k0 + (h + 1) * hd], kg_ref[...]).astype(BF16)
    for h in range(GQA_KV_HEADS):
        v_ref[:, h * 2 * hd:h * 2 * hd + hd] = p_ref[:, v0 + h * hd:v0 + (h + 1) * hd].astype(BF16)
        v_ref[:, h * 2 * hd + hd:(h + 1) * 2 * hd] = jnp.ones((v_ref.shape[0], hd), BF16)


def _gqa_qkv(p, cos, sin, qg, kg):
    t, n = p.shape
    bm = ROW_BLOCK
    hd = GQA_HD
    return pl.pallas_call(
        functools.partial(_gqa_qkv_kernel, scale=float(hd) ** -0.5),
        grid=(t // bm,),
        in_specs=[
            pl.BlockSpec((bm, n), lambda i: (i, 0)),
            pl.BlockSpec((bm, hd), lambda i: (i, 0)),
            pl.BlockSpec((bm, hd), lambda i: (i, 0)),
            pl.BlockSpec((1, hd), lambda i: (0, 0)),
            pl.BlockSpec((1, hd), lambda i: (0, 0)),
        ],
        out_specs=[
            pl.BlockSpec((bm, GQA_HEADS * hd), lambda i: (i, 0)),
            pl.BlockSpec((bm, GQA_KV_HEADS * hd), lambda i: (i, 0)),
            pl.BlockSpec((bm, GQA_KV_HEADS * 2 * hd), lambda i: (i, 0)),
        ],
        out_shape=[
            jax.ShapeDtypeStruct((t, GQA_HEADS * hd), BF16),
            jax.ShapeDtypeStruct((t, GQA_KV_HEADS * hd), BF16),
            jax.ShapeDtypeStruct((t, GQA_KV_HEADS * 2 * hd), BF16),
        ],
        compiler_params=_cparams(("parallel",)),
        name="gqa_qkv",
    )(p, cos, sin, qg.reshape(1, hd), kg.reshape(1, hd))


def _attn_kernel(q_ref, kc_ref, vc_ref, *rest, group, dk, dv, with_latent):
    if with_latent:
        kl_ref, vl_ref, o_ref = rest
    else:
        (o_ref,) = rest
    tq = q_ref.shape[0]
    chain_rows = min(ATT_CHAIN_ROWS, group * tq)
    per_head = tq // chain_rows if chain_rows < tq else 0
    for c in range(group * tq // chain_rows):
        if per_head:
            g, part = divmod(c, per_head)
            rows = slice(part * chain_rows, (part + 1) * chain_rows)
            q = q_ref[rows, g * dk:(g + 1) * dk]
            dst = [(rows, g, slice(0, chain_rows))]
        else:
            heads = range(c * chain_rows // tq, (c + 1) * chain_rows // tq)
            q = jnp.concatenate([q_ref[:, g * dk:(g + 1) * dk] for g in heads], axis=0)
            dst = [(slice(0, tq), g, slice(n * tq, (n + 1) * tq)) for n, g in enumerate(heads)]
        sc = _nt_dot(q, kc_ref[...])
        m = jnp.max(sc, axis=-1, keepdims=True)
        if with_latent:
            sl = _nt_dot(q, kl_ref[...])
            m = jnp.maximum(m, jnp.max(sl, axis=-1, keepdims=True))
        o = jnp.dot(jnp.exp((sc - m).astype(BF16)), vc_ref[...], preferred_element_type=F32)
        if with_latent:
            o = o + jnp.dot(jnp.exp((sl - m).astype(BF16)), vl_ref[...], preferred_element_type=F32)
        o = (o[:, :dv] / o[:, dv:dv + 1]).astype(BF16)
        for rows, g, src in dst:
            o_ref[rows, g * dv:(g + 1) * dv] = o[src, :]


def _attention(q, k, v, *, batch, seq, ctx_len, kv_heads, group, dk, dv, need_ctx):
    t_lat = batch * seq
    tq = min(ATT_Q_ROWS // group, seq)
    nq = seq // tq
    ctx_blk0 = t_lat // ctx_len
    common = dict(group=group, dk=dk, dv=dv)
    lat = pl.pallas_call(
        functools.partial(_attn_kernel, with_latent=True, **common),
        grid=(batch, kv_heads, nq),
        in_specs=[
            pl.BlockSpec((tq, group * dk), lambda b, h, j: (b * nq + j, h)),
            pl.BlockSpec((ctx_len, dk), lambda b, h, j: (ctx_blk0 + b, h)),
            pl.BlockSpec((ctx_len, 2 * dv), lambda b, h, j: (ctx_blk0 + b, h)),
            pl.BlockSpec((seq, dk), lambda b, h, j: (b, h)),
            pl.BlockSpec((seq, 2 * dv), lambda b, h, j: (b, h)),
        ],
        out_specs=pl.BlockSpec((tq, group * dv), lambda b, h, j: (b * nq + j, h)),
        out_shape=jax.ShapeDtypeStruct((t_lat, kv_heads * group * dv), BF16),
        compiler_params=_cparams(("parallel", "parallel", "arbitrary")),
        name="attn_latent",
    )(q, k, v, k, v)
    if not need_ctx:
        return lat, None
    ctx = pl.pallas_call(
        functools.partial(_attn_kernel, with_latent=False, **common),
        grid=(batch, kv_heads),
        in_specs=[
            pl.BlockSpec((ctx_len, group * dk), lambda b, h: (ctx_blk0 + b, h)),
            pl.BlockSpec((ctx_len, dk), lambda b, h: (ctx_blk0 + b, h)),
            pl.BlockSpec((ctx_len, 2 * dv), lambda b, h: (ctx_blk0 + b, h)),
        ],
        out_specs=pl.BlockSpec((ctx_len, group * dv), lambda b, h: (b, h)),
        out_shape=jax.ShapeDtypeStruct((batch * ctx_len, kv_heads * group * dv), BF16),
        compiler_params=_cparams(("parallel", "parallel")),
        name="attn_ctx",
    )(q, k, v)
    return lat, ctx


def _mixout_kernel(*refs, with_conv, with_ctx, t_lat, seq, ctx_len, a_width):
    a_ref, refs = refs[0], refs[1:]
    actx_ref = None
    if with_ctx:
        actx_ref, refs = refs[0], refs[1:]
    if with_conv:
        (bg_ref, cg_ref, hz_ref, cgp_ref, hzp_ref, cgn_ref, hzn_ref, cw_ref, cb_ref,
         w_ref, x_ref, gate_ref, o_ref, cs_ref) = refs
    else:
        w_ref, x_ref, gate_ref, o_ref = refs
    i = pl.program_id(0)

    if with_conv:
        @pl.when(pl.program_id(1) == 0)
        def _():
            bm = cg_ref.shape[0]
            u = cg_ref[...] * hz_ref[...]
            u_before = cgp_ref[V7X_SUBLANES - 1:V7X_SUBLANES, :] * hzp_ref[V7X_SUBLANES - 1:V7X_SUBLANES, :]
            u_after = cgn_ref[0:1, :] * hzn_ref[0:1, :]
            local = lax.broadcasted_iota(jnp.int32, (bm, 1), 0)
            row = local + i * bm
            in_lat = row < t_lat
            seg_pos = jnp.where(in_lat, jnp.bitwise_and(row, seq - 1), jnp.bitwise_and(row - t_lat, ctx_len - 1))
            seg_len = jnp.where(in_lat, seq, ctx_len)
            up = jnp.where(local == 0, u_before, pltpu.roll(u, 1, 0))
            up = jnp.where(seg_pos == 0, 0.0, up)
            un = jnp.where(local == bm - 1, u_after, pltpu.roll(u, bm - 1, 0))
            un = jnp.where(seg_pos == seg_len - 1, 0.0, un)
            y = up * cw_ref[0:1, :] + u * cw_ref[1:2, :] + un * cw_ref[2:3, :] + cb_ref[...]
            cs_ref[...] = (bg_ref[...] * y).astype(BF16)

    def finish(a):
        acc = jnp.dot(a, w_ref[0:a_width, :], preferred_element_type=F32)
        if with_conv:
            acc = acc + jnp.dot(cs_ref[...], w_ref[a_width:, :], preferred_element_type=F32)
        o_ref[...] = x_ref[...] + gate_ref[0] * acc

    if with_ctx:
        lat_blocks = t_lat // a_ref.shape[0]
        pl.when(i < lat_blocks)(lambda: finish(a_ref[...]))
        pl.when(i >= lat_blocks)(lambda: finish(actx_ref[...]))
    else:
        finish(a_ref[...])


def _mixout(a, a_ctx, w, x, gate, group_of, conv=None, *, t_lat, seq, ctx_len):
    d = x.shape[1]
    t = x.shape[0] if a_ctx is not None else t_lat
    bm, bn = ROW_BLOCK, _col_block(d)
    a_width = a.shape[1]
    lat_blocks = t_lat // bm
    kw = dict(t_lat=t_lat, seq=seq, ctx_len=ctx_len, a_width=a_width, with_ctx=a_ctx is not None)
    in_specs = [pl.BlockSpec((bm, a_width), lambda i, j: (jnp.minimum(i, lat_blocks - 1), 0))]
    args = [a]
    if a_ctx is not None:
        in_specs.append(pl.BlockSpec((bm, a_width), lambda i, j: (jnp.maximum(i - lat_blocks, 0), 0)))
        args.append(a_ctx)
    scratch = []
    if conv is not None:
        p, z_col0, cw, cb = conv
        cwid = cw.shape[1]
        assert z_col0 % cwid == 0
        zb = z_col0 // cwid
        sub = V7X_SUBLANES
        last_halo = p.shape[0] // sub - 1
        prev_idx = lambda i: jnp.maximum(i * (bm // sub) - 1, 0)
        next_idx = lambda i: jnp.minimum((i + 1) * (bm // sub), last_halo)
        in_specs += [
            pl.BlockSpec((bm, cwid), lambda i, j: (i, zb)),
            pl.BlockSpec((bm, cwid), lambda i, j: (i, zb + 1)),
            pl.BlockSpec((bm, cwid), lambda i, j: (i, zb + 2)),
            pl.BlockSpec((sub, cwid), lambda i, j: (prev_idx(i), zb + 1)),
            pl.BlockSpec((sub, cwid), lambda i, j: (prev_idx(i), zb + 2)),
            pl.BlockSpec((sub, cwid), lambda i, j: (next_idx(i), zb + 1)),
            pl.BlockSpec((sub, cwid), lambda i, j: (next_idx(i), zb + 2)),
            pl.BlockSpec(cw.shape, lambda i, j: (0, 0)),
            pl.BlockSpec((1, cwid), lambda i, j: (0, 0)),
        ]
        args += [p] * 7 + [cw, cb.reshape(1, cwid)]
        scratch = [pltpu.VMEM((bm, cwid), BF16)]
    in_specs += [
        _weight_spec(w.shape[0], bn, d),
        pl.BlockSpec((bm, bn), lambda i, j: (i, j)),
        pl.BlockSpec((1, 1, bn), lambda i, j: (group_of(i), 0, j)),
    ]
    args += [w, x, gate]
    return pl.pallas_call(
        functools.partial(_mixout_kernel, with_conv=conv is not None, **kw),
        grid=(t // bm, d // bn),
        in_specs=in_specs,
        out_specs=pl.BlockSpec((bm, bn), lambda i, j: (i, j)),
        out_shape=jax.ShapeDtypeStruct((t, d), F32),
        scratch_shapes=scratch,
        compiler_params=_cparams(("parallel", "arbitrary")),
        name="mixout_conv" if conv is not None else "mixout",
    )(*args)


_CAND_PAIRS = [(a, b) for a in range(PEER_TOPK) for b in range(PEER_TOPK // (a + 1))]
_CAND_ROWS = -(-len(_CAND_PAIRS) // V7X_SUBLANES) * V7X_SUBLANES
BF16_ROWS = 2 * V7X_SUBLANES


def _oddeven_sort_pairs(n):
    pairs = []

    def merge(lo, hi, r):
        step = r * 2
        if step < hi - lo:
            merge(lo, hi, step)
            merge(lo + r, hi, step)
            pairs.extend((i, i + r) for i in range(lo + r, hi - r, step))
        else:
            pairs.append((lo, lo + r))

    def sort(lo, hi):
        if hi - lo >= 1:
            mid = lo + (hi - lo) // 2
            sort(lo, mid)
            sort(mid + 1, hi)
            merge(lo, hi, 1)

    sort(0, n - 1)
    return pairs


_SORT_PAIRS = _oddeven_sort_pairs(PEER_TOPK)


def _top_sorted(s):
    sub = V7X_SUBLANES
    k = PEER_TOPK
    assert s.shape[0] == sub * k
    v = [s[r * sub:(r + 1) * sub, :] for r in range(k)]

    def exchange(i, j):
        v[i], v[j] = jnp.maximum(v[i], v[j]), jnp.minimum(v[i], v[j])

    for i, j in _SORT_PAIRS:
        exchange(i, j)
    shift = sub // 2
    while shift:
        v = [jnp.maximum(v[r], pltpu.roll(v[k - 1 - r], shift, 0)) for r in range(k)]
        dist = k // 2
        while dist:
            for i in range(k):
                if not i & dist:
                    exchange(i, i + dist)
            dist //= 2
        shift //= 2
    return v


def _bf16_pair_words(v):
    bits = pltpu.bitcast(v.astype(BF16).astype(F32), jnp.uint32)
    return bits | (bits >> 16)


def _route_kernel(q_ref, sk_ref, r1_ref, p1_ref, np_ref, cand_ref):
    nk = N_KEYS
    k_top = PEER_TOPK
    lanes = V7X_LANES
    q = pltpu.bitcast(q_ref[...], BF16)
    tb = q.shape[0]
    cand_ref[len(_CAND_PAIRS):, :] = jnp.full((_CAND_ROWS - len(_CAND_PAIRS), lanes), NEG_INF, F32)
    for c in range(tb // lanes):
        cs = slice(c * lanes, (c + 1) * lanes)
        s0, s1 = (_nt_dot(sk_ref[0, half], q[cs, half * nk:(half + 1) * nk])
                  for half in range(2))

        tops0 = [v[0:1, :] for v in _top_sorted(s0)]
        tops1 = [v[0:1, :] for v in _top_sorted(s1)]
        rank = jnp.full((nk, lanes), float(k_top), F32)
        for b in reversed(range(k_top)):
            rank = jnp.where(s1 >= tops1[b], float(b), rank)

        for r, (a, b) in enumerate(_CAND_PAIRS):
            cand_ref[r:r + 1, :] = tops0[a] + tops1[b]
        cand = cand_ref[...]
        top = tops0[0] + tops1[0]
        z = jnp.zeros((1, lanes), F32)
        for k in range(k_top):
            tau = jnp.max(cand, axis=0, keepdims=True)
            z = z + jnp.exp(tau - top)
            cand = jnp.where(cand == tau, NEG_INF, cand)

        n0 = jnp.zeros((nk, lanes), F32)
        for b in range(k_top):
            n0 = jnp.where(s0 + tops1[b] >= tau, float(b + 1), n0)
        n0 = _bf16_pair_words(n0)
        p0 = _bf16_pair_words(jnp.exp(s0 - tops0[0]))
        for g in range(nk // PEER_ROWS):
            np_ref[0, g, 0:PEER_ROWS, cs] = n0[g * PEER_ROWS:(g + 1) * PEER_ROWS, :]
            np_ref[0, g, PEER_ROWS:, cs] = p0[g * PEER_ROWS:(g + 1) * PEER_ROWS, :]
        rank_b = rank.astype(BF16)
        p1_b = (jnp.exp(s1 - tops1[0]) / z).astype(BF16)
        for g in range(nk // BF16_ROWS):
            r1_ref[0, g, :, cs] = pltpu.bitcast(rank_b[g * BF16_ROWS:(g + 1) * BF16_ROWS, :], jnp.int32)
            p1_ref[0, g, :, cs] = pltpu.bitcast(p1_b[g * BF16_ROWS:(g + 1) * BF16_ROWS, :], jnp.int32)


def _route(qp, sub_keys):
    t = 2 * qp.shape[0]
    tb = PEER_TB
    nk = N_KEYS
    heads = PEER_HEADS
    packed = jax.ShapeDtypeStruct((heads, nk // BF16_ROWS, V7X_SUBLANES, t), jnp.int32)
    packed_spec = pl.BlockSpec((1, nk // BF16_ROWS, V7X_SUBLANES, tb), lambda i, h: (h, 0, 0, i))
    plain = jax.ShapeDtypeStruct((heads, nk // PEER_ROWS, 2 * PEER_ROWS, t), jnp.uint32)
    plain_spec = pl.BlockSpec((1, nk // PEER_ROWS, 2 * PEER_ROWS, tb), lambda i, h: (h, 0, 0, i))
    return pl.pallas_call(
        _route_kernel,
        grid=(t // tb, heads),
        in_specs=[
            pl.BlockSpec((tb // 2, 2 * nk), lambda i, h: (i, h)),
            pl.BlockSpec((1, 2, nk, sub_keys.shape[-1]), lambda i, h: (h, 0, 0, 0)),
        ],
        out_specs=[packed_spec, packed_spec, plain_spec],
        out_shape=[packed, packed, plain],
        scratch_shapes=[pltpu.VMEM((_CAND_ROWS, V7X_LANES), F32)],
        compiler_params=_cparams(("parallel", "arbitrary")),
        name="peer_route",
    )(qp, sub_keys)


def _peer_kernel(h_ref, u_ref, vt_ref, r1_ref, p1_ref, np_ref, x_ref, gate_ref,
                 o_ref, acc_ref, at0_ref, at1_ref, hs_ref, *, n_blocks):
    j = pl.program_id(1)
    nk = N_KEYS
    lanes = V7X_LANES
    eb, tb = at0_ref.shape
    at_refs = (at0_ref, at1_ref)

    def key_row(hd, row, cs):
        words = jnp.broadcast_to(np_ref[hd, 0, row:row + 1, cs], (V7X_SUBLANES, lanes))
        return pltpu.bitcast(words, BF16)[None]

    def gate_rows(il, prev):
        for c in range(tb // lanes):
            cs = slice(c * lanes, (c + 1) * lanes)
            w = None
            for hd in range(PEER_HEADS):
                r1 = pltpu.bitcast(r1_ref[hd, :, :, cs], BF16)
                p1 = pltpu.bitcast(p1_ref[hd, :, :, cs], BF16)
                term = jnp.where(r1 < key_row(hd, il, cs), p1 * key_row(hd, PEER_ROWS + il, cs),
                                 jnp.zeros((), BF16))
                w = term if w is None else w + term
            a = at_refs[prev][il * nk:(il + 1) * nk, cs]
            gelu = (0.5 * a * (1.0 + lax.erf(a * SQRT_HALF))).astype(BF16)
            for g in range(nk // BF16_ROWS):
                r0 = il * nk + g * BF16_ROWS
                hs_ref[r0:r0 + BF16_ROWS, cs] = w[g] * gelu[g * BF16_ROWS:(g + 1) * BF16_ROWS, :]

    def first_matmul(cur):
        at_refs[cur][...] = _nt_dot(pltpu.bitcast(u_ref[0], BF16), pltpu.bitcast(h_ref[...], BF16))

    def mix(prev):
        for il in range(PEER_ROWS):
            gate_rows(il, prev)
        acc_ref[...] += jnp.dot(pltpu.bitcast(vt_ref[0, 0], BF16), hs_ref[...], preferred_element_type=F32)

    @pl.when(j == 0)
    def _():
        acc_ref[...] = jnp.zeros_like(acc_ref)
        first_matmul(0)

    for parity in range(2):
        @pl.when(jnp.logical_and(jnp.logical_and(j > 0, j < n_blocks), j % 2 == parity))
        def _(parity=parity):
            first_matmul(parity)
            mix(1 - parity)

    @pl.when(j == n_blocks)
    def _():
        mix((n_blocks - 1) % 2)
        o_ref[...] = x_ref[...] + gate_ref[0] * acc_ref[...].T


def _peer(h, u, vt, layer, r1, p1, n0p0, x, gate, group_of):
    t, d = x.shape
    tb, eb = PEER_TB, PEER_EB
    n_blocks = vt.shape[1]
    nk = N_KEYS
    heads = PEER_HEADS
    packed_spec = pl.BlockSpec((heads, nk // BF16_ROWS, V7X_SUBLANES, tb),
                               lambda i, j: (0, 0, 0, jnp.where(j >= 1, i, jnp.maximum(i - 1, 0))))
    plain_spec = pl.BlockSpec((heads, 1, 2 * PEER_ROWS, tb), lambda i, j: (0, jnp.clip(j - 1, 0, n_blocks - 1), 0, i))
    last_tok = t // tb - 1
    h_spec = pl.BlockSpec((tb // 2, d), lambda i, j: (jnp.minimum(jnp.where(j == n_blocks, i + 1, i), last_tok), 0))
    x_spec = pl.BlockSpec((tb, d), lambda i, j: (jnp.where(j >= n_blocks - 1, i, jnp.maximum(i - 1, 0)), 0))
    return pl.pallas_call(
        functools.partial(_peer_kernel, n_blocks=n_blocks),
        grid=(t // tb, n_blocks + 1),
        in_specs=[
            h_spec,
            pl.BlockSpec((1, eb // 2, d), lambda i, j: (layer, jnp.minimum(j, n_blocks - 1), 0)),
            pl.BlockSpec((1, 1, d // 2, eb), lambda i, j: (layer, jnp.maximum(j - 1, 0), 0, 0)),
            packed_spec, packed_spec, plain_spec,
            x_spec,
            pl.BlockSpec((1, 1, d), lambda i, j: (group_of(i), 0, 0)),
        ],
        out_specs=pl.BlockSpec((tb, d), lambda i, j: (i, 0)),
        out_shape=jax.ShapeDtypeStruct((t, d), F32),
        scratch_shapes=[
            pltpu.VMEM((d, tb), F32),
            pltpu.VMEM((eb, tb), F32),
            pltpu.VMEM((eb, tb), F32),
            pltpu.VMEM((eb, tb), BF16),
        ],
        compiler_params=_cparams(("parallel", "arbitrary")),
        name="peer_mix",
    )(h, u, vt, r1, p1, n0p0, x, gate)


def _pack_table_kernel(x_ref, o_ref, *, transpose):
    x = x_ref[0]
    if transpose:
        o_ref[0, 0] = pltpu.bitcast(x.T.astype(BF16), jnp.uint32)
    else:
        o_ref[0] = pltpu.bitcast(x.astype(BF16), jnp.uint32)


def _pack_table(tab, transpose):
    layers, ne, d = tab.shape
    eb = PEER_EB
    if transpose:
        out_shape = jax.ShapeDtypeStruct((layers, ne // eb, d // 2, eb), jnp.uint32)
        out_spec = pl.BlockSpec((1, 1, d // 2, eb), lambda l, j: (l, j, 0, 0))
    else:
        out_shape = jax.ShapeDtypeStruct((layers, ne // 2, d), jnp.uint32)
        out_spec = pl.BlockSpec((1, eb // 2, d), lambda l, j: (l, j, 0))
    return pl.pallas_call(
        functools.partial(_pack_table_kernel, transpose=transpose),
        grid=(layers, ne // eb),
        in_specs=[pl.BlockSpec((1, eb, d), lambda l, j: (l, j, 0))],
        out_specs=out_spec,
        out_shape=out_shape,
        compiler_params=_cparams(("parallel", "parallel")),
        name="pack_table_t" if transpose else "pack_table",
    )(tab)


def _rope_tables(seq, rope_dim, batch, ctx_rows):
    rows = seq // GRID_W
    row = jnp.repeat(jnp.arange(rows, dtype=F32), GRID_W)
    col = jnp.tile(jnp.arange(GRID_W, dtype=F32), rows)
    quarter = rope_dim // 4
    freqs = ROPE_THETA ** (-jnp.arange(quarter, dtype=F32) / quarter)
    ang = jnp.concatenate([row[:, None] * freqs, col[:, None] * freqs], axis=-1)
    cos, sin = jnp.cos(ang), jnp.sin(ang)
    pad = V7X_LANES - rope_dim
    cos_l = jnp.pad(jnp.concatenate([cos, cos], axis=-1), ((0, 0), (0, pad)))
    sin_l = jnp.pad(jnp.concatenate([-sin, sin], axis=-1), ((0, 0), (0, pad)))
    cos_c = jnp.pad(jnp.ones((ctx_rows, rope_dim), F32), ((0, 0), (0, pad)))
    sin_c = jnp.zeros((ctx_rows, V7X_LANES), F32)
    return (jnp.concatenate([jnp.tile(cos_l, (batch, 1)), cos_c], axis=0),
            jnp.concatenate([jnp.tile(sin_l, (batch, 1)), sin_c], axis=0))


def _swap_halves(a):
    half = a.shape[-1] // 2
    return jnp.concatenate([a[..., half:], a[..., :half]], axis=-1)


def _pad_lanes(a):
    return jnp.pad(a, [(0, 0)] * (a.ndim - 1) + [(0, V7X_LANES - a.shape[-1])])


def _mla_weights(w_in, w_uq, q_g, k_g, q_lora, kv_lora):
    d = w_in.shape[0]
    c_kr = q_lora + kv_lora
    w_kr = w_in[:, c_kr:c_kr + MLA_ROPE]
    w_in_ext = jnp.concatenate([
        w_in[:, :c_kr], _pad_lanes(w_kr), _pad_lanes(_swap_halves(w_kr)), w_in[:, c_kr + MLA_ROPE:]], axis=1)
    wq = w_uq.reshape(q_lora, MLA_HEADS, MLA_NOPE + MLA_ROPE)
    wq_rope = wq[:, :, MLA_NOPE:]
    wq_ext = jnp.concatenate([wq[:, :, :MLA_NOPE], _pad_lanes(wq_rope), _pad_lanes(_swap_halves(wq_rope))], axis=-1)
    wq_ext = wq_ext.reshape(q_lora, MLA_HEADS * 3 * V7X_LANES)

    def gains(g):
        g_r = g[MLA_NOPE:]
        return [g[:MLA_NOPE].reshape(1, -1), _pad_lanes(g_r).reshape(1, -1), _pad_lanes(_swap_halves(g_r)).reshape(1, -1)]

    return w_in_ext.astype(BF16), wq_ext.astype(BF16), gains(q_g) + gains(k_g)


def kernel(x, c, ctx, c_ctx, ada_w, ada_b, norm1_g, norm2_g, a_w_in, a_q_lora_g, a_kv_lora_g, a_w_uq, a_w_ukv, a_q_g, a_k_g, b_conv_w, b_conv_b, e_w_o, c_w_qkv, c_q_g, c_k_g, c_w_o, p_w_q, p_sub_keys, p_u, p_v):
    batch, seq, d = x.shape
    ctx_len = ctx.shape[1]
    depth = ada_w.shape[0]
    q_lora = a_q_lora_g.shape[1]
    kv_lora = a_kv_lora_g.shape[1]
    conv_width = b_conv_w.shape[2]
    t_lat = batch * seq
    t_ctx = batch * ctx_len
    t = t_lat + t_ctx
    bm = ROW_BLOCK
    assert seq % bm == 0 and t_ctx % bm == 0 and t % PEER_TB == 0 and seq % min(ATT_Q_ROWS, seq) == 0
    assert seq & (seq - 1) == 0 and ctx_len & (ctx_len - 1) == 0 and seq % GRID_W == 0
    assert batch + 1 <= V7X_SUBLANES

    blocks_per_batch = seq // bm
    group_of = lambda i: jnp.minimum(i // blocks_per_batch, batch)

    cvec = jnp.zeros((V7X_SUBLANES, d), F32).at[:batch].set(c).at[batch].set(c_ctx)
    mods = _ada(cvec, ada_w, ada_b)
    mods = mods.reshape(depth, V7X_SUBLANES, N_MOD, 1, d).transpose(0, 2, 1, 3, 4)

    cos_a, sin_a = _rope_tables(seq, MLA_ROPE, batch, t_ctx)
    cos_c, sin_c = _rope_tables(seq, GQA_HD, batch, t_ctx)

    u_all = _pack_table(p_u, transpose=False)
    vt_all = _pack_table(p_v, transpose=True)
    xs = jnp.concatenate([x.reshape(t_lat, d), ctx.reshape(t_ctx, d)], axis=0)
    for l in range(depth):
        sh1, sc1, g1, sh2, sc2, g2 = (mods[l, k] for k in range(N_MOD))
        i = l // 2
        need_ctx = l < depth - 1
        if l % 2 == 0:
            w_in_ext, wq_ext, gains = _mla_weights(a_w_in[i], a_w_uq[i], a_q_g[i], a_k_g[i], q_lora, kv_lora)
            p = _normproj(xs, norm1_g[l], sc1, sh1, w_in_ext, group_of, emit_h=False)
            q, k, v = _mla_qkv(p, cos_a, sin_a, wq_ext, a_w_ukv[i].astype(BF16),
                               [a_q_lora_g[i].reshape(1, -1), a_kv_lora_g[i].reshape(1, -1)] + gains,
                               q_lora, kv_lora)
            att, att_ctx = _attention(q, k, v, batch=batch, seq=seq, ctx_len=ctx_len, kv_heads=MLA_HEADS,
                                      group=1, dk=2 * V7X_LANES, dv=MLA_V, need_ctx=need_ctx)
            z_col0 = q_lora + kv_lora + 2 * V7X_LANES
            xs = _mixout(att, att_ctx, e_w_o[i].astype(BF16), xs, g1, group_of,
                         conv=(p, z_col0, b_conv_w[i], b_conv_b[i]), t_lat=t_lat, seq=seq, ctx_len=ctx_len)
        else:
            p = _normproj(xs, norm1_g[l], sc1, sh1, c_w_qkv[i].astype(BF16), group_of, emit_h=False)
            q, k, v = _gqa_qkv(p, cos_c, sin_c, c_q_g[i], c_k_g[i])
            att, att_ctx = _attention(q, k, v, batch=batch, seq=seq, ctx_len=ctx_len, kv_heads=GQA_KV_HEADS,
                                      group=GQA_HEADS // GQA_KV_HEADS, dk=GQA_HD, dv=GQA_HD, need_ctx=need_ctx)
            xs = _mixout(att, att_ctx, c_w_o[i].astype(BF16), xs, g1, group_of,
                         t_lat=t_lat, seq=seq, ctx_len=ctx_len)
        qp, h2 = _normproj(xs, norm2_g[l], sc2, sh2, p_w_q[l].astype(BF16), group_of, emit_h=True)
        r1, p1, n0p0 = _route(qp, p_sub_keys[l].astype(BF16))
        xs = _peer(h2, u_all, vt_all, l, r1, p1, n0p0, xs, g2, group_of)
    return xs.reshape(batch, seq, d)
```

```python
import functools
import math

import jax
import jax.numpy as jnp
from jax import lax
from jax.experimental import pallas as pl
from jax.experimental.pallas import tpu as pltpu

F32 = jnp.float32
BF16 = jnp.bfloat16

EPS = 1e-6
ROPE_THETA = 10000.0
GRID_W = 64
MLA_HEADS = 8
MLA_NOPE = 128
MLA_ROPE = 64
MLA_V = 128
GQA_HEADS = 16
GQA_KV_HEADS = 4
GQA_HD = 128
PEER_HEADS = 8
N_KEYS = 128
PEER_TOPK = 16
N_MOD = 6

V7X_LANES = 128
V7X_SUBLANES = 8
V7X_VMEM_LIMIT_BYTES = 56 * 1024 * 1024

ROW_BLOCK = 512
MAX_COL_BLOCK = 4096
ATT_Q_ROWS = 2048
ATT_CHAIN_ROWS = 256
PEER_TB = 512
PEER_EB = 1024
PEER_ROWS = PEER_EB // N_KEYS
ROUTE_HEADS = 2
SQRT_HALF = 0.7071067811865476
NEG_INF = float("-inf")


def _cparams(sem):
    return pltpu.CompilerParams(dimension_semantics=sem, vmem_limit_bytes=V7X_VMEM_LIMIT_BYTES)


def _col_block(n):
    bn = n
    while bn > MAX_COL_BLOCK:
        assert bn % 2 == 0
        bn //= 2
    return bn


def _weight_spec(k, bn, n):
    if bn == n:
        return pl.BlockSpec((k, bn), lambda i, j: (0, 0), pipeline_mode=pl.Buffered(1))
    return pl.BlockSpec((k, bn), lambda i, j: (0, j))


def _nt_dot(a, b):
    return lax.dot_general(a, b, (((1,), (1,)), ((), ())), preferred_element_type=F32)


def _ada_kernel(c_ref, w_ref, b_ref, o_ref):
    c = c_ref[...]
    s = c / (1.0 + jnp.exp(-c))
    o_ref[0] = jnp.dot(s.astype(BF16), w_ref[0].astype(BF16), preferred_element_type=F32) + b_ref[0]


def _ada(cvec, ada_w, ada_b):
    depth, d, n = ada_w.shape
    nb = 2048
    rows = cvec.shape[0]
    return pl.pallas_call(
        _ada_kernel,
        grid=(depth, n // nb),
        in_specs=[
            pl.BlockSpec((rows, d), lambda l, j: (0, 0)),
            pl.BlockSpec((1, d, nb), lambda l, j: (l, 0, j)),
            pl.BlockSpec((1, 1, nb), lambda l, j: (l, 0, j)),
        ],
        out_specs=pl.BlockSpec((1, rows, nb), lambda l, j: (l, 0, j)),
        out_shape=jax.ShapeDtypeStruct((depth, rows, n), F32),
        compiler_params=_cparams(("parallel", "parallel")),
        name="ada_mod",
    )(cvec, ada_w, ada_b.reshape(depth, 1, n))


def _normproj_kernel(x_ref, g_ref, sc_ref, sh_ref, w_ref, o_ref, *rest, emit_h):
    hs_ref = rest[-1]

    @pl.when(pl.program_id(1) == 0)
    def _():
        x = x_ref[...]
        y = x * lax.rsqrt(jnp.mean(x * x, axis=-1, keepdims=True) + EPS) * g_ref[...]
        h = (y * (1.0 + sc_ref[0]) + sh_ref[0]).astype(BF16)
        hs_ref[...] = h
        if emit_h:
            rest[0][...] = pltpu.bitcast(h, jnp.uint32)

    out = jnp.dot(hs_ref[...], w_ref[...], preferred_element_type=F32)
    o_ref[...] = pltpu.bitcast(out.astype(BF16), jnp.uint32) if emit_h else out


def _normproj(x, g, sc, sh, w, group_of, emit_h):
    t, d = x.shape
    n = w.shape[1]
    bm, bn = ROW_BLOCK, _col_block(n)
    out_shape = [jax.ShapeDtypeStruct((t, n), F32)]
    out_specs = [pl.BlockSpec((bm, bn), lambda i, j: (i, j))]
    if emit_h:
        out_shape = [jax.ShapeDtypeStruct((t // 2, n), jnp.uint32)]
        out_specs = [pl.BlockSpec((bm // 2, bn), lambda i, j: (i, j))]
        out_shape.append(jax.ShapeDtypeStruct((t // 2, d), jnp.uint32))
        out_specs.append(pl.BlockSpec((bm // 2, d), lambda i, j: (i, 0)))
    res = pl.pallas_call(
        functools.partial(_normproj_kernel, emit_h=emit_h),
        grid=(t // bm, n // bn),
        in_specs=[
            pl.BlockSpec((bm, d), lambda i, j: (i, 0)),
            pl.BlockSpec((1, d), lambda i, j: (0, 0)),
            pl.BlockSpec((1, 1, d), lambda i, j: (group_of(i), 0, 0)),
            pl.BlockSpec((1, 1, d), lambda i, j: (group_of(i), 0, 0)),
            _weight_spec(d, bn, n),
        ],
        out_specs=out_specs,
        out_shape=out_shape,
        scratch_shapes=[pltpu.VMEM((bm, d), BF16)],
        compiler_params=_cparams(("parallel", "arbitrary")),
        name="normproj_h" if emit_h else "normproj",
    )(x, g.reshape(1, d), sc, sh, w)
    return res if emit_h else res[0]


def _mla_qkv_kernel(p_ref, cos_ref, sin_ref, wuq_ref, wukv_ref, qlg_ref, kvlg_ref,
                    qgn_ref, qgr_ref, qgs_ref, kgn_ref, kgr_ref, kgs_ref,
                    q_ref, k_ref, v_ref, *, q_lora, kv_lora, scale):
    qk_dim = MLA_NOPE + MLA_ROPE
    lanes = V7X_LANES
    cq = p_ref[:, 0:q_lora]
    ckv = p_ref[:, q_lora:q_lora + kv_lora]
    kr = p_ref[:, q_lora + kv_lora:q_lora + kv_lora + lanes]
    krs = p_ref[:, q_lora + kv_lora + lanes:q_lora + kv_lora + 2 * lanes]
    cos = cos_ref[...]
    sin = sin_ref[...]

    cqn = cq * lax.rsqrt(jnp.mean(cq * cq, axis=-1, keepdims=True) + EPS) * qlg_ref[...]
    qraw = jnp.dot(cqn.astype(BF16), wuq_ref[...], preferred_element_type=F32)
    ckvn = ckv * lax.rsqrt(jnp.mean(ckv * ckv, axis=-1, keepdims=True) + EPS) * kvlg_ref[...]
    kvraw = jnp.dot(ckvn.astype(BF16), wukv_ref[...], preferred_element_type=F32)

    kr_rot = kr * kgr_ref[...] * cos + krs * kgs_ref[...] * sin
    kr_ssq = jnp.sum(kr * kr, axis=-1, keepdims=True)
    for h in range(MLA_HEADS):
        nope = qraw[:, h * 3 * lanes:h * 3 * lanes + lanes]
        rope = qraw[:, h * 3 * lanes + lanes:h * 3 * lanes + 2 * lanes]
        rope_sw = qraw[:, h * 3 * lanes + 2 * lanes:h * 3 * lanes + 3 * lanes]
        ssq = jnp.sum(nope * nope, axis=-1, keepdims=True) + jnp.sum(rope * rope, axis=-1, keepdims=True)
        r = lax.rsqrt(ssq * (1.0 / qk_dim) + EPS) * scale
        q_ref[:, h * 2 * lanes:h * 2 * lanes + lanes] = (nope * r * qgn_ref[...]).astype(BF16)
        q_ref[:, h * 2 * lanes + lanes:(h + 1) * 2 * lanes] = (
            (rope * qgr_ref[...] * cos + rope_sw * qgs_ref[...] * sin) * r).astype(BF16)

        k_nope = kvraw[:, h * 2 * lanes:h * 2 * lanes + lanes]
        v = kvraw[:, h * 2 * lanes + lanes:(h + 1) * 2 * lanes]
        kssq = jnp.sum(k_nope * k_nope, axis=-1, keepdims=True) + kr_ssq
        rk = lax.rsqrt(kssq * (1.0 / qk_dim) + EPS)
        k_ref[:, h * 2 * lanes:h * 2 * lanes + lanes] = (k_nope * rk * kgn_ref[...]).astype(BF16)
        k_ref[:, h * 2 * lanes + lanes:(h + 1) * 2 * lanes] = (kr_rot * rk).astype(BF16)
        v_ref[:, h * 2 * lanes:h * 2 * lanes + lanes] = v.astype(BF16)
        v_ref[:, h * 2 * lanes + lanes:(h + 1) * 2 * lanes] = jnp.ones((v.shape[0], lanes), BF16)


def _mla_qkv(p, cos, sin, wuq, wukv, gains, q_lora, kv_lora):
    t = p.shape[0]
    bm = ROW_BLOCK
    lanes = V7X_LANES
    head_cols = q_lora + kv_lora + 2 * lanes
    full = lambda a: pl.BlockSpec(a.shape, lambda i: (0,) * a.ndim)
    return pl.pallas_call(
        functools.partial(_mla_qkv_kernel, q_lora=q_lora, kv_lora=kv_lora,
                          scale=float(MLA_NOPE + MLA_ROPE) ** -0.5),
        grid=(t // bm,),
        in_specs=[
            pl.BlockSpec((bm, head_cols), lambda i: (i, 0)),
            pl.BlockSpec((bm, lanes), lambda i: (i, 0)),
            pl.BlockSpec((bm, lanes), lambda i: (i, 0)),
            full(wuq), full(wukv)] + [full(g) for g in gains],
        out_specs=[
            pl.BlockSpec((bm, MLA_HEADS * 2 * lanes), lambda i: (i, 0)),
            pl.BlockSpec((bm, MLA_HEADS * 2 * lanes), lambda i: (i, 0)),
            pl.BlockSpec((bm, MLA_HEADS * 2 * lanes), lambda i: (i, 0)),
        ],
        out_shape=[
            jax.ShapeDtypeStruct((t, MLA_HEADS * 2 * lanes), BF16),
            jax.ShapeDtypeStruct((t, MLA_HEADS * 2 * lanes), BF16),
            jax.ShapeDtypeStruct((t, MLA_HEADS * 2 * lanes), BF16),
        ],
        compiler_params=_cparams(("parallel",)),
        name="mla_qkv",
    )(p, cos, sin, wuq, wukv, *gains)


def _gqa_qkv_kernel(p_ref, cos_ref, sin_ref, qg_ref, kg_ref, q_ref, k_ref, v_ref, *, scale):
    hd = GQA_HD
    cos = cos_ref[...]
    sin = sin_ref[...]

    def head(x, g):
        y = x * lax.rsqrt(jnp.mean(x * x, axis=-1, keepdims=True) + EPS) * g
        return y * cos + pltpu.roll(y, hd // 2, 1) * sin

    for h in range(GQA_HEADS):
        q_ref[:, h * hd:(h + 1) * hd] = (head(p_ref[:, h * hd:(h + 1) * hd], qg_ref[...]) * scale).astype(BF16)
    k0 = GQA_HEADS * hd
    v0 = k0 + GQA_KV_HEADS * hd
    for h in range(GQA_KV_HEADS):
        k_ref[:, h * hd:(h + 1) * hd] = head(p_ref[:, k0 + h * hd:k0 + (h + 1) * hd], kg_ref[...]).astype(BF16)
    for h in range(GQA_KV_HEADS):
        v_ref[:, h * 2 * hd:h * 2 * hd + hd] = p_ref[:, v0 + h * hd:v0 + (h + 1) * hd].astype(BF16)
        v_ref[:, h * 2 * hd + hd:(h + 1) * 2 * hd] = jnp.ones((v_ref.shape[0], hd), BF16)


def _gqa_qkv(p, cos, sin, qg, kg):
    t, n = p.shape
    bm = ROW_BLOCK
    hd = GQA_HD
    return pl.pallas_call(
        functools.partial(_gqa_qkv_kernel, scale=float(hd) ** -0.5),
        grid=(t // bm,),
        in_specs=[
            pl.BlockSpec((bm, n), lambda i: (i, 0)),
            pl.BlockSpec((bm, hd), lambda i: (i, 0)),
            pl.BlockSpec((bm, hd), lambda i: (i, 0)),
            pl.BlockSpec((1, hd), lambda i: (0, 0)),
            pl.BlockSpec((1, hd), lambda i: (0, 0)),
        ],
        out_specs=[
            pl.BlockSpec((bm, GQA_HEADS * hd), lambda i: (i, 0)),
            pl.BlockSpec((bm, GQA_KV_HEADS * hd), lambda i: (i, 0)),
            pl.BlockSpec((bm, GQA_KV_HEADS * 2 * hd), lambda i: (i, 0)),
        ],
        out_shape=[
            jax.ShapeDtypeStruct((t, GQA_HEADS * hd), BF16),
            jax.ShapeDtypeStruct((t, GQA_KV_HEADS * hd), BF16),
            jax.ShapeDtypeStruct((t, GQA_KV_HEADS * 2 * hd), BF16),
        ],
        compiler_params=_cparams(("parallel",)),
        name="gqa_qkv",
    )(p, cos, sin, qg.reshape(1, hd), kg.reshape(1, hd))


def _attn_kernel(q_ref, kc_ref, vc_ref, *rest, group, dk, dv, with_latent):
    if with_latent:
        kl_ref, vl_ref, o_ref = rest
    else:
        (o_ref,) = rest
    tq = q_ref.shape[0]
    chain_rows = min(ATT_CHAIN_ROWS, group * tq)
    per_head = tq // chain_rows if chain_rows < tq else 0
    for c in range(group * tq // chain_rows):
        if per_head:
            g, part = divmod(c, per_head)
            rows = slice(part * chain_rows, (part + 1) * chain_rows)
            q = q_ref[rows, g * dk:(g + 1) * dk]
            dst = [(rows, g, slice(0, chain_rows))]
        else:
            heads = range(c * chain_rows // tq, (c + 1) * chain_rows // tq)
            q = jnp.concatenate([q_ref[:, g * dk:(g + 1) * dk] for g in heads], axis=0)
            dst = [(slice(0, tq), g, slice(n * tq, (n + 1) * tq)) for n, g in enumerate(heads)]
        sc = _nt_dot(q, kc_ref[...])
        m = jnp.max(sc, axis=-1, keepdims=True)
        if with_latent:
            sl = _nt_dot(q, kl_ref[...])
            m = jnp.maximum(m, jnp.max(sl, axis=-1, keepdims=True))
        o = jnp.dot(jnp.exp((sc - m).astype(BF16)), vc_ref[...], preferred_element_type=F32)
        if with_latent:
            o = o + jnp.dot(jnp.exp((sl - m).astype(BF16)), vl_ref[...], preferred_element_type=F32)
        o = (o[:, :dv] / o[:, dv:dv + 1]).astype(BF16)
        for rows, g, src in dst:
            o_ref[rows, g * dv:(g + 1) * dv] = o[src, :]


def _attention(q, k, v, *, batch, seq, ctx_len, kv_heads, group, dk, dv, need_ctx):
    t_lat = batch * seq
    tq = min(ATT_Q_ROWS // group, seq)
    nq = seq // tq
    ctx_blk0 = t_lat // ctx_len
    common = dict(group=group, dk=dk, dv=dv)
    lat = pl.pallas_call(
        functools.partial(_attn_kernel, with_latent=True, **common),
        grid=(batch, kv_heads, nq),
        in_specs=[
            pl.BlockSpec((tq, group * dk), lambda b, h, j: (b * nq + j, h)),
            pl.BlockSpec((ctx_len, dk), lambda b, h, j: (ctx_blk0 + b, h)),
            pl.BlockSpec((ctx_len, 2 * dv), lambda b, h, j: (ctx_blk0 + b, h)),
            pl.BlockSpec((seq, dk), lambda b, h, j: (b, h)),
            pl.BlockSpec((seq, 2 * dv), lambda b, h, j: (b, h)),
        ],
        out_specs=pl.BlockSpec((tq, group * dv), lambda b, h, j: (b * nq + j, h)),
        out_shape=jax.ShapeDtypeStruct((t_lat, kv_heads * group * dv), BF16),
        compiler_params=_cparams(("parallel", "parallel", "arbitrary")),
        name="attn_latent",
    )(q, k, v, k, v)
    if not need_ctx:
        return lat, None
    ctx = pl.pallas_call(
        functools.partial(_attn_kernel, with_latent=False, **common),
        grid=(batch, kv_heads),
        in_specs=[
            pl.BlockSpec((ctx_len, group * dk), lambda b, h: (ctx_blk0 + b, h)),
            pl.BlockSpec((ctx_len, dk), lambda b, h: (ctx_blk0 + b, h)),
            pl.BlockSpec((ctx_len, 2 * dv), lambda b, h: (ctx_blk0 + b, h)),
        ],
        out_specs=pl.BlockSpec((ctx_len, group * dv), lambda b, h: (b, h)),
        out_shape=jax.ShapeDtypeStruct((batch * ctx_len, kv_heads * group * dv), BF16),
        compiler_params=_cparams(("parallel", "parallel")),
        name="attn_ctx",
    )(q, k, v)
    return lat, ctx


def _mixout_kernel(*refs, with_conv, with_ctx, t_lat, seq, ctx_len, a_width):
    a_ref, refs = refs[0], refs[1:]
    actx_ref = None
    if with_ctx:
        actx_ref, refs = refs[0], refs[1:]
    if with_conv:
        (bg_ref, cg_ref, hz_ref, cgp_ref, hzp_ref, cgn_ref, hzn_ref, cw_ref, cb_ref,
         w_ref, x_ref, gate_ref, o_ref, cs_ref) = refs
    else:
        w_ref, x_ref, gate_ref, o_ref = refs
    i = pl.program_id(0)

    if with_conv:
        @pl.when(pl.program_id(1) == 0)
        def _():
            bm = cg_ref.shape[0]
            u = cg_ref[...] * hz_ref[...]
            u_before = cgp_ref[V7X_SUBLANES - 1:V7X_SUBLANES, :] * hzp_ref[V7X_SUBLANES - 1:V7X_SUBLANES, :]
            u_after = cgn_ref[0:1, :] * hzn_ref[0:1, :]
            local = lax.broadcasted_iota(jnp.int32, (bm, 1), 0)
            row = local + i * bm
            in_lat = row < t_lat
            seg_pos = jnp.where(in_lat, jnp.bitwise_and(row, seq - 1), jnp.bitwise_and(row - t_lat, ctx_len - 1))
            seg_len = jnp.where(in_lat, seq, ctx_len)
            up = jnp.where(local == 0, u_before, pltpu.roll(u, 1, 0))
            up = jnp.where(seg_pos == 0, 0.0, up)
            un = jnp.where(local == bm - 1, u_after, pltpu.roll(u, bm - 1, 0))
            un = jnp.where(seg_pos == seg_len - 1, 0.0, un)
            y = up * cw_ref[0:1, :] + u * cw_ref[1:2, :] + un * cw_ref[2:3, :] + cb_ref[...]
            cs_ref[...] = (bg_ref[...] * y).astype(BF16)

    def finish(a):
        acc = jnp.dot(a, w_ref[0:a_width, :], preferred_element_type=F32)
        if with_conv:
            acc = acc + jnp.dot(cs_ref[...], w_ref[a_width:, :], preferred_element_type=F32)
        o_ref[...] = x_ref[...] + gate_ref[0] * acc

    if with_ctx:
        lat_blocks = t_lat // a_ref.shape[0]
        pl.when(i < lat_blocks)(lambda: finish(a_ref[...]))
        pl.when(i >= lat_blocks)(lambda: finish(actx_ref[...]))
    else:
        finish(a_ref[...])


def _mixout(a, a_ctx, w, x, gate, group_of, conv=None, *, t_lat, seq, ctx_len):
    d = x.shape[1]
    t = x.shape[0] if a_ctx is not None else t_lat
    bm, bn = ROW_BLOCK, _col_block(d)
    a_width = a.shape[1]
    lat_blocks = t_lat // bm
    kw = dict(t_lat=t_lat, seq=seq, ctx_len=ctx_len, a_width=a_width, with_ctx=a_ctx is not None)
    in_specs = [pl.BlockSpec((bm, a_width), lambda i, j: (jnp.minimum(i, lat_blocks - 1), 0))]
    args = [a]
    if a_ctx is not None:
        in_specs.append(pl.BlockSpec((bm, a_width), lambda i, j: (jnp.maximum(i - lat_blocks, 0), 0)))
        args.append(a_ctx)
    scratch = []
    if conv is not None:
        p, z_col0, cw, cb = conv
        cwid = cw.shape[1]
        assert z_col0 % cwid == 0
        zb = z_col0 // cwid
        sub = V7X_SUBLANES
        last_halo = p.shape[0] // sub - 1
        prev_idx = lambda i: jnp.maximum(i * (bm // sub) - 1, 0)
        next_idx = lambda i: jnp.minimum((i + 1) * (bm // sub), last_halo)
        in_specs += [
            pl.BlockSpec((bm, cwid), lambda i, j: (i, zb)),
            pl.BlockSpec((bm, cwid), lambda i, j: (i, zb + 1)),
            pl.BlockSpec((bm, cwid), lambda i, j: (i, zb + 2)),
            pl.BlockSpec((sub, cwid), lambda i, j: (prev_idx(i), zb + 1)),
            pl.BlockSpec((sub, cwid), lambda i, j: (prev_idx(i), zb + 2)),
            pl.BlockSpec((sub, cwid), lambda i, j: (next_idx(i), zb + 1)),
            pl.BlockSpec((sub, cwid), lambda i, j: (next_idx(i), zb + 2)),
            pl.BlockSpec(cw.shape, lambda i, j: (0, 0)),
            pl.BlockSpec((1, cwid), lambda i, j: (0, 0)),
        ]
        args += [p] * 7 + [cw, cb.reshape(1, cwid)]
        scratch = [pltpu.VMEM((bm, cwid), BF16)]
    in_specs += [
        _weight_spec(w.shape[0], bn, d),
        pl.BlockSpec((bm, bn), lambda i, j: (i, j)),
        pl.BlockSpec((1, 1, bn), lambda i, j: (group_of(i), 0, j)),
    ]
    args += [w, x, gate]
    return pl.pallas_call(
        functools.partial(_mixout_kernel, with_conv=conv is not None, **kw),
        grid=(t // bm, d // bn),
        in_specs=in_specs,
        out_specs=pl.BlockSpec((bm, bn), lambda i, j: (i, j)),
        out_shape=jax.ShapeDtypeStruct((t, d), F32),
        scratch_shapes=scratch,
        compiler_params=_cparams(("parallel", "arbitrary")),
        name="mixout_conv" if conv is not None else "mixout",
    )(*args)


_CAND_PAIRS = [(a, b) for a in range(PEER_TOPK) for b in range(PEER_TOPK // (a + 1))]
_CAND_ROWS = -(-len(_CAND_PAIRS) // V7X_SUBLANES) * V7X_SUBLANES
BF16_ROWS = 2 * V7X_SUBLANES


def _oddeven_sort_pairs(n):
    pairs = []

    def merge(lo, hi, r):
        step = r * 2
        if step < hi - lo:
            merge(lo, hi, step)
            merge(lo + r, hi, step)
            pairs.extend((i, i + r) for i in range(lo + r, hi - r, step))
        else:
            pairs.append((lo, lo + r))

    def sort(lo, hi):
        if hi - lo >= 1:
            mid = lo + (hi - lo) // 2
            sort(lo, mid)
            sort(mid + 1, hi)
            merge(lo, hi, 1)

    sort(0, n - 1)
    return pairs


_SORT_PAIRS = _oddeven_sort_pairs(PEER_TOPK)


def _top_sorted(s):
    sub = V7X_SUBLANES
    k = PEER_TOPK
    assert s.shape[0] == sub * k
    v = [s[r * sub:(r + 1) * sub, :] for r in range(k)]

    def exchange(i, j):
        v[i], v[j] = jnp.maximum(v[i], v[j]), jnp.minimum(v[i], v[j])

    for i, j in _SORT_PAIRS:
        exchange(i, j)
    shift = sub // 2
    while shift:
        v = [jnp.maximum(v[r], pltpu.roll(v[k - 1 - r], shift, 0)) for r in range(k)]
        dist = k // 2
        while dist:
            for i in range(k):
                if not i & dist:
                    exchange(i, i + dist)
            dist //= 2
        shift //= 2
    return v


def _bf16_pair_words(v):
    bits = pltpu.bitcast(v.astype(BF16).astype(F32), jnp.uint32)
    return bits | (bits >> 16)


def _route_kernel(q_ref, sk_ref, r1_ref, p1_ref, np_ref, cand_ref):
    nk = N_KEYS
    k_top = PEER_TOPK
    lanes = V7X_LANES
    q = pltpu.bitcast(q_ref[...], BF16)
    tb = q.shape[0]
    cand_ref[len(_CAND_PAIRS):, :] = jnp.full((_CAND_ROWS - len(_CAND_PAIRS), lanes), NEG_INF, F32)
    for hh, c in [(hh, c) for hh in range(ROUTE_HEADS) for c in range(tb // lanes)]:
        cs = slice(c * lanes, (c + 1) * lanes)
        s0, s1 = (_nt_dot(sk_ref[hh, half], q[cs, (2 * hh + half) * nk:(2 * hh + half + 1) * nk])
                  for half in range(2))

        tops0 = [v[0:1, :] for v in _top_sorted(s0)]
        tops1 = [v[0:1, :] for v in _top_sorted(s1)]
        rank = jnp.full((nk, lanes), float(k_top), F32)
        for b in reversed(range(k_top)):
            rank = jnp.where(s1 >= tops1[b], float(b), rank)

        for r, (a, b) in enumerate(_CAND_PAIRS):
            cand_ref[r:r + 1, :] = tops0[a] + tops1[b]
        cand = cand_ref[...]
        top = tops0[0] + tops1[0]
        z = jnp.zeros((1, lanes), F32)
        for k in range(k_top):
            tau = jnp.max(cand, axis=0, keepdims=True)
            z = z + jnp.exp(tau - top)
            cand = jnp.where(cand == tau, NEG_INF, cand)

        n0 = jnp.zeros((nk, lanes), F32)
        for b in range(k_top):
            n0 = jnp.where(s0 + tops1[b] >= tau, float(b + 1), n0)
        n0 = _bf16_pair_words(n0)
        p0 = _bf16_pair_words(jnp.exp(s0 - tops0[0]))
        for g in range(nk // PEER_ROWS):
            np_ref[hh, g, 0:PEER_ROWS, cs] = n0[g * PEER_ROWS:(g + 1) * PEER_ROWS, :]
            np_ref[hh, g, PEER_ROWS:, cs] = p0[g * PEER_ROWS:(g + 1) * PEER_ROWS, :]
        rank_b = rank.astype(BF16)
        p1_b = (jnp.exp(s1 - tops1[0]) / z).astype(BF16)
        for g in range(nk // BF16_ROWS):
            r1_ref[hh, g, :, cs] = pltpu.bitcast(rank_b[g * BF16_ROWS:(g + 1) * BF16_ROWS, :], jnp.int32)
            p1_ref[hh, g, :, cs] = pltpu.bitcast(p1_b[g * BF16_ROWS:(g + 1) * BF16_ROWS, :], jnp.int32)


def _route(qp, sub_keys):
    t = 2 * qp.shape[0]
    tb = PEER_TB
    nk = N_KEYS
    heads = PEER_HEADS
    packed = jax.ShapeDtypeStruct((heads, nk // BF16_ROWS, V7X_SUBLANES, t), jnp.int32)
    packed_spec = pl.BlockSpec((ROUTE_HEADS, nk // BF16_ROWS, V7X_SUBLANES, tb), lambda i, h: (h, 0, 0, i))
    plain = jax.ShapeDtypeStruct((heads, nk // PEER_ROWS, 2 * PEER_ROWS, t), jnp.uint32)
    plain_spec = pl.BlockSpec((ROUTE_HEADS, nk // PEER_ROWS, 2 * PEER_ROWS, tb), lambda i, h: (h, 0, 0, i))
    return pl.pallas_call(
        _route_kernel,
        grid=(t // tb, heads // ROUTE_HEADS),
        in_specs=[
            pl.BlockSpec((tb // 2, ROUTE_HEADS * 2 * nk), lambda i, h: (i, h)),
            pl.BlockSpec((ROUTE_HEADS, 2, nk, sub_keys.shape[-1]), lambda i, h: (h, 0, 0, 0)),
        ],
        out_specs=[packed_spec, packed_spec, plain_spec],
        out_shape=[packed, packed, plain],
        scratch_shapes=[pltpu.VMEM((_CAND_ROWS, V7X_LANES), F32)],
        compiler_params=_cparams(("parallel", "arbitrary")),
        name="peer_route",
    )(qp, sub_keys)


def _peer_kernel(h_ref, u_ref, vt_ref, r1_ref, p1_ref, np_ref, x_ref, gate_ref,
                 o_ref, acc_ref, at0_ref, at1_ref, hs_ref, *, n_blocks):
    j = pl.program_id(1)
    nk = N_KEYS
    lanes = V7X_LANES
    eb, tb = at0_ref.shape
    at_refs = (at0_ref, at1_ref)

    def key_row(hd, row, cs):
        words = jnp.broadcast_to(np_ref[hd, 0, row:row + 1, cs], (V7X_SUBLANES, lanes))
        return pltpu.bitcast(words, BF16)[None]

    def gate_rows(il, prev):
        for c in range(tb // lanes):
            cs = slice(c * lanes, (c + 1) * lanes)
            w = None
            for hd in range(PEER_HEADS):
                r1 = pltpu.bitcast(r1_ref[hd, :, :, cs], BF16)
                p1 = pltpu.bitcast(p1_ref[hd, :, :, cs], BF16)
                term = jnp.where(r1 < key_row(hd, il, cs), p1 * key_row(hd, PEER_ROWS + il, cs),
                                 jnp.zeros((), BF16))
                w = term if w is None else w + term
            a = at_refs[prev][il * nk:(il + 1) * nk, cs]
            gelu = (0.5 * a * (1.0 + lax.erf(a * SQRT_HALF))).astype(BF16)
            for g in range(nk // BF16_ROWS):
                r0 = il * nk + g * BF16_ROWS
                hs_ref[r0:r0 + BF16_ROWS, cs] = w[g] * gelu[g * BF16_ROWS:(g + 1) * BF16_ROWS, :]

    def first_matmul(cur):
        at_refs[cur][...] = _nt_dot(pltpu.bitcast(u_ref[0], BF16), pltpu.bitcast(h_ref[...], BF16))

    def mix(prev):
        for il in range(PEER_ROWS):
            gate_rows(il, prev)
        acc_ref[...] += jnp.dot(pltpu.bitcast(vt_ref[0, 0], BF16), hs_ref[...], preferred_element_type=F32)

    @pl.when(j == 0)
    def _():
        acc_ref[...] = jnp.zeros_like(acc_ref)
        first_matmul(0)

    for parity in range(2):
        @pl.when(jnp.logical_and(jnp.logical_and(j > 0, j < n_blocks), j % 2 == parity))
        def _(parity=parity):
            first_matmul(parity)
            mix(1 - parity)

    @pl.when(j == n_blocks)
    def _():
        mix((n_blocks - 1) % 2)
        o_ref[...] = x_ref[...] + gate_ref[0] * acc_ref[...].T


def _peer(h, u, vt, layer, r1, p1, n0p0, x, gate, group_of):
    t, d = x.shape
    tb, eb = PEER_TB, PEER_EB
    n_blocks = vt.shape[1]
    nk = N_KEYS
    heads = PEER_HEADS
    packed_spec = pl.BlockSpec((heads, nk // BF16_ROWS, V7X_SUBLANES, tb), lambda i, j: (0, 0, 0, i))
    plain_spec = pl.BlockSpec((heads, 1, 2 * PEER_ROWS, tb), lambda i, j: (0, jnp.clip(j - 1, 0, n_blocks - 1), 0, i))
    last_tok = t // tb - 1
    h_spec = pl.BlockSpec((tb // 2, d), lambda i, j: (jnp.minimum(jnp.where(j == n_blocks, i + 1, i), last_tok), 0))
    x_spec = pl.BlockSpec((tb, d), lambda i, j: (jnp.where(j >= n_blocks - 1, i, jnp.maximum(i - 1, 0)), 0))
    return pl.pallas_call(
        functools.partial(_peer_kernel, n_blocks=n_blocks),
        grid=(t // tb, n_blocks + 1),
        in_specs=[
            h_spec,
            pl.BlockSpec((1, eb // 2, d), lambda i, j: (layer, jnp.minimum(j, n_blocks - 1), 0)),
            pl.BlockSpec((1, 1, d // 2, eb), lambda i, j: (layer, jnp.maximum(j - 1, 0), 0, 0)),
            packed_spec, packed_spec, plain_spec,
            x_spec,
            pl.BlockSpec((1, 1, d), lambda i, j: (group_of(i), 0, 0)),
        ],
        out_specs=pl.BlockSpec((tb, d), lambda i, j: (i, 0)),
        out_shape=jax.ShapeDtypeStruct((t, d), F32),
        scratch_shapes=[
            pltpu.VMEM((d, tb), F32),
            pltpu.VMEM((eb, tb), F32),
            pltpu.VMEM((eb, tb), F32),
            pltpu.VMEM((eb, tb), BF16),
        ],
        compiler_params=_cparams(("parallel", "arbitrary")),
        name="peer_mix",
    )(h, u, vt, r1, p1, n0p0, x, gate)


def _pack_table_kernel(x_ref, o_ref, *, transpose):
    x = x_ref[0]
    if transpose:
        o_ref[0, 0] = pltpu.bitcast(x.T.astype(BF16), jnp.uint32)
    else:
        o_ref[0] = pltpu.bitcast(x.astype(BF16), jnp.uint32)


def _pack_table(tab, transpose):
    layers, ne, d = tab.shape
    eb = PEER_EB
    if transpose:
        out_shape = jax.ShapeDtypeStruct((layers, ne // eb, d // 2, eb), jnp.uint32)
        out_spec = pl.BlockSpec((1, 1, d // 2, eb), lambda l, j: (l, j, 0, 0))
    else:
        out_shape = jax.ShapeDtypeStruct((layers, ne // 2, d), jnp.uint32)
        out_spec = pl.BlockSpec((1, eb // 2, d), lambda l, j: (l, j, 0))
    return pl.pallas_call(
        functools.partial(_pack_table_kernel, transpose=transpose),
        grid=(layers, ne // eb),
        in_specs=[pl.BlockSpec((1, eb, d), lambda l, j: (l, j, 0))],
        out_specs=out_spec,
        out_shape=out_shape,
        compiler_params=_cparams(("parallel", "parallel")),
        name="pack_table_t" if transpose else "pack_table",
    )(tab)


def _rope_tables(seq, rope_dim, batch, ctx_rows):
    rows = seq // GRID_W
    row = jnp.repeat(jnp.arange(rows, dtype=F32), GRID_W)
    col = jnp.tile(jnp.arange(GRID_W, dtype=F32), rows)
    quarter = rope_dim // 4
    freqs = ROPE_THETA ** (-jnp.arange(quarter, dtype=F32) / quarter)
    ang = jnp.concatenate([row[:, None] * freqs, col[:, None] * freqs], axis=-1)
    cos, sin = jnp.cos(ang), jnp.sin(ang)
    pad = V7X_LANES - rope_dim
    cos_l = jnp.pad(jnp.concatenate([cos, cos], axis=-1), ((0, 0), (0, pad)))
    sin_l = jnp.pad(jnp.concatenate([-sin, sin], axis=-1), ((0, 0), (0, pad)))
    cos_c = jnp.pad(jnp.ones((ctx_rows, rope_dim), F32), ((0, 0), (0, pad)))
    sin_c = jnp.zeros((ctx_rows, V7X_LANES), F32)
    return (jnp.concatenate([jnp.tile(cos_l, (batch, 1)), cos_c], axis=0),
            jnp.concatenate([jnp.tile(sin_l, (batch, 1)), sin_c], axis=0))


def _swap_halves(a):
    half = a.shape[-1] // 2
    return jnp.concatenate([a[..., half:], a[..., :half]], axis=-1)


def _pad_lanes(a):
    return jnp.pad(a, [(0, 0)] * (a.ndim - 1) + [(0, V7X_LANES - a.shape[-1])])


def _mla_weights(w_in, w_uq, q_g, k_g, q_lora, kv_lora):
    d = w_in.shape[0]
    c_kr = q_lora + kv_lora
    w_kr = w_in[:, c_kr:c_kr + MLA_ROPE]
    w_in_ext = jnp.concatenate([
        w_in[:, :c_kr], _pad_lanes(w_kr), _pad_lanes(_swap_halves(w_kr)), w_in[:, c_kr + MLA_ROPE:]], axis=1)
    wq = w_uq.reshape(q_lora, MLA_HEADS, MLA_NOPE + MLA_ROPE)
    wq_rope = wq[:, :, MLA_NOPE:]
    wq_ext = jnp.concatenate([wq[:, :, :MLA_NOPE], _pad_lanes(wq_rope), _pad_lanes(_swap_halves(wq_rope))], axis=-1)
    wq_ext = wq_ext.reshape(q_lora, MLA_HEADS * 3 * V7X_LANES)

    def gains(g):
        g_r = g[MLA_NOPE:]
        return [g[:MLA_NOPE].reshape(1, -1), _pad_lanes(g_r).reshape(1, -1), _pad_lanes(_swap_halves(g_r)).reshape(1, -1)]

    return w_in_ext.astype(BF16), wq_ext.astype(BF16), gains(q_g) + gains(k_g)


def kernel(x, c, ctx, c_ctx, ada_w, ada_b, norm1_g, norm2_g, a_w_in, a_q_lora_g, a_kv_lora_g, a_w_uq, a_w_ukv, a_q_g, a_k_g, b_conv_w, b_conv_b, e_w_o, c_w_qkv, c_q_g, c_k_g, c_w_o, p_w_q, p_sub_keys, p_u, p_v):
    batch, seq, d = x.shape
    ctx_len = ctx.shape[1]
    depth = ada_w.shape[0]
    q_lora = a_q_lora_g.shape[1]
    kv_lora = a_kv_lora_g.shape[1]
    conv_width = b_conv_w.shape[2]
    t_lat = batch * seq
    t_ctx = batch * ctx_len
    t = t_lat + t_ctx
    bm = ROW_BLOCK
    assert seq % bm == 0 and t_ctx % bm == 0 and t % PEER_TB == 0 and seq % min(ATT_Q_ROWS, seq) == 0
    assert seq & (seq - 1) == 0 and ctx_len & (ctx_len - 1) == 0 and seq % GRID_W == 0
    assert batch + 1 <= V7X_SUBLANES

    blocks_per_batch = seq // bm
    group_of = lambda i: jnp.minimum(i // blocks_per_batch, batch)

    cvec = jnp.zeros((V7X_SUBLANES, d), F32).at[:batch].set(c).at[batch].set(c_ctx)
    mods = _ada(cvec, ada_w, ada_b)
    mods = mods.reshape(depth, V7X_SUBLANES, N_MOD, 1, d).transpose(0, 2, 1, 3, 4)

    cos_a, sin_a = _rope_tables(seq, MLA_ROPE, batch, t_ctx)
    cos_c, sin_c = _rope_tables(seq, GQA_HD, batch, t_ctx)

    u_all = _pack_table(p_u, transpose=False)
    vt_all = _pack_table(p_v, transpose=True)
    xs = jnp.concatenate([x.reshape(t_lat, d), ctx.reshape(t_ctx, d)], axis=0)
    for l in range(depth):
        sh1, sc1, g1, sh2, sc2, g2 = (mods[l, k] for k in range(N_MOD))
        i = l // 2
        need_ctx = l < depth - 1
        if l % 2 == 0:
            w_in_ext, wq_ext, gains = _mla_weights(a_w_in[i], a_w_uq[i], a_q_g[i], a_k_g[i], q_lora, kv_lora)
            p = _normproj(xs, norm1_g[l], sc1, sh1, w_in_ext, group_of, emit_h=False)
            q, k, v = _mla_qkv(p, cos_a, sin_a, wq_ext, a_w_ukv[i].astype(BF16),
                               [a_q_lora_g[i].reshape(1, -1), a_kv_lora_g[i].reshape(1, -1)] + gains,
                               q_lora, kv_lora)
            att, att_ctx = _attention(q, k, v, batch=batch, seq=seq, ctx_len=ctx_len, kv_heads=MLA_HEADS,
                                      group=1, dk=2 * V7X_LANES, dv=MLA_V, need_ctx=need_ctx)
            z_col0 = q_lora + kv_lora + 2 * V7X_LANES
            xs = _mixout(att, att_ctx, e_w_o[i].astype(BF16), xs, g1, group_of,
                         conv=(p, z_col0, b_conv_w[i], b_conv_b[i]), t_lat=t_lat, seq=seq, ctx_len=ctx_len)
        else:
            p = _normproj(xs, norm1_g[l], sc1, sh1, c_w_qkv[i].astype(BF16), group_of, emit_h=False)
            q, k, v = _gqa_qkv(p, cos_c, sin_c, c_q_g[i], c_k_g[i])
            att, att_ctx = _attention(q, k, v, batch=batch, seq=seq, ctx_len=ctx_len, kv_heads=GQA_KV_HEADS,
                                      group=GQA_HEADS // GQA_KV_HEADS, dk=GQA_HD, dv=GQA_HD, need_ctx=need_ctx)
            xs = _mixout(att, att_ctx, c_w_o[i].astype(BF16), xs, g1, group_of,
                         t_lat=t_lat, seq=seq, ctx_len=ctx_len)
        qp, h2 = _normproj(xs, norm2_g[l], sc2, sh2, p_w_q[l].astype(BF16), group_of, emit_h=True)
        r1, p1, n0p0 = _route(qp, p_sub_keys[l].astype(BF16))
        xs = _peer(h2, u_all, vt_all, l, r1, p1, n0p0, xs, g2, group_of)
    return xs.reshape(batch, seq, d)
```
